```python
import math
import jax, jax.numpy as jnp
from jax import lax
import numpy as np


D_MODEL = 1024
BATCH = 2
SEQ = 16384
DEPTH = 4

N_META = 16
N_MIXERS = 3
S5_WIDTH = D_MODEL
S5_GROUP_CH = 16
S5_GROUPS = S5_WIDTH // S5_GROUP_CH
S5_STATE = 64
S5_MIN_STEP = 1e-3
S5_MAX_STEP = 1e-1
RET_HEADS = 4
RET_DK = D_MODEL // RET_HEADS
RET_DV = 2 * RET_DK
RET_CHUNK = 128
RET_THETA = 10000.0
DIFF_HEADS = D_MODEL // 128
DIFF_DQK = 64
DIFF_DV = 2 * DIFF_DQK
Q_BLOCK = 128
N_EXPERTS = 32
TOP_K = 4
D_EXPERT = D_MODEL
SWIGLU_LIMIT = 7.0
SWIGLU_ALPHA = 1.702
MOE_BLOCK = 256
LN_EPS = 1e-5
MASK_VALUE = -1e30
DEEPNORM_ALPHA = (2 * DEPTH) ** 0.25
DEEPNORM_BETA = (8 * DEPTH) ** -0.25
N_S5_LAYERS = (DEPTH + 2) // 3
N_RET_LAYERS = (DEPTH + 1) // 3
N_DIFF_LAYERS = DEPTH // 3

kernel_name = "hybrid_s5_retnet_diffattn_moe_trunk"


def _layer_norm(x, g, b):
    xf = x.astype(jnp.float32)
    mu = jnp.mean(xf, axis=-1, keepdims=True)
    var = jnp.mean(jnp.square(xf - mu), axis=-1, keepdims=True)
    y = (xf - mu) * lax.rsqrt(var + LN_EPS) * g.astype(jnp.float32) + b.astype(jnp.float32)
    return y.astype(x.dtype)


def _s5_mixer(h, w_in, lam_re, lam_im, log_step, b_re, b_im, c_re, c_im, d_skip, w_glu, b_glu, w_out):
    B_, L, _ = h.shape
    u = (h @ w_in).astype(jnp.float32).reshape(B_, L, S5_GROUPS, S5_GROUP_CH)
    dt = jnp.exp(log_step.astype(jnp.float32))[:, None]
    lr = jnp.minimum(lam_re.astype(jnp.float32), -1e-4)
    li = lam_im.astype(jnp.float32)
    mag = jnp.exp(lr * dt)
    ab_re = mag * jnp.cos(li * dt)
    ab_im = mag * jnp.sin(li * dt)
    den = lr * lr + li * li
    f_re = ((ab_re - 1.0) * lr + ab_im * li) / den
    f_im = (ab_im * lr - (ab_re - 1.0) * li) / den
    br = b_re.astype(jnp.float32)
    bi = b_im.astype(jnp.float32)
    bbar_re = f_re[..., None] * br - f_im[..., None] * bi
    bbar_im = f_re[..., None] * bi + f_im[..., None] * br
    bu_re = jnp.einsum('blgc,gnc->lbgn', u, bbar_re)
    bu_im = jnp.einsum('blgc,gnc->lbgn', u, bbar_im)
    a_re = jnp.broadcast_to(ab_re[None, None], (L, 1, S5_GROUPS, S5_STATE))
    a_im = jnp.broadcast_to(ab_im[None, None], (L, 1, S5_GROUPS, S5_STATE))

    def combine(e1, e2):
        a1r, a1i, b1r, b1i = e1
        a2r, a2i, b2r, b2i = e2
        return (a2r * a1r - a2i * a1i,
                a2r * a1i + a2i * a1r,
                a2r * b1r - a2i * b1i + b2r,
                a2r * b1i + a2i * b1r + b2i)

    _, _, xr, xi = lax.associative_scan(combine, (a_re, a_im, bu_re, bu_im), axis=0)
    y = (jnp.einsum('lbgn,gcn->blgc', xr, c_re.astype(jnp.float32))
         - jnp.einsum('lbgn,gcn->blgc', xi, c_im.astype(jnp.float32))
         + d_skip.astype(jnp.float32).reshape(S5_GROUPS, S5_GROUP_CH) * u)
    z = jax.nn.gelu(y.reshape(B_, L, S5_WIDTH)).astype(h.dtype)
    g = z * jax.nn.sigmoid(z @ w_glu + b_glu)
    return g @ w_out


def _rotary(t, pos):
    half = t.shape[-1] // 2
    inv_freq = jnp.power(RET_THETA, -jnp.arange(half, dtype=jnp.float32) / half)
    ang = pos[:, None] * inv_freq[None, :]
    cos = jnp.cos(ang)[None, :, None, :]
    sin = jnp.sin(ang)[None, :, None, :]
    t1, t2 = t[..., :half], t[..., half:]
    return jnp.concatenate([t1 * cos - t2 * sin, t1 * sin + t2 * cos], axis=-1)


def _retention_mixer(h, w_in, gn_g, w_out):
    B_, L, _ = h.shape
    qd = RET_HEADS * RET_DK
    vd = RET_HEADS * RET_DV
    q, k, v, gate = jnp.split(h @ w_in, [qd, 2 * qd, 2 * qd + vd], axis=-1)
    pos = jnp.arange(L, dtype=jnp.float32)
    q = _rotary(q.reshape(B_, L, RET_HEADS, RET_DK).astype(jnp.float32), pos)
    k = _rotary(k.reshape(B_, L, RET_HEADS, RET_DK).astype(jnp.float32), pos) * (RET_DK ** -0.5)
    v = v.reshape(B_, L, RET_HEADS, RET_DV).astype(jnp.float32)
    pad = (-L) % RET_CHUNK
    n_chunks = (L + pad) // RET_CHUNK

    def to_chunks(t):
        t = jnp.pad(t, ((0, 0), (pad, 0), (0, 0), (0, 0)))
        return t.reshape(B_, n_chunks, RET_CHUNK, RET_HEADS, t.shape[-1]).transpose(1, 0, 2, 3, 4)

    log_gamma = jnp.log1p(-jnp.power(2.0, -5.0 - jnp.arange(RET_HEADS, dtype=jnp.float32)))
    idx = jnp.arange(RET_CHUNK, dtype=jnp.float32)
    rel = idx[:, None] - idx[None, :]
    decay_mask = jnp.where(rel[None] >= 0,
                           jnp.exp(log_gamma[:, None, None] * jnp.maximum(rel, 0.0)[None]), 0.0)
    q_decay = jnp.exp(log_gamma[None, :] * (idx[:, None] + 1.0))
    k_decay = jnp.exp(log_gamma[None, :] * (RET_CHUNK - 1.0 - idx[:, None]))
    chunk_decay = jnp.exp(log_gamma * RET_CHUNK)

    def step(state, chunk):
        qc, kc, vc = chunk
        scores = jnp.einsum('bnhd,bmhd->bhnm', qc, kc) * decay_mask[None]
        intra = jnp.einsum('bhnm,bmhe->bnhe', scores, vc)
        inter = jnp.einsum('bnhd,bhde->bnhe', qc, state) * q_decay[None, :, :, None]
        state = (state * chunk_decay[None, :, None, None]
                 + jnp.einsum('bmhd,bmhe->bhde', kc * k_decay[None, :, :, None], vc))
        return state, intra + inter

    state0 = jnp.zeros((B_, RET_HEADS, RET_DK, RET_DV), jnp.float32)
    _, o = lax.scan(step, state0, (to_chunks(q), to_chunks(k), to_chunks(v)))
    o = o.transpose(1, 0, 2, 3, 4).reshape(B_, n_chunks * RET_CHUNK, RET_HEADS, RET_DV)[:, pad:]
    mu = jnp.mean(o, axis=-1, keepdims=True)
    var = jnp.mean(jnp.square(o - mu), axis=-1, keepdims=True)
    o = ((o - mu) * lax.rsqrt(var + LN_EPS)).reshape(B_, L, RET_HEADS * RET_DV) * gn_g.astype(jnp.float32)
    out = jax.nn.silu(gate.astype(jnp.float32)) * o
    return out.astype(h.dtype) @ w_out


def _diff_attn_mixer(h, w_in, lq1, lk1, lq2, lk2, subln_g, w_out, lambda_init):
    B_, L, _ = h.shape
    qk = DIFF_HEADS * 2 * DIFF_DQK
    q, k, v = jnp.split(h @ w_in, [qk, 2 * qk], axis=-1)
    q = q.reshape(B_, L, DIFF_HEADS, 2, DIFF_DQK).astype(jnp.float32) * (DIFF_DQK ** -0.5)
    k = k.reshape(B_, L, DIFF_HEADS, 2, DIFF_DQK).astype(jnp.float32)
    v = v.reshape(B_, L, DIFF_HEADS, DIFF_DV).astype(jnp.float32)
    lam = (jnp.exp(jnp.sum(lq1.astype(jnp.float32) * lk1.astype(jnp.float32)))
           - jnp.exp(jnp.sum(lq2.astype(jnp.float32) * lk2.astype(jnp.float32))) + lambda_init)
    pad = (-L) % Q_BLOCK
    Lp = L + pad
    n_blocks = Lp // Q_BLOCK
    qp = jnp.pad(q, ((0, 0), (pad, 0), (0, 0), (0, 0), (0, 0)))
    kp = jnp.pad(k, ((0, 0), (pad, 0), (0, 0), (0, 0), (0, 0)))
    vp = jnp.pad(v, ((0, 0), (pad, 0), (0, 0), (0, 0)))
    kpos = jnp.arange(Lp)

    def attend_block(b):
        start = b * Q_BLOCK
        qb = lax.dynamic_slice_in_dim(qp, start, Q_BLOCK, axis=1)
        s = jnp.einsum('bqhpd,bkhpd->bhpqk', qb, kp)
        qpos = start + jnp.arange(Q_BLOCK)
        allowed = (kpos[None, :] <= qpos[:, None]) & (kpos[None, :] >= pad)
        p = jax.nn.softmax(jnp.where(allowed, s, MASK_VALUE), axis=-1)
        attn = p[:, :, 0] - lam * p[:, :, 1]
        return jnp.einsum('bhqk,bkhe->bqhe', attn, vp)

    o = lax.map(attend_block, jnp.arange(n_blocks))
    o = jnp.moveaxis(o, 0, 1).reshape(B_, Lp, DIFF_HEADS, DIFF_DV)[:, pad:]
    o = o * lax.rsqrt(jnp.mean(jnp.square(o), axis=-1, keepdims=True) + LN_EPS)
    o = o * subln_g.astype(jnp.float32) * (1.0 - lambda_init)
    return o.reshape(B_, L, DIFF_HEADS * DIFF_DV).astype(h.dtype) @ w_out


def _moe_ffn(h, w_router, b_router, w_up, b_up, w_down, b_down):
    B_, L, Dm = h.shape
    T = B_ * L
    A = T * TOP_K
    xt = h.reshape(T, Dm)
    logits = (xt @ w_router + b_router).astype(jnp.float32)
    top_vals, top_idx = lax.top_k(logits, TOP_K)
    gates = jax.nn.softmax(top_vals, axis=-1)
    e_flat = top_idx.reshape(A).astype(jnp.int32)
    g_flat = gates.reshape(A)
    tok_flat = jnp.arange(A, dtype=jnp.int32) // TOP_K
    order = jnp.argsort(e_flat).astype(jnp.int32)
    e_sorted = e_flat[order]
    counts = jnp.bincount(e_flat, length=N_EXPERTS).astype(jnp.int32)
    padded = ((counts + MOE_BLOCK - 1) // MOE_BLOCK) * MOE_BLOCK
    pad_end = jnp.cumsum(padded)
    pad_start = pad_end - padded
    start = jnp.cumsum(counts) - counts
    rank = jnp.arange(A, dtype=jnp.int32) - start[e_sorted]
    dest = pad_start[e_sorted] + rank
    n_blocks = -(-A // MOE_BLOCK) + N_EXPERTS
    n_slots = n_blocks * MOE_BLOCK
    slot_assign = jnp.full((n_slots,), A, jnp.int32).at[dest].set(order)
    slot_tok = jnp.concatenate([tok_flat, jnp.array([T], jnp.int32)])[slot_assign]
    slot_gate = jnp.concatenate([g_flat, jnp.zeros((1,), jnp.float32)])[slot_assign]
    x_pad = jnp.concatenate([xt, jnp.zeros((1, Dm), xt.dtype)], axis=0)
    x_slots = x_pad[slot_tok].reshape(n_blocks, MOE_BLOCK, Dm)
    block_e = jnp.minimum(jnp.searchsorted(pad_end, jnp.arange(n_blocks, dtype=jnp.int32) * MOE_BLOCK,
                                           side='right'), N_EXPERTS - 1)

    def expert_block(args):
        xb, e = args
        hb = (xb @ w_up[e] + b_up[e]).astype(jnp.float32)
        x_glu = jnp.minimum(hb[:, :D_EXPERT], SWIGLU_LIMIT)
        x_lin = jnp.clip(hb[:, D_EXPERT:], -SWIGLU_LIMIT, SWIGLU_LIMIT)
        act = x_glu * jax.nn.sigmoid(SWIGLU_ALPHA * x_glu) * (x_lin + 1.0)
        return act.astype(xb.dtype) @ w_down[e] + b_down[e]

    y_slots = lax.map(expert_block, (x_slots, block_e)).reshape(n_slots, Dm)
    y = jnp.zeros((T + 1, Dm), jnp.float32).at[slot_tok].add(slot_gate[:, None] * y_slots.astype(jnp.float32))
    return y[:T].reshape(B_, L, Dm).astype(h.dtype)


def setup_inputs(seed: int = 0) -> dict:
    key = jax.random.key(seed)
    ks = iter(jax.random.split(key, 48))

    def nrm(shape, scale):
        return scale * jax.random.normal(next(ks), shape, jnp.float32)

    nA, nB, nC = N_S5_LAYERS, N_RET_LAYERS, N_DIFF_LAYERS
    G, N, C, E = S5_GROUPS, S5_STATE, S5_GROUP_CH, S5_WIDTH
    beta = DEEPNORM_BETA
    ret_in = 2 * RET_HEADS * RET_DK + 2 * RET_HEADS * RET_DV
    diff_in = 2 * DIFF_HEADS * 2 * DIFF_DQK + DIFF_HEADS * DIFF_DV
    return {
        'x': nrm((BATCH, SEQ, D_MODEL), 1.0),
        'meta_tokens': nrm((N_META, D_MODEL), 1.0),
        's5_w_in': nrm((nA, D_MODEL, E), D_MODEL ** -0.5),
        's5_lambda_re': -0.5 + nrm((nA, G, N), 0.01),
        's5_lambda_im': jnp.broadcast_to(math.pi * jnp.arange(N, dtype=jnp.float32), (nA, G, N)),
        's5_log_step': jax.random.uniform(next(ks), (nA, G), jnp.float32,
                                          math.log(S5_MIN_STEP), math.log(S5_MAX_STEP)),
        's5_b_re': nrm((nA, G, N, C), (2 * C) ** -0.5),
        's5_b_im': nrm((nA, G, N, C), (2 * C) ** -0.5),
        's5_c_re': nrm((nA, G, C, N), (2 * N) ** -0.5),
        's5_c_im': nrm((nA, G, C, N), (2 * N) ** -0.5),
        's5_d': nrm((nA, E), 1.0),
        's5_w_glu': nrm((nA, E, E), E ** -0.5),
        's5_b_glu': nrm((nA, E), 0.01),
        's5_w_out': nrm((nA, E, D_MODEL), beta * E ** -0.5),
        'ret_w_in': nrm((nB, D_MODEL, ret_in), D_MODEL ** -0.5),
        'ret_gn_g': 1.0 + nrm((nB, RET_HEADS * RET_DV), 0.01),
        'ret_w_out': nrm((nB, RET_HEADS * RET_DV, D_MODEL), beta * (RET_HEADS * RET_DV) ** -0.5),
        'diff_w_in': nrm((nC, D_MODEL, diff_in), D_MODEL ** -0.5),
        'diff_lambda_q1': nrm((nC, DIFF_DQK), 0.1),
        'diff_lambda_k1': nrm((nC, DIFF_DQK), 0.1),
        'diff_lambda_q2': nrm((nC, DIFF_DQK), 0.1),
        'diff_lambda_k2': nrm((nC, DIFF_DQK), 0.1),
        'diff_subln_g': 1.0 + nrm((nC, DIFF_DV), 0.01),
        'diff_w_out': nrm((nC, DIFF_HEADS * DIFF_DV, D_MODEL), beta * (DIFF_HEADS * DIFF_DV) ** -0.5),
        'ln_mix_g': 1.0 + nrm((DEPTH, D_MODEL), 0.01),
        'ln_mix_b': nrm((DEPTH, D_MODEL), 0.01),
        'moe_w_router': nrm((DEPTH, D_MODEL, N_EXPERTS), D_MODEL ** -0.5),
        'moe_b_router': nrm((DEPTH, N_EXPERTS), 0.01),
        'moe_w_up': nrm((DEPTH, N_EXPERTS, D_MODEL, 2 * D_EXPERT), D_MODEL ** -0.5),
        'moe_b_up': nrm((DEPTH, N_EXPERTS, 2 * D_EXPERT), 0.01),
        'moe_w_down': nrm((DEPTH, N_EXPERTS, D_EXPERT, D_MODEL), beta * D_EXPERT ** -0.5),
        'moe_b_down': nrm((DEPTH, N_EXPERTS, D_MODEL), 0.01),
        'ln_ffn_g': 1.0 + nrm((DEPTH, D_MODEL), 0.01),
        'ln_ffn_b': nrm((DEPTH, D_MODEL), 0.01),
    }


def reference(x, meta_tokens, s5_w_in, s5_lambda_re, s5_lambda_im, s5_log_step, s5_b_re, s5_b_im,
              s5_c_re, s5_c_im, s5_d, s5_w_glu, s5_b_glu, s5_w_out, ret_w_in, ret_gn_g, ret_w_out,
              diff_w_in, diff_lambda_q1, diff_lambda_k1, diff_lambda_q2, diff_lambda_k2, diff_subln_g,
              diff_w_out, ln_mix_g, ln_mix_b, moe_w_router, moe_b_router, moe_w_up, moe_b_up,
              moe_w_down, moe_b_down, ln_ffn_g, ln_ffn_b):
    B_ = x.shape[0]
    meta = jnp.broadcast_to(meta_tokens[None].astype(x.dtype), (B_, N_META, D_MODEL))
    h = jnp.concatenate([meta, x], axis=1)
    for i in range(DEPTH):
        kind = i % N_MIXERS
        j = i // N_MIXERS
        if kind == 0:
            mix = _s5_mixer(h, s5_w_in[j], s5_lambda_re[j], s5_lambda_im[j], s5_log_step[j],
                            s5_b_re[j], s5_b_im[j], s5_c_re[j], s5_c_im[j], s5_d[j],
                            s5_w_glu[j], s5_b_glu[j], s5_w_out[j])
        elif kind == 1:
            mix = _retention_mixer(h, ret_w_in[j], ret_gn_g[j], ret_w_out[j])
        else:
            lambda_init = 0.8 - 0.6 * math.exp(-0.3 * i)
            mix = _diff_attn_mixer(h, diff_w_in[j], diff_lambda_q1[j], diff_lambda_k1[j],
                                   diff_lambda_q2[j], diff_lambda_k2[j], diff_subln_g[j],
                                   diff_w_out[j], lambda_init)
        h = _layer_norm(DEEPNORM_ALPHA * h + mix, ln_mix_g[i], ln_mix_b[i])
        ffn = _moe_ffn(h, moe_w_router[i], moe_b_router[i], moe_w_up[i], moe_b_up[i],
                       moe_w_down[i], moe_b_down[i])
        h = _layer_norm(DEEPNORM_ALPHA * h + ffn, ln_ffn_g[i], ln_ffn_b[i])
    return h[:, N_META:, :]
```

```python
import functools
import math

import jax
import jax.numpy as jnp
from jax import lax
from jax.experimental import pallas as pl
from jax.experimental.pallas import tpu as pltpu

F32 = jnp.float32
BF16 = jnp.bfloat16

D_MODEL = 1024
DEPTH = 4
N_META = 16
N_MIXERS = 3
S5_GROUP_CH = 16
S5_GROUPS = D_MODEL // S5_GROUP_CH
S5_STATE = 64
S5_CHUNK = 16
S5_OCT = 8
N_OCT = S5_GROUPS // S5_OCT
RET_HEADS = 4
RET_DK = D_MODEL // RET_HEADS
RET_DV = 2 * RET_DK
RET_THETA = 10000.0
DIFF_HEADS = D_MODEL // 128
DIFF_DQK = 64
DIFF_DV = 128
N_EXPERTS = 32
TOP_K = 4
D_EXPERT = D_MODEL
SWIGLU_LIMIT = 7.0
SWIGLU_ALPHA = 1.702
LN_EPS = 1e-5
MASK_VALUE = -1e30
DEEPNORM_ALPHA = (2 * DEPTH) ** 0.25
SEQ_ALIGN = 128
LANES = 128
VMEM_LIMIT = 56 * 1024 * 1024


def _cparams(*sem):
    return pltpu.CompilerParams(dimension_semantics=sem, vmem_limit_bytes=VMEM_LIMIT)


def _tile(n, target, mult=8):
    best = None
    for t in range(mult, min(n, target) + 1, mult):
        if n % t == 0:
            best = t
    assert best is not None, (n, target, mult)
    return best


def _layer_norm_rows(v, g, b):
    mu = jnp.mean(v, axis=-1, keepdims=True)
    c = v - mu
    var = jnp.mean(c * c, axis=-1, keepdims=True)
    return c * lax.rsqrt(var + LN_EPS) * g + b


def _mm_kernel(x_ref, w_ref, o_ref):
    o_ref[...] = jnp.dot(x_ref[...].astype(BF16), w_ref[...],
                         preferred_element_type=F32).astype(o_ref.dtype)


def _matmul(x, w, out_dtype, tm, tn):
    m, k = x.shape
    n = w.shape[1]
    return pl.pallas_call(
        _mm_kernel,
        grid=(n // tn, m // tm),
        in_specs=[pl.BlockSpec((tm, k), lambda j, i: (i, 0)),
                  pl.BlockSpec((k, tn), lambda j, i: (0, j))],
        out_specs=pl.BlockSpec((tm, tn), lambda j, i: (i, j)),
        out_shape=jax.ShapeDtypeStruct((m, n), out_dtype),
        compiler_params=_cparams("parallel", "parallel"),
        name="matmul",
    )(x, w)


def _mm_res_ln_kernel(x_ref, w_ref, res_ref, g_ref, b_ref, o_ref):
    y = jnp.dot(x_ref[...].astype(BF16), w_ref[...], preferred_element_type=F32)
    o_ref[...] = _layer_norm_rows(DEEPNORM_ALPHA * res_ref[...] + y, g_ref[...], b_ref[...])


def _matmul_res_ln(x, w, res, g, b, tm):
    m, k = x.shape
    n = w.shape[1]
    row = lambda i: (i, 0)
    fix = lambda i: (0, 0)
    return pl.pallas_call(
        _mm_res_ln_kernel,
        grid=(m // tm,),
        in_specs=[pl.BlockSpec((tm, k), row), pl.BlockSpec((k, n), fix), pl.BlockSpec((tm, n), row),
                  pl.BlockSpec((1, n), fix), pl.BlockSpec((1, n), fix)],
        out_specs=pl.BlockSpec((tm, n), row),
        out_shape=jax.ShapeDtypeStruct((m, n), F32),
        compiler_params=_cparams("parallel"),
        name="matmul_res_ln",
    )(x, w, res, g.reshape(1, n), b.reshape(1, n))


def _res_ln_kernel(res_ref, y_ref, g_ref, b_ref, o_ref):
    o_ref[...] = _layer_norm_rows(DEEPNORM_ALPHA * res_ref[...] + y_ref[...], g_ref[...], b_ref[...])


def _res_ln(res, y, g, b, tm):
    m, n = res.shape
    row = lambda i: (i, 0)
    fix = lambda i: (0, 0)
    return pl.pallas_call(
        _res_ln_kernel,
        grid=(m // tm,),
        in_specs=[pl.BlockSpec((tm, n), row), pl.BlockSpec((tm, n), row),
                  pl.BlockSpec((1, n), fix), pl.BlockSpec((1, n), fix)],
        out_specs=pl.BlockSpec((tm, n), row),
        out_shape=jax.ShapeDtypeStruct((m, n), F32),
        compiler_params=_cparams("parallel"),
        name="res_ln",
    )(res, y, g.reshape(1, n), b.reshape(1, n))


def _s5_tables(lam_re, lam_im, log_step, b_re, b_im, c_re, c_im, d_skip):
    hp = lax.Precision.HIGHEST
    g_, n_, c_, q_, o_ = S5_GROUPS, S5_STATE, S5_GROUP_CH, S5_CHUNK, S5_OCT
    dt = jnp.exp(log_step.astype(F32))[:, None]
    lr = jnp.minimum(lam_re.astype(F32), -1e-4)
    li = lam_im.astype(F32)
    mag = jnp.exp(lr * dt)
    ab_re = mag * jnp.cos(li * dt)
    ab_im = mag * jnp.sin(li * dt)
    den = lr * lr + li * li
    f_re = ((ab_re - 1.0) * lr + ab_im * li) / den
    f_im = (ab_im * lr - (ab_re - 1.0) * li) / den
    br, bi = b_re.astype(F32), b_im.astype(F32)
    bbar_re = f_re[..., None] * br - f_im[..., None] * bi
    bbar_im = f_re[..., None] * bi + f_im[..., None] * br
    j = jnp.arange(q_ + 1, dtype=F32)[:, None, None]
    pmag = jnp.exp(j * (lr * dt)[None])
    pw_re = pmag * jnp.cos(j * (li * dt)[None])
    pw_im = pmag * jnp.sin(j * (li * dt)[None])
    cr, ci = c_re.astype(F32), c_im.astype(F32)
    w_re = pw_re[..., None] * bbar_re[None] - pw_im[..., None] * bbar_im[None]
    w_im = pw_re[..., None] * bbar_im[None] + pw_im[..., None] * bbar_re[None]
    kj = (jnp.einsum('gon,jgni->jgoi', cr, w_re[:q_], precision=hp)
          - jnp.einsum('gon,jgni->jgoi', ci, w_im[:q_], precision=hp))
    kj = kj.at[0].add(d_skip.astype(F32).reshape(g_, c_)[:, :, None] * jnp.eye(c_, dtype=F32)[None])
    eye = jnp.eye(o_, dtype=F32)
    s_idx = jnp.arange(q_)[:, None]
    t_idx = jnp.arange(q_)[None, :]
    lag = t_idx - s_idx
    kts = jnp.where((lag >= 0)[:, :, None, None, None], kj[jnp.clip(lag, 0, q_ - 1)], 0.0)
    kts = kts.reshape(q_, q_, N_OCT, o_, c_, c_)
    m_tab = jnp.einsum('stogpi,gh->osgithp', kts, eye).reshape(N_OCT, q_, o_ * c_, q_ * o_ * c_)
    pre = w_re[:q_][::-1].reshape(q_, N_OCT, o_, n_, c_)
    pim = w_im[:q_][::-1].reshape(q_, N_OCT, o_, n_, c_)
    p_tab = jnp.stack([jnp.einsum('sogni,gh->osgihn', pre, eye),
                       jnp.einsum('sogni,gh->osgihn', pim, eye)], axis=4)
    p_tab = p_tab.reshape(N_OCT, q_, o_ * c_, 2 * o_ * n_)
    ar = pw_re[1:].reshape(q_, N_OCT, o_, n_)
    ai = pw_im[1:].reshape(q_, N_OCT, o_, n_)
    cr_o = cr.reshape(N_OCT, o_, c_, n_)
    ci_o = ci.reshape(N_OCT, o_, c_, n_)
    r_re = cr_o[None] * ar[:, :, :, None, :] - ci_o[None] * ai[:, :, :, None, :]
    r_im = -(cr_o[None] * ai[:, :, :, None, :] + ci_o[None] * ar[:, :, :, None, :])
    r_tab = jnp.stack([jnp.einsum('togpn,gh->ognthp', r_re, eye),
                       jnp.einsum('togpn,gh->ognthp', r_im, eye)], axis=1)
    r_tab = r_tab.reshape(N_OCT, 2 * o_ * n_, q_ * o_ * c_)
    a16_re = pw_re[q_].reshape(N_OCT, o_ * n_)
    a16_im = pw_im[q_].reshape(N_OCT, o_ * n_)
    return m_tab.astype(BF16), p_tab.astype(BF16), r_tab.astype(BF16), a16_re, a16_im


def _s5_chunk_rows(u_ref, s, tr, valid):
    us = u_ref[pl.ds(s, tr, stride=S5_CHUNK), :]
    return jnp.where(valid, us, 0.0).astype(BF16)


def _s5_valid(tr, chunks_per_batch, pad_chunks):
    chunk = pl.program_id(1) * tr + lax.broadcasted_iota(jnp.int32, (tr, 1), 0)
    return (chunk % chunks_per_batch) >= pad_chunks


def _s5_state_kernel(u_ref, p_ref, s_ref, *, tr, chunks_per_batch, pad_chunks):
    valid = _s5_valid(tr, chunks_per_batch, pad_chunks)
    acc = jnp.zeros(s_ref.shape, F32)
    for s in range(S5_CHUNK):
        acc += jnp.dot(_s5_chunk_rows(u_ref, s, tr, valid), p_ref[0, s], preferred_element_type=F32)
    s_ref[...] = acc


def _s5_scan_kernel(s_ref, ar_ref, ai_ref, x_ref, st_ref, *, tc):
    half = st_ref.shape[1] // 2

    @pl.when(pl.program_id(1) == 0)
    def _():
        st_ref[...] = jnp.zeros(st_ref.shape, F32)

    ar = ar_ref[...]
    ai = ai_ref[...]

    def body(c, carry):
        xr, xi = carry
        x_ref[c, :, :half] = xr
        x_ref[c, :, half:] = xi
        s = s_ref[c]
        return ar * xr - ai * xi + s[:, :half], ar * xi + ai * xr + s[:, half:]

    xr, xi = lax.fori_loop(0, tc, body, (st_ref[:, :half], st_ref[:, half:]))
    st_ref[:, :half] = xr
    st_ref[:, half:] = xi


def _gelu_tanh(y):
    return 0.5 * y * (1.0 + jnp.tanh(math.sqrt(2.0 / math.pi) * (y + 0.044715 * (y * y * y))))


def _s5_out_kernel(u_ref, xp_ref, m_ref, r_ref, z_ref, acc_ref, *, tr, chunks_per_batch, pad_chunks):
    valid = _s5_valid(tr, chunks_per_batch, pad_chunks)
    acc_ref[...] = jnp.dot(xp_ref[...].astype(BF16), r_ref[0], preferred_element_type=F32)
    for s in range(S5_CHUNK):
        acc_ref[:, s * LANES:] += jnp.dot(_s5_chunk_rows(u_ref, s, tr, valid), m_ref[0, s, :, s * LANES:],
                                          preferred_element_type=F32)
    for t in range(S5_CHUNK):
        z_ref[pl.ds(t, tr, stride=S5_CHUNK), :] = _gelu_tanh(acc_ref[:, t * LANES:(t + 1) * LANES])


def _s5_tail_kernel(z_ref, wg_ref, bg_ref, wo_ref, res_ref, g_ref, b_ref, o_ref):
    z = z_ref[...]
    t = jnp.dot(z.astype(BF16), wg_ref[...], preferred_element_type=F32) + bg_ref[...]
    glu = z * jax.nn.sigmoid(t)
    y = jnp.dot(glu.astype(BF16), wo_ref[...], preferred_element_type=F32)
    o_ref[...] = _layer_norm_rows(DEEPNORM_ALPHA * res_ref[...] + y, g_ref[...], b_ref[...])


def _s5_mixer(h, lp, pad, w_in, lam_re, lam_im, log_step, b_re, b_im, c_re, c_im, d_skip,
              w_glu, b_glu, w_out, ln_g, ln_b, tm):
    tp, d = h.shape
    n_batch = tp // lp
    m_tab, p_tab, r_tab, a16_re, a16_im = _s5_tables(lam_re, lam_im, log_step, b_re, b_im,
                                                     c_re, c_im, d_skip)
    u = _matmul(h, w_in.astype(BF16), F32, tm, d)
    n_chunks = tp // S5_CHUNK
    chunks_per_batch = lp // S5_CHUNK
    pad_chunks = pad // S5_CHUNK
    tr = _tile(n_chunks, 344)
    st_w = 2 * S5_OCT * S5_STATE
    kw = dict(tr=tr, chunks_per_batch=chunks_per_batch, pad_chunks=pad_chunks)
    s_all = pl.pallas_call(
        functools.partial(_s5_state_kernel, **kw),
        grid=(N_OCT, n_chunks // tr),
        in_specs=[pl.BlockSpec((tr * S5_CHUNK, LANES), lambda o, i: (i, o)),
                  pl.BlockSpec((1, S5_CHUNK, LANES, st_w), lambda o, i: (o, 0, 0, 0))],
        out_specs=pl.BlockSpec((tr, st_w), lambda o, i: (i, o)),
        out_shape=jax.ShapeDtypeStruct((n_chunks, N_OCT * st_w), F32),
        compiler_params=_cparams("parallel", "parallel"),
        name="s5_state",
    )(u, p_tab)
    tc = _tile(chunks_per_batch, 129, 1)
    nt = chunks_per_batch // tc
    x_prev = pl.pallas_call(
        functools.partial(_s5_scan_kernel, tc=tc),
        grid=(n_batch, nt),
        in_specs=[pl.BlockSpec((tc, N_OCT, st_w), lambda b, i: (b * nt + i, 0, 0)),
                  pl.BlockSpec((N_OCT, st_w // 2), lambda b, i: (0, 0)),
                  pl.BlockSpec((N_OCT, st_w // 2), lambda b, i: (0, 0))],
        out_specs=pl.BlockSpec((tc, N_OCT, st_w), lambda b, i: (b * nt + i, 0, 0)),
        out_shape=jax.ShapeDtypeStruct((n_chunks, N_OCT, st_w), F32),
        scratch_shapes=[pltpu.VMEM((N_OCT, st_w), F32)],
        compiler_params=_cparams("arbitrary", "arbitrary"),
        name="s5_scan",
    )(s_all.reshape(n_chunks, N_OCT, st_w), a16_re, a16_im)
    z = pl.pallas_call(
        functools.partial(_s5_out_kernel, **kw),
        grid=(N_OCT, n_chunks // tr),
        in_specs=[pl.BlockSpec((tr * S5_CHUNK, LANES), lambda o, i: (i, o)),
                  pl.BlockSpec((tr, st_w), lambda o, i: (i, o)),
                  pl.BlockSpec((1, S5_CHUNK, LANES, S5_CHUNK * LANES), lambda o, i: (o, 0, 0, 0)),
                  pl.BlockSpec((1, st_w, S5_CHUNK * LANES), lambda o, i: (o, 0, 0))],
        out_specs=pl.BlockSpec((tr * S5_CHUNK, LANES), lambda o, i: (i, o)),
        out_shape=jax.ShapeDtypeStruct((tp, d), F32),
        scratch_shapes=[pltpu.VMEM((tr, S5_CHUNK * LANES), F32)],
        compiler_params=_cparams("parallel", "parallel"),
        name="s5_out",
    )(u, x_prev.reshape(n_chunks, N_OCT * st_w), m_tab, r_tab)
    row = lambda i: (i, 0)
    fix = lambda i: (0, 0)
    return pl.pallas_call(
        _s5_tail_kernel,
        grid=(tp // tm,),
        in_specs=[pl.BlockSpec((tm, d), row), pl.BlockSpec((d, d), fix), pl.BlockSpec((1, d), fix),
                  pl.BlockSpec((d, d), fix), pl.BlockSpec((tm, d), row),
                  pl.BlockSpec((1, d), fix), pl.BlockSpec((1, d), fix)],
        out_specs=pl.BlockSpec((tm, d), row),
        out_shape=jax.ShapeDtypeStruct((tp, d), F32),
        compiler_params=_cparams("parallel"),
        name="s5_tail",
    )(z, w_glu.astype(BF16), b_glu.reshape(1, d).astype(F32), w_out.astype(BF16), h,
      ln_g.reshape(1, d), ln_b.reshape(1, d))


def _ret_kernel(q_ref, k_ref, v_ref, gate_ref, cos_ref, sin_ref, dm_ref, qd_ref, kd_ref, cd_ref, g_ref,
                o_ref, st_ref, *, chunk, pad):
    c = pl.program_id(2)

    @pl.when(c == 0)
    def _():
        st_ref[...] = jnp.zeros(st_ref.shape, F32)

    cos = cos_ref[...]
    sin = sin_ref[...]
    half = RET_DK // 2

    def rot(t):
        t1 = t[:, :half]
        t2 = t[:, half:]
        return jnp.concatenate([t1 * cos - t2 * sin, t1 * sin + t2 * cos], axis=-1)

    q = rot(q_ref[...])
    k = rot(k_ref[...]) * (RET_DK ** -0.5)
    pos = c * chunk + lax.broadcasted_iota(jnp.int32, (chunk, 1), 0)
    k = jnp.where(pos >= pad, k, 0.0)
    qb = q.astype(BF16)
    vb = v_ref[...].astype(BF16)
    scores = lax.dot_general(qb, k.astype(BF16), (((1,), (1,)), ((), ())),
                             preferred_element_type=F32) * dm_ref[0]
    intra = jnp.dot(scores.astype(BF16), vb, preferred_element_type=F32)
    state = st_ref[...]
    inter = jnp.dot(qb, state.astype(BF16), preferred_element_type=F32) * qd_ref[0]
    st_ref[...] = state * cd_ref[0] + lax.dot_general((k * kd_ref[0]).astype(BF16), vb,
                                                      (((0,), (0,)), ((), ())),
                                                      preferred_element_type=F32)
    o = intra + inter
    mu = jnp.mean(o, axis=-1, keepdims=True)
    oc = o - mu
    var = jnp.mean(oc * oc, axis=-1, keepdims=True)
    o = oc * lax.rsqrt(var + LN_EPS) * g_ref[...]
    gate = gate_ref[...]
    o_ref[...] = (gate * jax.nn.sigmoid(gate) * o).astype(o_ref.dtype)


def _retention_mixer(h, lp, pad, w_in, gn_g, w_out, ln_g, ln_b, tm):
    tp, d = h.shape
    n_batch = tp // lp
    chunk = SEQ_ALIGN
    ncb = lp // chunk
    qd = RET_HEADS * RET_DK
    vd = RET_HEADS * RET_DV
    proj = _matmul(h, w_in.astype(BF16), F32, tm, _tile(w_in.shape[1], 1536, LANES))
    half = RET_DK // 2
    pos = jnp.arange(lp, dtype=F32) - pad
    inv_freq = jnp.power(RET_THETA, -jnp.arange(half, dtype=F32) / half)
    ang = pos[:, None] * inv_freq[None, :]
    cos, sin = jnp.cos(ang), jnp.sin(ang)
    log_gamma = jnp.log1p(-jnp.power(2.0, -5.0 - jnp.arange(RET_HEADS, dtype=F32)))
    idx = jnp.arange(chunk, dtype=F32)
    rel = idx[:, None] - idx[None, :]
    dmask = jnp.where(rel[None] >= 0, jnp.exp(log_gamma[:, None, None] * jnp.maximum(rel, 0.0)[None]), 0.0)
    q_decay = jnp.exp(log_gamma[:, None] * (idx[None, :] + 1.0))[:, :, None]
    k_decay = jnp.exp(log_gamma[:, None] * (chunk - 1.0 - idx[None, :]))[:, :, None]
    chunk_decay = jnp.exp(log_gamma * chunk).reshape(RET_HEADS, 1, 1)
    kb = qd // RET_DK
    vb = 2 * qd // RET_DV
    gb = (2 * qd + vd) // RET_DV
    o = pl.pallas_call(
        functools.partial(_ret_kernel, chunk=chunk, pad=pad),
        grid=(n_batch, RET_HEADS, ncb),
        in_specs=[pl.BlockSpec((chunk, RET_DK), lambda b, hh, c: (b * ncb + c, hh)),
                  pl.BlockSpec((chunk, RET_DK), lambda b, hh, c: (b * ncb + c, kb + hh)),
                  pl.BlockSpec((chunk, RET_DV), lambda b, hh, c: (b * ncb + c, vb + hh)),
                  pl.BlockSpec((chunk, RET_DV), lambda b, hh, c: (b * ncb + c, gb + hh)),
                  pl.BlockSpec((chunk, half), lambda b, hh, c: (c, 0)),
                  pl.BlockSpec((chunk, half), lambda b, hh, c: (c, 0)),
                  pl.BlockSpec((1, chunk, chunk), lambda b, hh, c: (hh, 0, 0)),
                  pl.BlockSpec((1, chunk, 1), lambda b, hh, c: (hh, 0, 0)),
                  pl.BlockSpec((1, chunk, 1), lambda b, hh, c: (hh, 0, 0)),
                  pl.BlockSpec((1, 1, 1), lambda b, hh, c: (hh, 0, 0)),
                  pl.BlockSpec((1, RET_DV), lambda b, hh, c: (0, hh))],
        out_specs=pl.BlockSpec((chunk, RET_DV), lambda b, hh, c: (b * ncb + c, hh)),
        out_shape=jax.ShapeDtypeStruct((tp, vd), BF16),
        scratch_shapes=[pltpu.VMEM((RET_DK, RET_DV), F32)],
        compiler_params=_cparams("parallel", "parallel", "arbitrary"),
        name="retention",
    )(proj, proj, proj, proj, cos, sin, dmask, q_decay, k_decay, chunk_decay,
      gn_g.reshape(1, vd).astype(F32))
    return _matmul_res_ln(o, w_out.astype(BF16), h, ln_g, ln_b, tm)


def _diff_kernel(q_ref, k_ref, v_ref, lam_ref, g_ref, o_ref, m_ref, l_ref, acc_ref,
                 *, tq, pad, lambda_init):
    i = pl.program_id(2)
    q = q_ref[...]
    lane = lax.broadcasted_iota(jnp.int32, q.shape, 1)
    zero = jnp.zeros_like(q)
    q_parts = (jnp.where(lane < DIFF_DQK, q, zero), jnp.where(lane >= DIFF_DQK, q, zero))
    m_ref[...] = jnp.full(m_ref.shape, MASK_VALUE, F32)
    l_ref[...] = jnp.zeros(l_ref.shape, F32)
    acc_ref[...] = jnp.zeros(acc_ref.shape, F32)

    def step(j, masked):
        start = j * tq if isinstance(j, int) else pl.multiple_of(j * tq, tq)
        k = k_ref[pl.ds(start, tq), :]
        v = v_ref[pl.ds(start, tq), :]
        if masked:
            qpos = i * tq + lax.broadcasted_iota(jnp.int32, (tq, tq), 0)
            kpos = j * tq + lax.broadcasted_iota(jnp.int32, (tq, tq), 1)
            allowed = (kpos <= qpos) & (kpos >= pad)
        for p in range(2):
            s = lax.dot_general(q_parts[p], k, (((1,), (1,)), ((), ())), preferred_element_type=F32)
            if masked:
                s = jnp.where(allowed, s, MASK_VALUE)
            m_prev = m_ref[p]
            m_new = jnp.maximum(m_prev, jnp.max(s, axis=-1, keepdims=True))
            alpha = jnp.exp(m_prev - m_new)
            pe = jnp.exp(s - m_new)
            l_ref[p] = alpha * l_ref[p] + jnp.sum(pe, axis=-1, keepdims=True)
            acc_ref[p] = alpha * acc_ref[p] + jnp.dot(pe.astype(BF16), v, preferred_element_type=F32)
            m_ref[p] = m_new

    step(0, True)

    @pl.when(i > 0)
    def _():
        def body(j, carry):
            step(j, False)
            return carry
        lax.fori_loop(1, i, body, 0)
        step(i, True)

    lam_p = lam_ref[...]
    lam = (jnp.exp(jnp.sum(lam_p[0:1] * lam_p[1:2], axis=-1, keepdims=True))
           - jnp.exp(jnp.sum(lam_p[2:3] * lam_p[3:4], axis=-1, keepdims=True)) + lambda_init)
    o = acc_ref[0] / l_ref[0] - lam * (acc_ref[1] / l_ref[1])
    o = o * lax.rsqrt(jnp.mean(o * o, axis=-1, keepdims=True) + LN_EPS)
    o_ref[...] = (o * g_ref[...] * (1.0 - lambda_init)).astype(o_ref.dtype)


def _diff_attn_mixer(h, lp, pad, w_in, lq1, lk1, lq2, lk2, subln_g, w_out, lambda_init, ln_g, ln_b, tm):
    tp, d = h.shape
    n_batch = tp // lp
    qk = DIFF_HEADS * 2 * DIFF_DQK
    col_scale = jnp.concatenate([jnp.full((qk,), DIFF_DQK ** -0.5, F32),
                                 jnp.ones((w_in.shape[1] - qk,), F32)])
    proj = _matmul(h, (w_in * col_scale[None, :]).astype(BF16), BF16, tm, _tile(w_in.shape[1], 1536, LANES))
    tq = _tile(lp, 384, SEQ_ALIGN)
    nq = lp // tq
    lam_p = jnp.stack([lq1, lk1, lq2, lk2]).astype(F32)
    o = pl.pallas_call(
        functools.partial(_diff_kernel, tq=tq, pad=pad, lambda_init=lambda_init),
        grid=(n_batch, DIFF_HEADS, nq),
        in_specs=[pl.BlockSpec((tq, LANES), lambda b, hh, i: (b * nq + i, hh)),
                  pl.BlockSpec((lp, LANES), lambda b, hh, i: (b, DIFF_HEADS + hh)),
                  pl.BlockSpec((lp, LANES), lambda b, hh, i: (b, 2 * DIFF_HEADS + hh)),
                  pl.BlockSpec((4, DIFF_DQK), lambda b, hh, i: (0, 0)),
                  pl.BlockSpec((1, DIFF_DV), lambda b, hh, i: (0, 0))],
        out_specs=pl.BlockSpec((tq, DIFF_DV), lambda b, hh, i: (b * nq + i, hh)),
        out_shape=jax.ShapeDtypeStruct((tp, DIFF_HEADS * DIFF_DV), BF16),
        scratch_shapes=[pltpu.VMEM((2, tq, 1), F32), pltpu.VMEM((2, tq, 1), F32),
                        pltpu.VMEM((2, tq, DIFF_DV), F32)],
        compiler_params=_cparams("parallel", "parallel", "arbitrary"),
        name="diff_attn",
    )(proj, proj, proj, lam_p, subln_g.reshape(1, DIFF_DV).astype(F32))
    return _matmul_res_ln(o, w_out.astype(BF16), h, ln_g, ln_b, tm)


def _router_kernel(x_ref, w_ref, b_ref, idx_ref, gate_ref, rank_ref, cnt_ref, run_ref, *, tm):
    i = pl.program_id(0)

    @pl.when(i == 0)
    def _():
        run_ref[...] = jnp.zeros(run_ref.shape, F32)

    logits = jnp.dot(x_ref[...], w_ref[...], preferred_element_type=F32,
                     precision=lax.Precision.HIGHEST) + b_ref[...]
    lane = lax.broadcasted_iota(jnp.int32, logits.shape, 1).astype(F32)
    work = jnp.where(lane < N_EXPERTS, logits, -jnp.inf)
    vals, idxs = [], []
    picked = jnp.zeros(logits.shape, F32)
    for _ in range(TOP_K):
        m = jnp.max(work, axis=-1, keepdims=True)
        sel = jnp.min(jnp.where(work == m, lane, float(LANES)), axis=-1, keepdims=True)
        hit = lane == sel
        work = jnp.where(hit, -jnp.inf, work)
        picked = jnp.where(hit, 1.0, picked)
        vals.append(m)
        idxs.append(sel)
    exps = [jnp.exp(v - vals[0]) for v in vals]
    tot = exps[0] + exps[1] + exps[2] + exps[3]
    r = lax.broadcasted_iota(jnp.int32, (tm, tm), 0)
    c = lax.broadcasted_iota(jnp.int32, (tm, tm), 1)
    lower = jnp.where(c < r, 1.0, 0.0).astype(BF16)
    before = jnp.dot(lower, picked.astype(BF16), preferred_element_type=F32) + run_ref[...]
    for kk in range(TOP_K):
        idx_ref[:, kk:kk + 1] = idxs[kk].astype(jnp.int32)
        gate_ref[:, kk:kk + 1] = exps[kk] / tot
        rank_ref[:, kk:kk + 1] = jnp.sum(jnp.where(lane == idxs[kk], before, 0.0), axis=-1,
                                         keepdims=True).astype(jnp.int32)
    run_ref[...] += jnp.sum(picked, axis=0, keepdims=True)
    cnt_ref[...] = run_ref[...]


def _expert_kernel(be_ref, nu_ref, x_ref, wu_ref, bu_ref, wd_ref, bd_ref, o_ref):
    @pl.when(pl.program_id(0) < nu_ref[0])
    def _():
        hb = jnp.dot(x_ref[...], wu_ref[0], preferred_element_type=F32) + bu_ref[0]
        x_glu = jnp.minimum(hb[:, :D_EXPERT], SWIGLU_LIMIT)
        x_lin = jnp.clip(hb[:, D_EXPERT:], -SWIGLU_LIMIT, SWIGLU_LIMIT)
        act = x_glu * jax.nn.sigmoid(SWIGLU_ALPHA * x_glu) * (x_lin + 1.0)
        o_ref[...] = jnp.dot(act.astype(BF16), wd_ref[0], preferred_element_type=F32) + bd_ref[0]


def _moe_ffn(h, w_router, b_router, w_up, b_up, w_down, b_down, ln_g, ln_b, tm, bm):
    tp, d = h.shape
    a = tp * TOP_K
    wr = jnp.zeros((d, LANES), F32).at[:, :N_EXPERTS].set(w_router.astype(F32))
    br = jnp.zeros((1, LANES), F32).at[0, :N_EXPERTS].set(b_router.astype(F32))
    row = lambda i: (i, 0)
    fix = lambda i: (0, 0)
    idx, gates, rank, cnt = pl.pallas_call(
        functools.partial(_router_kernel, tm=tm),
        grid=(tp // tm,),
        in_specs=[pl.BlockSpec((tm, d), row), pl.BlockSpec((d, LANES), fix), pl.BlockSpec((1, LANES), fix)],
        out_specs=[pl.BlockSpec((tm, TOP_K), row), pl.BlockSpec((tm, TOP_K), row),
                   pl.BlockSpec((tm, TOP_K), row), pl.BlockSpec((1, LANES), fix)],
        out_shape=[jax.ShapeDtypeStruct((tp, TOP_K), jnp.int32), jax.ShapeDtypeStruct((tp, TOP_K), F32),
                   jax.ShapeDtypeStruct((tp, TOP_K), jnp.int32), jax.ShapeDtypeStruct((1, LANES), F32)],
        scratch_shapes=[pltpu.VMEM((1, LANES), F32)],
        compiler_params=_cparams("arbitrary"),
        name="router",
    )(h, wr, br)
    counts = cnt[0, :N_EXPERTS].astype(jnp.int32)
    padded = ((counts + bm - 1) // bm) * bm
    pad_end = jnp.cumsum(padded)
    pad_start = pad_end - padded
    dest = pad_start[idx] + rank
    n_blocks = -(-a // bm) + N_EXPERTS
    n_slots = n_blocks * bm
    block_e = jnp.minimum(jnp.searchsorted(pad_end, jnp.arange(n_blocks, dtype=jnp.int32) * bm, side='right'),
                          N_EXPERTS - 1).astype(jnp.int32)
    n_used = (pad_end[-1:] // bm).astype(jnp.int32)
    tok = jnp.repeat(jnp.arange(tp, dtype=jnp.int32), TOP_K)
    slot_tok = jnp.zeros((n_slots,), jnp.int32).at[dest.reshape(a)].set(tok)
    x_slots = h.astype(BF16)[slot_tok]
    y_slots = pl.pallas_call(
        _expert_kernel,
        grid_spec=pltpu.PrefetchScalarGridSpec(
            num_scalar_prefetch=2,
            grid=(n_blocks,),
            in_specs=[pl.BlockSpec((bm, d), lambda i, be, nu: (i, 0)),
                      pl.BlockSpec((1, d, 2 * D_EXPERT), lambda i, be, nu: (be[i], 0, 0)),
                      pl.BlockSpec((1, 1, 2 * D_EXPERT), lambda i, be, nu: (be[i], 0, 0)),
                      pl.BlockSpec((1, D_EXPERT, d), lambda i, be, nu: (be[i], 0, 0)),
                      pl.BlockSpec((1, 1, d), lambda i, be, nu: (be[i], 0, 0))],
            out_specs=pl.BlockSpec((bm, d), lambda i, be, nu: (i, 0)),
        ),
        out_shape=jax.ShapeDtypeStruct((n_slots, d), F32),
        compiler_params=_cparams("arbitrary"),
        name="experts",
    )(block_e, n_used, x_slots, w_up.astype(BF16), b_up.reshape(N_EXPERTS, 1, 2 * D_EXPERT).astype(F32),
      w_down.astype(BF16), b_down.reshape(N_EXPERTS, 1, d).astype(F32))
    y = jnp.sum(gates[:, :, None] * y_slots[dest], axis=1)
    return _res_ln(h, y, ln_g, ln_b, tm)


def kernel(x, meta_tokens, s5_w_in, s5_lambda_re, s5_lambda_im, s5_log_step, s5_b_re, s5_b_im, s5_c_re, s5_c_im, s5_d, s5_w_glu, s5_b_glu, s5_w_out, ret_w_in, ret_gn_g, ret_w_out, diff_w_in, diff_lambda_q1, diff_lambda_k1, diff_lambda_q2, diff_lambda_k2, diff_subln_g, diff_w_out, ln_mix_g, ln_mix_b, moe_w_router, moe_b_router, moe_w_up, moe_b_up, moe_w_down, moe_b_down, ln_ffn_g, ln_ffn_b):
    n_batch, seq, d = x.shape
    length = seq + N_META
    pad = (-length) % SEQ_ALIGN
    lp = length + pad
    assert pad % S5_CHUNK == 0 and d == D_MODEL
    tp = n_batch * lp
    tm = _tile(tp, 768)
    meta = jnp.broadcast_to(meta_tokens[None].astype(x.dtype), (n_batch, N_META, d))
    h = jnp.concatenate([jnp.zeros((n_batch, pad, d), x.dtype), meta, x], axis=1).reshape(tp, d)
    for i in range(DEPTH):
        kind = i % N_MIXERS
        j = i // N_MIXERS
        if kind == 0:
            h = _s5_mixer(h, lp, pad, s5_w_in[j], s5_lambda_re[j], s5_lambda_im[j], s5_log_step[j],
                          s5_b_re[j], s5_b_im[j], s5_c_re[j], s5_c_im[j], s5_d[j],
                          s5_w_glu[j], s5_b_glu[j], s5_w_out[j], ln_mix_g[i], ln_mix_b[i], tm)
        elif kind == 1:
            h = _retention_mixer(h, lp, pad, ret_w_in[j], ret_gn_g[j], ret_w_out[j],
                                 ln_mix_g[i], ln_mix_b[i], tm)
        else:
            lambda_init = 0.8 - 0.6 * math.exp(-0.3 * i)
            h = _diff_attn_mixer(h, lp, pad, diff_w_in[j], diff_lambda_q1[j], diff_lambda_k1[j],
                                 diff_lambda_q2[j], diff_lambda_k2[j], diff_subln_g[j], diff_w_out[j],
                                 lambda_init, ln_mix_g[i], ln_mix_b[i], tm)
        h = _moe_ffn(h, moe_w_router[i], moe_b_router[i], moe_w_up[i], moe_b_up[i],
                     moe_w_down[i], moe_b_down[i], ln_ffn_g[i], ln_ffn_b[i], tm, 256)
    return h.reshape(n_batch, lp, d)[:, pad + N_META:, :]
```

```python
import functools
import math

import jax
import jax.numpy as jnp
from jax import lax
from jax.experimental import pallas as pl
from jax.experimental.pallas import tpu as pltpu

F32 = jnp.float32
BF16 = jnp.bfloat16

D_MODEL = 1024
DEPTH = 4
N_META = 16
N_MIXERS = 3
S5_GROUP_CH = 16
S5_GROUPS = D_MODEL // S5_GROUP_CH
S5_STATE = 64
S5_CHUNK = 16
S5_OCT = 8
N_OCT = S5_GROUPS // S5_OCT
RET_HEADS = 4
RET_DK = D_MODEL // RET_HEADS
RET_DV = 2 * RET_DK
RET_THETA = 10000.0
DIFF_HEADS = D_MODEL // 128
DIFF_DQK = 64
DIFF_DV = 128
N_EXPERTS = 32
TOP_K = 4
D_EXPERT = D_MODEL
SWIGLU_LIMIT = 7.0
SWIGLU_ALPHA = 1.702
LN_EPS = 1e-5
MASK_VALUE = -1e30
DEEPNORM_ALPHA = (2 * DEPTH) ** 0.25
SEQ_ALIGN = 128
LANES = 128
VMEM_LIMIT = 56 * 1024 * 1024


def _cparams(*sem):
    return pltpu.CompilerParams(dimension_semantics=sem, vmem_limit_bytes=VMEM_LIMIT)


def _tile(n, target, mult=8):
    best = None
    for t in range(mult, min(n, target) + 1, mult):
        if n % t == 0:
            best = t
    assert best is not None, (n, target, mult)
    return best


def _layer_norm_rows(v, g, b):
    mu = jnp.mean(v, axis=-1, keepdims=True)
    c = v - mu
    var = jnp.mean(c * c, axis=-1, keepdims=True)
    return c * lax.rsqrt(var + LN_EPS) * g + b


def _mm_kernel(x_ref, w_ref, o_ref):
    o_ref[...] = jnp.dot(x_ref[...].astype(BF16), w_ref[...],
                         preferred_element_type=F32).astype(o_ref.dtype)


def _matmul(x, w, out_dtype, tm, tn):
    m, k = x.shape
    n = w.shape[1]
    return pl.pallas_call(
        _mm_kernel,
        grid=(n // tn, m // tm),
        in_specs=[pl.BlockSpec((tm, k), lambda j, i: (i, 0)),
                  pl.BlockSpec((k, tn), lambda j, i: (0, j))],
        out_specs=pl.BlockSpec((tm, tn), lambda j, i: (i, j)),
        out_shape=jax.ShapeDtypeStruct((m, n), out_dtype),
        compiler_params=_cparams("parallel", "parallel"),
        name="matmul",
    )(x, w)


def _mm_res_ln_kernel(x_ref, w_ref, res_ref, g_ref, b_ref, o_ref):
    y = jnp.dot(x_ref[...].astype(BF16), w_ref[...], preferred_element_type=F32)
    o_ref[...] = _layer_norm_rows(DEEPNORM_ALPHA * res_ref[...] + y, g_ref[...], b_ref[...])


def _matmul_res_ln(x, w, res, g, b, tm):
    m, k = x.shape
    n = w.shape[1]
    row = lambda i: (i, 0)
    fix = lambda i: (0, 0)
    return pl.pallas_call(
        _mm_res_ln_kernel,
        grid=(m // tm,),
        in_specs=[pl.BlockSpec((tm, k), row), pl.BlockSpec((k, n), fix), pl.BlockSpec((tm, n), row),
                  pl.BlockSpec((1, n), fix), pl.BlockSpec((1, n), fix)],
        out_specs=pl.BlockSpec((tm, n), row),
        out_shape=jax.ShapeDtypeStruct((m, n), F32),
        compiler_params=_cparams("parallel"),
        name="matmul_res_ln",
    )(x, w, res, g.reshape(1, n), b.reshape(1, n))


def _expand_block_diag(a, row_inner, outer, inner):
    rows, k = a.shape
    cols = outer * S5_OCT * inner
    r = jnp.arange(k)[:, None]
    c = jnp.arange(cols)[None, :]
    rep = ((r // inner == c // (S5_OCT * inner)) & (r % inner == c % inner)).astype(BF16)
    out = jnp.dot(a.astype(BF16), rep, preferred_element_type=F32)
    g_row = (jnp.arange(rows)[:, None] // row_inner) % S5_OCT
    h_col = (c // inner) % S5_OCT
    return jnp.where(g_row == h_col, out, 0.0).astype(BF16)


def _s5_tables(lam_re, lam_im, log_step, b_re, b_im, c_re, c_im, d_skip):
    hp = lax.Precision.HIGHEST
    g_, n_, c_, q_, o_ = S5_GROUPS, S5_STATE, S5_GROUP_CH, S5_CHUNK, S5_OCT
    dt = jnp.exp(log_step.astype(F32))[:, None]
    lr = jnp.minimum(lam_re.astype(F32), -1e-4)
    li = lam_im.astype(F32)
    mag = jnp.exp(lr * dt)
    ab_re = mag * jnp.cos(li * dt)
    ab_im = mag * jnp.sin(li * dt)
    den = lr * lr + li * li
    f_re = ((ab_re - 1.0) * lr + ab_im * li) / den
    f_im = (ab_im * lr - (ab_re - 1.0) * li) / den
    br, bi = b_re.astype(F32), b_im.astype(F32)
    bbar_re = f_re[..., None] * br - f_im[..., None] * bi
    bbar_im = f_re[..., None] * bi + f_im[..., None] * br
    j = jnp.arange(q_ + 1, dtype=F32)[:, None, None]
    pmag = jnp.exp(j * (lr * dt)[None])
    pw_re = pmag * jnp.cos(j * (li * dt)[None])
    pw_im = pmag * jnp.sin(j * (li * dt)[None])
    cr, ci = c_re.astype(F32), c_im.astype(F32)
    w_re = pw_re[..., None] * bbar_re[None] - pw_im[..., None] * bbar_im[None]
    w_im = pw_re[..., None] * bbar_im[None] + pw_im[..., None] * bbar_re[None]
    kj = (jnp.einsum('gon,jgni->jgoi', cr, w_re[:q_], precision=hp)
          - jnp.einsum('gon,jgni->jgoi', ci, w_im[:q_], precision=hp))
    kj = kj.at[0].add(d_skip.astype(F32).reshape(g_, c_)[:, :, None] * jnp.eye(c_, dtype=F32)[None])
    s_idx = jnp.arange(q_)[:, None]
    t_idx = jnp.arange(q_)[None, :]
    lag = t_idx - s_idx
    kts = jnp.where((lag >= 0)[:, :, None, None, None], kj[jnp.clip(lag, 0, q_ - 1)], 0.0)
    kts = kts.reshape(q_, q_, N_OCT, o_, c_, c_).transpose(2, 0, 3, 5, 1, 4)
    m_tab = _expand_block_diag(kts.reshape(N_OCT * q_ * o_ * c_, q_ * c_), c_, q_, c_)
    m_tab = m_tab.reshape(N_OCT, q_, o_ * c_, q_ * o_ * c_)
    pst = jnp.stack([w_re[:q_][::-1], w_im[:q_][::-1]], axis=0)
    pst = pst.reshape(2, q_, N_OCT, o_, n_, c_).transpose(2, 1, 3, 5, 0, 4)
    p_tab = _expand_block_diag(pst.reshape(N_OCT * q_ * o_ * c_, 2 * n_), c_, 2, n_)
    p_tab = p_tab.reshape(N_OCT, q_, o_ * c_, 2 * o_ * n_)
    ar = pw_re[1:].reshape(q_, N_OCT, o_, n_)
    ai = pw_im[1:].reshape(q_, N_OCT, o_, n_)
    cr_o = cr.reshape(N_OCT, o_, c_, n_)
    ci_o = ci.reshape(N_OCT, o_, c_, n_)
    r_re = cr_o[None] * ar[:, :, :, None, :] - ci_o[None] * ai[:, :, :, None, :]
    r_im = -(cr_o[None] * ai[:, :, :, None, :] + ci_o[None] * ar[:, :, :, None, :])
    rst = jnp.stack([r_re, r_im], axis=0).transpose(2, 0, 3, 5, 1, 4)
    r_tab = _expand_block_diag(rst.reshape(N_OCT * 2 * o_ * n_, q_ * c_), n_, q_, c_)
    r_tab = r_tab.reshape(N_OCT, 2 * o_ * n_, q_ * o_ * c_)
    a16_re = pw_re[q_].reshape(N_OCT, o_ * n_)
    a16_im = pw_im[q_].reshape(N_OCT, o_ * n_)
    return m_tab, p_tab, r_tab, a16_re, a16_im


def _s5_chunk_rows(u_ref, s, tr, valid):
    us = u_ref[pl.ds(s, tr, stride=S5_CHUNK), :]
    return jnp.where(valid, us, 0.0).astype(BF16)


def _s5_valid(tr, chunks_per_batch, pad_chunks):
    chunk = pl.program_id(1) * tr + lax.broadcasted_iota(jnp.int32, (tr, 1), 0)
    return (chunk % chunks_per_batch) >= pad_chunks


def _s5_state_kernel(u_ref, p_ref, s_ref, *, tr, chunks_per_batch, pad_chunks):
    valid = _s5_valid(tr, chunks_per_batch, pad_chunks)
    acc = jnp.zeros(s_ref.shape, F32)
    for s in range(S5_CHUNK):
        acc += jnp.dot(_s5_chunk_rows(u_ref, s, tr, valid), p_ref[0, s], preferred_element_type=F32)
    s_ref[...] = acc


def _s5_scan_kernel(s_ref, ar_ref, ai_ref, x_ref, st_ref, *, tc):
    half = st_ref.shape[1] // 2

    @pl.when(pl.program_id(1) == 0)
    def _():
        st_ref[...] = jnp.zeros(st_ref.shape, F32)

    ar = ar_ref[...]
    ai = ai_ref[...]

    def body(c, carry):
        xr, xi = carry
        x_ref[c, :, :half] = xr
        x_ref[c, :, half:] = xi
        s = s_ref[c]
        return ar * xr - ai * xi + s[:, :half], ar * xi + ai * xr + s[:, half:]

    xr, xi = lax.fori_loop(0, tc, body, (st_ref[:, :half], st_ref[:, half:]))
    st_ref[:, :half] = xr
    st_ref[:, half:] = xi


def _gelu_tanh(y):
    return 0.5 * y * (1.0 + jnp.tanh(math.sqrt(2.0 / math.pi) * (y + 0.044715 * (y * y * y))))


def _s5_out_kernel(u_ref, xp_ref, m_ref, r_ref, z_ref, acc_ref, *, tr, chunks_per_batch, pad_chunks):
    valid = _s5_valid(tr, chunks_per_batch, pad_chunks)
    acc_ref[...] = jnp.dot(xp_ref[...].astype(BF16), r_ref[0], preferred_element_type=F32)
    for s in range(S5_CHUNK):
        acc_ref[:, s * LANES:] += jnp.dot(_s5_chunk_rows(u_ref, s, tr, valid), m_ref[0, s, :, s * LANES:],
                                          preferred_element_type=F32)
    for t in range(S5_CHUNK):
        z_ref[pl.ds(t, tr, stride=S5_CHUNK), :] = _gelu_tanh(acc_ref[:, t * LANES:(t + 1) * LANES])


def _s5_tail_kernel(z_ref, wg_ref, bg_ref, wo_ref, res_ref, g_ref, b_ref, o_ref):
    z = z_ref[...]
    t = jnp.dot(z.astype(BF16), wg_ref[...], preferred_element_type=F32) + bg_ref[...]
    glu = z * jax.nn.sigmoid(t)
    y = jnp.dot(glu.astype(BF16), wo_ref[...], preferred_element_type=F32)
    o_ref[...] = _layer_norm_rows(DEEPNORM_ALPHA * res_ref[...] + y, g_ref[...], b_ref[...])


def _s5_mixer(h, lp, pad, w_in, lam_re, lam_im, log_step, b_re, b_im, c_re, c_im, d_skip,
              w_glu, b_glu, w_out, ln_g, ln_b, tm):
    tp, d = h.shape
    n_batch = tp // lp
    m_tab, p_tab, r_tab, a16_re, a16_im = _s5_tables(lam_re, lam_im, log_step, b_re, b_im,
                                                     c_re, c_im, d_skip)
    u = _matmul(h, w_in.astype(BF16), F32, tm, d)
    n_chunks = tp // S5_CHUNK
    chunks_per_batch = lp // S5_CHUNK
    pad_chunks = pad // S5_CHUNK
    tr = _tile(n_chunks, 344)
    st_w = 2 * S5_OCT * S5_STATE
    kw = dict(tr=tr, chunks_per_batch=chunks_per_batch, pad_chunks=pad_chunks)
    s_all = pl.pallas_call(
        functools.partial(_s5_state_kernel, **kw),
        grid=(N_OCT, n_chunks // tr),
        in_specs=[pl.BlockSpec((tr * S5_CHUNK, LANES), lambda o, i: (i, o)),
                  pl.BlockSpec((1, S5_CHUNK, LANES, st_w), lambda o, i: (o, 0, 0, 0))],
        out_specs=pl.BlockSpec((tr, st_w), lambda o, i: (i, o)),
        out_shape=jax.ShapeDtypeStruct((n_chunks, N_OCT * st_w), F32),
        compiler_params=_cparams("parallel", "parallel"),
        name="s5_state",
    )(u, p_tab)
    tc = _tile(chunks_per_batch, 129, 1)
    nt = chunks_per_batch // tc
    x_prev = pl.pallas_call(
        functools.partial(_s5_scan_kernel, tc=tc),
        grid=(n_batch, nt),
        in_specs=[pl.BlockSpec((tc, N_OCT, st_w), lambda b, i: (b * nt + i, 0, 0)),
                  pl.BlockSpec((N_OCT, st_w // 2), lambda b, i: (0, 0)),
                  pl.BlockSpec((N_OCT, st_w // 2), lambda b, i: (0, 0))],
        out_specs=pl.BlockSpec((tc, N_OCT, st_w), lambda b, i: (b * nt + i, 0, 0)),
        out_shape=jax.ShapeDtypeStruct((n_chunks, N_OCT, st_w), F32),
        scratch_shapes=[pltpu.VMEM((N_OCT, st_w), F32)],
        compiler_params=_cparams("arbitrary", "arbitrary"),
        name="s5_scan",
    )(s_all.reshape(n_chunks, N_OCT, st_w), a16_re, a16_im)
    z = pl.pallas_call(
        functools.partial(_s5_out_kernel, **kw),
        grid=(N_OCT, n_chunks // tr),
        in_specs=[pl.BlockSpec((tr * S5_CHUNK, LANES), lambda o, i: (i, o)),
                  pl.BlockSpec((tr, st_w), lambda o, i: (i, o)),
                  pl.BlockSpec((1, S5_CHUNK, LANES, S5_CHUNK * LANES), lambda o, i: (o, 0, 0, 0)),
                  pl.BlockSpec((1, st_w, S5_CHUNK * LANES), lambda o, i: (o, 0, 0))],
        out_specs=pl.BlockSpec((tr * S5_CHUNK, LANES), lambda o, i: (i, o)),
        out_shape=jax.ShapeDtypeStruct((tp, d), F32),
        scratch_shapes=[pltpu.VMEM((tr, S5_CHUNK * LANES), F32)],
        compiler_params=_cparams("parallel", "parallel"),
        name="s5_out",
    )(u, x_prev.reshape(n_chunks, N_OCT * st_w), m_tab, r_tab)
    row = lambda i: (i, 0)
    fix = lambda i: (0, 0)
    return pl.pallas_call(
        _s5_tail_kernel,
        grid=(tp // tm,),
        in_specs=[pl.BlockSpec((tm, d), row), pl.BlockSpec((d, d), fix), pl.BlockSpec((1, d), fix),
                  pl.BlockSpec((d, d), fix), pl.BlockSpec((tm, d), row),
                  pl.BlockSpec((1, d), fix), pl.BlockSpec((1, d), fix)],
        out_specs=pl.BlockSpec((tm, d), row),
        out_shape=jax.ShapeDtypeStruct((tp, d), F32),
        compiler_params=_cparams("parallel"),
        name="s5_tail",
    )(z, w_glu.astype(BF16), b_glu.reshape(1, d).astype(F32), w_out.astype(BF16), h,
      ln_g.reshape(1, d), ln_b.reshape(1, d))


def _ret_kernel(q_ref, k_ref, v_ref, gate_ref, cos_ref, sin_ref, dm_ref, qd_ref, kd_ref, cd_ref, g_ref,
                o_ref, st_ref, *, chunk, pad):
    c = pl.program_id(2)

    @pl.when(c == 0)
    def _():
        st_ref[...] = jnp.zeros(st_ref.shape, F32)

    cos = cos_ref[...]
    sin = sin_ref[...]
    half = RET_DK // 2

    def rot(t):
        t1 = t[:, :half]
        t2 = t[:, half:]
        return jnp.concatenate([t1 * cos - t2 * sin, t1 * sin + t2 * cos], axis=-1)

    q = rot(q_ref[...])
    k = rot(k_ref[...]) * (RET_DK ** -0.5)
    pos = c * chunk + lax.broadcasted_iota(jnp.int32, (chunk, 1), 0)
    k = jnp.where(pos >= pad, k, 0.0)
    qb = q.astype(BF16)
    vb = v_ref[...].astype(BF16)
    scores = lax.dot_general(qb, k.astype(BF16), (((1,), (1,)), ((), ())),
                             preferred_element_type=F32) * dm_ref[0]
    intra = jnp.dot(scores.astype(BF16), vb, preferred_element_type=F32)
    state = st_ref[...]
    inter = jnp.dot(qb, state.astype(BF16), preferred_element_type=F32) * qd_ref[0]
    st_ref[...] = state * cd_ref[0] + lax.dot_general((k * kd_ref[0]).astype(BF16), vb,
                                                      (((0,), (0,)), ((), ())),
                                                      preferred_element_type=F32)
    o = intra + inter
    mu = jnp.mean(o, axis=-1, keepdims=True)
    oc = o - mu
    var = jnp.mean(oc * oc, axis=-1, keepdims=True)
    o = oc * lax.rsqrt(var + LN_EPS) * g_ref[...]
    gate = gate_ref[...]
    o_ref[...] = (gate * jax.nn.sigmoid(gate) * o).astype(o_ref.dtype)


def _retention_mixer(h, lp, pad, w_in, gn_g, w_out, ln_g, ln_b, tm):
    tp, d = h.shape
    n_batch = tp // lp
    chunk = SEQ_ALIGN
    ncb = lp // chunk
    qd = RET_HEADS * RET_DK
    vd = RET_HEADS * RET_DV
    proj = _matmul(h, w_in.astype(BF16), F32, tm, _tile(w_in.shape[1], 1536, LANES))
    half = RET_DK // 2
    pos = jnp.arange(lp, dtype=F32) - pad
    inv_freq = jnp.power(RET_THETA, -jnp.arange(half, dtype=F32) / half)
    ang = pos[:, None] * inv_freq[None, :]
    cos, sin = jnp.cos(ang), jnp.sin(ang)
    log_gamma = jnp.log1p(-jnp.power(2.0, -5.0 - jnp.arange(RET_HEADS, dtype=F32)))
    idx = jnp.arange(chunk, dtype=F32)
    rel = idx[:, None] - idx[None, :]
    dmask = jnp.where(rel[None] >= 0, jnp.exp(log_gamma[:, None, None] * jnp.maximum(rel, 0.0)[None]), 0.0)
    q_decay = jnp.exp(log_gamma[:, None] * (idx[None, :] + 1.0))[:, :, None]
    k_decay = jnp.exp(log_gamma[:, None] * (chunk - 1.0 - idx[None, :]))[:, :, None]
    chunk_decay = jnp.exp(log_gamma * chunk).reshape(RET_HEADS, 1, 1)
    kb = qd // RET_DK
    vb = 2 * qd // RET_DV
    gb = (2 * qd + vd) // RET_DV
    o = pl.pallas_call(
        functools.partial(_ret_kernel, chunk=chunk, pad=pad),
        grid=(n_batch, RET_HEADS, ncb),
        in_specs=[pl.BlockSpec((chunk, RET_DK), lambda b, hh, c: (b * ncb + c, hh)),
                  pl.BlockSpec((chunk, RET_DK), lambda b, hh, c: (b * ncb + c, kb + hh)),
                  pl.BlockSpec((chunk, RET_DV), lambda b, hh, c: (b * ncb + c, vb + hh)),
                  pl.BlockSpec((chunk, RET_DV), lambda b, hh, c: (b * ncb + c, gb + hh)),
                  pl.BlockSpec((chunk, half), lambda b, hh, c: (c, 0)),
                  pl.BlockSpec((chunk, half), lambda b, hh, c: (c, 0)),
                  pl.BlockSpec((1, chunk, chunk), lambda b, hh, c: (hh, 0, 0)),
                  pl.BlockSpec((1, chunk, 1), lambda b, hh, c: (hh, 0, 0)),
                  pl.BlockSpec((1, chunk, 1), lambda b, hh, c: (hh, 0, 0)),
                  pl.BlockSpec((1, 1, 1), lambda b, hh, c: (hh, 0, 0)),
                  pl.BlockSpec((1, RET_DV), lambda b, hh, c: (0, hh))],
        out_specs=pl.BlockSpec((chunk, RET_DV), lambda b, hh, c: (b * ncb + c, hh)),
        out_shape=jax.ShapeDtypeStruct((tp, vd), BF16),
        scratch_shapes=[pltpu.VMEM((RET_DK, RET_DV), F32)],
        compiler_params=_cparams("parallel", "parallel", "arbitrary"),
        name="retention",
    )(proj, proj, proj, proj, cos, sin, dmask, q_decay, k_decay, chunk_decay,
      gn_g.reshape(1, vd).astype(F32))
    return _matmul_res_ln(o, w_out.astype(BF16), h, ln_g, ln_b, tm)


ONES_ROWS = 16


def _diff_kernel(q_ref, k_ref, vt_ref, lam_ref, g_ref, o_ref, vext_ref, m_ref, acc_ref,
                 *, tq, pad, lambda_init):
    i = pl.program_id(2)
    nkb = vext_ref.shape[0]

    @pl.when(i == 0)
    def _():
        for jb in range(nkb):
            vext_ref[jb, 0:DIFF_DV, :] = vt_ref[:, jb * tq:(jb + 1) * tq]
            vext_ref[jb, DIFF_DV:, :] = jnp.ones((ONES_ROWS, tq), BF16)

    q = q_ref[...]
    lane = lax.broadcasted_iota(jnp.int32, q.shape, 1)
    zero = jnp.zeros_like(q)
    q_both = jnp.concatenate([jnp.where(lane < DIFF_DQK, q, zero), jnp.where(lane >= DIFF_DQK, q, zero)],
                             axis=0)
    m_ref[...] = jnp.full(m_ref.shape, MASK_VALUE, F32)
    acc_ref[...] = jnp.zeros(acc_ref.shape, F32)

    def step(j, masked):
        start = j * tq if isinstance(j, int) else pl.multiple_of(j * tq, tq)
        k = k_ref[pl.ds(start, tq), :]
        s = lax.dot_general(k, q_both, (((1,), (1,)), ((), ())), preferred_element_type=F32)
        if masked:
            kpos = j * tq + lax.broadcasted_iota(jnp.int32, (tq, tq), 0)
            qpos = i * tq + lax.broadcasted_iota(jnp.int32, (tq, tq), 1)
            allowed = (kpos <= qpos) & (kpos >= pad)
            allowed = jnp.concatenate([allowed, allowed], axis=1)
            s = jnp.where(allowed, s, MASK_VALUE)
        m_prev = m_ref[...]
        m_new = jnp.maximum(m_prev, jnp.max(s, axis=0, keepdims=True))
        alpha = jnp.exp2(m_prev - m_new)
        pe = jnp.exp2((s - m_new).astype(BF16))
        acc_ref[...] = alpha * acc_ref[...] + jnp.dot(vext_ref[j], pe, preferred_element_type=F32)
        m_ref[...] = m_new

    step(0, True)

    @pl.when(i > 0)
    def _():
        def body(j, carry):
            step(j, False)
            return carry
        lax.fori_loop(1, i, body, 0)
        step(i, True)

    lam_p = lam_ref[...]
    lam = (jnp.exp(jnp.sum(lam_p[0:1] * lam_p[1:2], axis=-1, keepdims=True))
           - jnp.exp(jnp.sum(lam_p[2:3] * lam_p[3:4], axis=-1, keepdims=True)) + lambda_init)
    acc = acc_ref[...]
    o_both = acc[0:DIFF_DV] / acc[DIFF_DV:DIFF_DV + 1]
    o = o_both[:, :tq] - lam * o_both[:, tq:]
    o = o * lax.rsqrt(jnp.mean(o * o, axis=0, keepdims=True) + LN_EPS)
    o = o * (g_ref[...] * (1.0 - lambda_init))
    o_ref[...] = o.T.astype(o_ref.dtype)


def _mm_nt_kernel(wt_ref, x_ref, o_ref):
    o_ref[...] = lax.dot_general(wt_ref[...], x_ref[...].astype(BF16), (((1,), (1,)), ((), ())),
                                 preferred_element_type=F32).astype(o_ref.dtype)


def _matmul_nt(wt, x, out_dtype, tm):
    n, k = wt.shape
    m = x.shape[0]
    return pl.pallas_call(
        _mm_nt_kernel,
        grid=(m // tm,),
        in_specs=[pl.BlockSpec((n, k), lambda i: (0, 0)), pl.BlockSpec((tm, k), lambda i: (i, 0))],
        out_specs=pl.BlockSpec((n, tm), lambda i: (0, i)),
        out_shape=jax.ShapeDtypeStruct((n, m), out_dtype),
        compiler_params=_cparams("parallel"),
        name="matmul_nt",
    )(wt, x)


def _diff_attn_mixer(h, lp, pad, w_in, lq1, lk1, lq2, lk2, subln_g, w_out, lambda_init, ln_g, ln_b, tm):
    tp, d = h.shape
    n_batch = tp // lp
    qk = DIFF_HEADS * 2 * DIFF_DQK
    col_scale = jnp.concatenate([jnp.full((qk,), DIFF_DQK ** -0.5 * math.log2(math.e), F32),
                                 jnp.ones((qk,), F32)])
    proj = _matmul(h, (w_in[:, :2 * qk] * col_scale[None, :]).astype(BF16), BF16, tm, _tile(2 * qk, 2048, LANES))
    vt = _matmul_nt(w_in[:, 2 * qk:].T.astype(BF16), h, BF16, _tile(tp, 768, LANES))
    tq = _tile(lp, 384, SEQ_ALIGN)
    nq = lp // tq
    lam_p = jnp.stack([lq1, lk1, lq2, lk2]).astype(F32)
    o = pl.pallas_call(
        functools.partial(_diff_kernel, tq=tq, pad=pad, lambda_init=lambda_init),
        grid=(n_batch, DIFF_HEADS, nq),
        in_specs=[pl.BlockSpec((tq, LANES), lambda b, hh, i: (b * nq + i, hh)),
                  pl.BlockSpec((lp, LANES), lambda b, hh, i: (b, DIFF_HEADS + hh)),
                  pl.BlockSpec((DIFF_DV, lp), lambda b, hh, i: (hh, b)),
                  pl.BlockSpec((4, DIFF_DQK), lambda b, hh, i: (0, 0)),
                  pl.BlockSpec((DIFF_DV, 1), lambda b, hh, i: (0, 0))],
        out_specs=pl.BlockSpec((tq, DIFF_DV), lambda b, hh, i: (b * nq + i, hh)),
        out_shape=jax.ShapeDtypeStruct((tp, DIFF_HEADS * DIFF_DV), BF16),
        scratch_shapes=[pltpu.VMEM((nq, DIFF_DV + ONES_ROWS, tq), BF16),
                        pltpu.VMEM((1, 2 * tq), F32),
                        pltpu.VMEM((DIFF_DV + ONES_ROWS, 2 * tq), F32)],
        compiler_params=_cparams("parallel", "parallel", "arbitrary"),
        name="diff_attn",
    )(proj, proj, vt, lam_p, subln_g.reshape(DIFF_DV, 1).astype(F32))
    return _matmul_res_ln(o, w_out.astype(BF16), h, ln_g, ln_b, tm)


def _router_kernel(x_ref, w_ref, b_ref, idx_ref, gate_ref, rank_ref, cnt_ref, run_ref, *, tm):
    i = pl.program_id(0)

    @pl.when(i == 0)
    def _():
        run_ref[...] = jnp.zeros(run_ref.shape, F32)

    logits = jnp.dot(x_ref[...], w_ref[...], preferred_element_type=F32,
                     precision=lax.Precision.HIGHEST) + b_ref[...]
    lane = lax.broadcasted_iota(jnp.int32, logits.shape, 1).astype(F32)
    work = jnp.where(lane < N_EXPERTS, logits, -jnp.inf)
    vals, idxs = [], []
    picked = jnp.zeros(logits.shape, F32)
    for _ in range(TOP_K):
        m = jnp.max(work, axis=-1, keepdims=True)
        sel = jnp.min(jnp.where(work == m, lane, float(LANES)), axis=-1, keepdims=True)
        hit = lane == sel
        work = jnp.where(hit, -jnp.inf, work)
        picked = jnp.where(hit, 1.0, picked)
        vals.append(m)
        idxs.append(sel)
    exps = [jnp.exp(v - vals[0]) for v in vals]
    tot = exps[0] + exps[1] + exps[2] + exps[3]
    r = lax.broadcasted_iota(jnp.int32, (tm, tm), 0)
    c = lax.broadcasted_iota(jnp.int32, (tm, tm), 1)
    lower = jnp.where(c < r, 1.0, 0.0).astype(BF16)
    before = jnp.dot(lower, picked.astype(BF16), preferred_element_type=F32) + run_ref[...]
    for kk in range(TOP_K):
        idx_ref[:, kk:kk + 1] = idxs[kk].astype(jnp.int32)
        gate_ref[:, kk:kk + 1] = exps[kk] / tot
        rank_ref[:, kk:kk + 1] = jnp.sum(jnp.where(lane == idxs[kk], before, 0.0), axis=-1,
                                         keepdims=True).astype(jnp.int32)
    run_ref[...] += jnp.sum(picked, axis=0, keepdims=True)
    cnt_ref[...] = run_ref[...]


def _expert_kernel(be_ref, nu_ref, x_ref, wu_ref, bu_ref, wd_ref, bd_ref, o_ref, wu_bf_ref, wd_bf_ref):
    i = pl.program_id(0)
    live = i < nu_ref[0]

    @pl.when(live & ((i == 0) | (be_ref[i] != be_ref[jnp.maximum(i - 1, 0)])))
    def _():
        wu_bf_ref[...] = wu_ref[0].astype(BF16)
        wd_bf_ref[...] = wd_ref[0].astype(BF16)

    @pl.when(live)
    def _():
        hb = jnp.dot(x_ref[...], wu_bf_ref[...], preferred_element_type=F32) + bu_ref[0]
        x_glu = jnp.minimum(hb[:, :D_EXPERT], SWIGLU_LIMIT)
        x_lin = jnp.clip(hb[:, D_EXPERT:], -SWIGLU_LIMIT, SWIGLU_LIMIT)
        act = x_glu * jax.nn.sigmoid(SWIGLU_ALPHA * x_glu) * (x_lin + 1.0)
        o_ref[...] = jnp.dot(act.astype(BF16), wd_bf_ref[...], preferred_element_type=F32) + bd_ref[0]


def _combine_ln_kernel(res_ref, y0_ref, y1_ref, y2_ref, y3_ref, gate_ref, g_ref, b_ref, o_ref):
    gates = gate_ref[...]
    y = (gates[:, 0:1] * y0_ref[...] + gates[:, 1:2] * y1_ref[...]
         + gates[:, 2:3] * y2_ref[...] + gates[:, 3:4] * y3_ref[...])
    o_ref[...] = _layer_norm_rows(DEEPNORM_ALPHA * res_ref[...] + y, g_ref[...], b_ref[...])


def _moe_ffn(h, w_router, b_router, w_up, b_up, w_down, b_down, ln_g, ln_b, tm, bm):
    tp, d = h.shape
    a = tp * TOP_K
    wr = jnp.zeros((d, LANES), F32).at[:, :N_EXPERTS].set(w_router.astype(F32))
    br = jnp.zeros((1, LANES), F32).at[0, :N_EXPERTS].set(b_router.astype(F32))
    row = lambda i: (i, 0)
    fix = lambda i: (0, 0)
    idx, gates, rank, cnt = pl.pallas_call(
        functools.partial(_router_kernel, tm=tm),
        grid=(tp // tm,),
        in_specs=[pl.BlockSpec((tm, d), row), pl.BlockSpec((d, LANES), fix), pl.BlockSpec((1, LANES), fix)],
        out_specs=[pl.BlockSpec((tm, TOP_K), row), pl.BlockSpec((tm, TOP_K), row),
                   pl.BlockSpec((tm, TOP_K), row), pl.BlockSpec((1, LANES), fix)],
        out_shape=[jax.ShapeDtypeStruct((tp, TOP_K), jnp.int32), jax.ShapeDtypeStruct((tp, TOP_K), F32),
                   jax.ShapeDtypeStruct((tp, TOP_K), jnp.int32), jax.ShapeDtypeStruct((1, LANES), F32)],
        scratch_shapes=[pltpu.VMEM((1, LANES), F32)],
        compiler_params=_cparams("arbitrary"),
        name="router",
    )(h, wr, br)
    counts = cnt[0, :N_EXPERTS].astype(jnp.int32)
    padded = ((counts + bm - 1) // bm) * bm
    pad_end = jnp.cumsum(padded)
    pad_start = pad_end - padded
    dest = pad_start[idx] + rank
    n_blocks = -(-a // bm) + N_EXPERTS
    n_slots = n_blocks * bm
    block_start = jnp.arange(n_blocks, dtype=jnp.int32) * bm
    block_e = jnp.minimum(jnp.sum((pad_end[None, :] <= block_start[:, None]).astype(jnp.int32), axis=1),
                          N_EXPERTS - 1)
    n_used = (pad_end[-1:] // bm).astype(jnp.int32)
    tok = jnp.repeat(jnp.arange(tp, dtype=jnp.int32), TOP_K)
    slot_tok = jnp.zeros((n_slots,), jnp.int32).at[dest.reshape(a)].set(tok)
    x_slots = h.astype(BF16)[slot_tok]
    y_slots = pl.pallas_call(
        _expert_kernel,
        grid_spec=pltpu.PrefetchScalarGridSpec(
            num_scalar_prefetch=2,
            grid=(n_blocks,),
            in_specs=[pl.BlockSpec((bm, d), lambda i, be, nu: (i, 0)),
                      pl.BlockSpec((1, d, 2 * D_EXPERT), lambda i, be, nu: (be[i], 0, 0)),
                      pl.BlockSpec((1, 1, 2 * D_EXPERT), lambda i, be, nu: (be[i], 0, 0)),
                      pl.BlockSpec((1, D_EXPERT, d), lambda i, be, nu: (be[i], 0, 0)),
                      pl.BlockSpec((1, 1, d), lambda i, be, nu: (be[i], 0, 0))],
            out_specs=pl.BlockSpec((bm, d), lambda i, be, nu: (i, 0)),
            scratch_shapes=[pltpu.VMEM((d, 2 * D_EXPERT), BF16), pltpu.VMEM((D_EXPERT, d), BF16)],
        ),
        out_shape=jax.ShapeDtypeStruct((n_slots, d), F32),
        compiler_params=_cparams("arbitrary"),
        name="experts",
    )(block_e, n_used, x_slots, w_up, b_up.reshape(N_EXPERTS, 1, 2 * D_EXPERT).astype(F32),
      w_down, b_down.reshape(N_EXPERTS, 1, d).astype(F32))
    y_k = [y_slots[dest[:, kk]] for kk in range(TOP_K)]
    return pl.pallas_call(
        _combine_ln_kernel,
        grid=(tp // tm,),
        in_specs=[pl.BlockSpec((tm, d), row)] * (1 + TOP_K)
                 + [pl.BlockSpec((tm, TOP_K), row), pl.BlockSpec((1, d), fix), pl.BlockSpec((1, d), fix)],
        out_specs=pl.BlockSpec((tm, d), row),
        out_shape=jax.ShapeDtypeStruct((tp, d), F32),
        compiler_params=_cparams("parallel"),
        name="moe_combine_ln",
    )(h, *y_k, gates, ln_g.reshape(1, d), ln_b.reshape(1, d))


def kernel(x, meta_tokens, s5_w_in, s5_lambda_re, s5_lambda_im, s5_log_step, s5_b_re, s5_b_im, s5_c_re, s5_c_im, s5_d, s5_w_glu, s5_b_glu, s5_w_out, ret_w_in, ret_gn_g, ret_w_out, diff_w_in, diff_lambda_q1, diff_lambda_k1, diff_lambda_q2, diff_lambda_k2, diff_subln_g, diff_w_out, ln_mix_g, ln_mix_b, moe_w_router, moe_b_router, moe_w_up, moe_b_up, moe_w_down, moe_b_down, ln_ffn_g, ln_ffn_b):
    n_batch, seq, d = x.shape
    length = seq + N_META
    pad = (-length) % SEQ_ALIGN
    lp = length + pad
    assert pad % S5_CHUNK == 0 and d == D_MODEL
    tp = n_batch * lp
    tm = _tile(tp, 768)
    meta = jnp.broadcast_to(meta_tokens[None].astype(x.dtype), (n_batch, N_META, d))
    h = jnp.concatenate([jnp.zeros((n_batch, pad, d), x.dtype), meta, x], axis=1).reshape(tp, d)
    for i in range(DEPTH):
        kind = i % N_MIXERS
        j = i // N_MIXERS
        if kind == 0:
            h = _s5_mixer(h, lp, pad, s5_w_in[j], s5_lambda_re[j], s5_lambda_im[j], s5_log_step[j],
                          s5_b_re[j], s5_b_im[j], s5_c_re[j], s5_c_im[j], s5_d[j],
                          s5_w_glu[j], s5_b_glu[j], s5_w_out[j], ln_mix_g[i], ln_mix_b[i], tm)
        elif kind == 1:
            h = _retention_mixer(h, lp, pad, ret_w_in[j], ret_gn_g[j], ret_w_out[j],
                                 ln_mix_g[i], ln_mix_b[i], tm)
        else:
            lambda_init = 0.8 - 0.6 * math.exp(-0.3 * i)
            h = _diff_attn_mixer(h, lp, pad, diff_w_in[j], diff_lambda_q1[j], diff_lambda_k1[j],
                                 diff_lambda_q2[j], diff_lambda_k2[j], diff_subln_g[j], diff_w_out[j],
                                 lambda_init, ln_mix_g[i], ln_mix_b[i], tm)
        h = _moe_ffn(h, moe_w_router[i], moe_b_router[i], moe_w_up[i], moe_b_up[i],
                     moe_w_down[i], moe_b_down[i], ln_ffn_g[i], ln_ffn_b[i], tm, 256)
    return h.reshape(n_batch, lp, d)[:, pad + N_META:, :]
```

```python
import functools
import math

import jax
import jax.numpy as jnp
from jax import lax
from jax.experimental import pallas as pl
from jax.experimental.pallas import tpu as pltpu

F32 = jnp.float32
BF16 = jnp.bfloat16

D_MODEL = 1024
DEPTH = 4
N_META = 16
N_MIXERS = 3
S5_GROUP_CH = 16
S5_GROUPS = D_MODEL // S5_GROUP_CH
S5_STATE = 64
S5_CHUNK = 16
S5_OCT = 8
N_OCT = S5_GROUPS // S5_OCT
RET_HEADS = 4
RET_DK = D_MODEL // RET_HEADS
RET_DV = 2 * RET_DK
RET_THETA = 10000.0
DIFF_HEADS = D_MODEL // 128
DIFF_DQK = 64
DIFF_DV = 128
N_EXPERTS = 32
TOP_K = 4
D_EXPERT = D_MODEL
SWIGLU_LIMIT = 7.0
SWIGLU_ALPHA = 1.702
MOE_BLOCK = 512
LN_EPS = 1e-5
MASK_VALUE = -1e30
DEEPNORM_ALPHA = (2 * DEPTH) ** 0.25
SEQ_ALIGN = 128
LANES = 128
VMEM_LIMIT = 56 * 1024 * 1024


def _cparams(*sem):
    return pltpu.CompilerParams(dimension_semantics=sem, vmem_limit_bytes=VMEM_LIMIT)


def _tile(n, target, mult=8):
    best = None
    for t in range(mult, min(n, target) + 1, mult):
        if n % t == 0:
            best = t
    assert best is not None, (n, target, mult)
    return best


def _layer_norm_rows(v, g, b):
    mu = jnp.mean(v, axis=-1, keepdims=True)
    c = v - mu
    var = jnp.mean(c * c, axis=-1, keepdims=True)
    return c * lax.rsqrt(var + LN_EPS) * g + b


def _mm_kernel(x_ref, w_ref, o_ref):
    o_ref[...] = jnp.dot(x_ref[...].astype(BF16), w_ref[...],
                         preferred_element_type=F32).astype(o_ref.dtype)


def _matmul(x, w, out_dtype, tm, tn):
    m, k = x.shape
    n = w.shape[1]
    return pl.pallas_call(
        _mm_kernel,
        grid=(n // tn, m // tm),
        in_specs=[pl.BlockSpec((tm, k), lambda j, i: (i, 0)),
                  pl.BlockSpec((k, tn), lambda j, i: (0, j))],
        out_specs=pl.BlockSpec((tm, tn), lambda j, i: (i, j)),
        out_shape=jax.ShapeDtypeStruct((m, n), out_dtype),
        compiler_params=_cparams("parallel", "parallel"),
        name="matmul",
    )(x, w)


def _mm_res_ln_kernel(x_ref, w_ref, res_ref, g_ref, b_ref, o_ref):
    y = jnp.dot(x_ref[...].astype(BF16), w_ref[...], preferred_element_type=F32)
    o_ref[...] = _layer_norm_rows(DEEPNORM_ALPHA * res_ref[...] + y, g_ref[...], b_ref[...])


def _matmul_res_ln(x, w, res, g, b, tm):
    m, k = x.shape
    n = w.shape[1]
    row = lambda i: (i, 0)
    fix = lambda i: (0, 0)
    return pl.pallas_call(
        _mm_res_ln_kernel,
        grid=(m // tm,),
        in_specs=[pl.BlockSpec((tm, k), row), pl.BlockSpec((k, n), fix), pl.BlockSpec((tm, n), row),
                  pl.BlockSpec((1, n), fix), pl.BlockSpec((1, n), fix)],
        out_specs=pl.BlockSpec((tm, n), row),
        out_shape=jax.ShapeDtypeStruct((m, n), F32),
        compiler_params=_cparams("parallel"),
        name="matmul_res_ln",
    )(x, w, res, g.reshape(1, n), b.reshape(1, n))


def _expand_block_diag(a, row_inner, outer, inner):
    rows, k = a.shape
    cols = outer * S5_OCT * inner
    r = jnp.arange(k)[:, None]
    c = jnp.arange(cols)[None, :]
    rep = ((r // inner == c // (S5_OCT * inner)) & (r % inner == c % inner)).astype(BF16)
    out = jnp.dot(a.astype(BF16), rep, preferred_element_type=F32)
    g_row = (jnp.arange(rows)[:, None] // row_inner) % S5_OCT
    h_col = (c // inner) % S5_OCT
    return jnp.where(g_row == h_col, out, 0.0).astype(BF16)


def _s5_tables(lam_re, lam_im, log_step, b_re, b_im, c_re, c_im, d_skip):
    hp = lax.Precision.HIGHEST
    g_, n_, c_, q_, o_ = S5_GROUPS, S5_STATE, S5_GROUP_CH, S5_CHUNK, S5_OCT
    dt = jnp.exp(log_step.astype(F32))[:, None]
    lr = jnp.minimum(lam_re.astype(F32), -1e-4)
    li = lam_im.astype(F32)
    mag = jnp.exp(lr * dt)
    ab_re = mag * jnp.cos(li * dt)
    ab_im = mag * jnp.sin(li * dt)
    den = lr * lr + li * li
    f_re = ((ab_re - 1.0) * lr + ab_im * li) / den
    f_im = (ab_im * lr - (ab_re - 1.0) * li) / den
    br, bi = b_re.astype(F32), b_im.astype(F32)
    bbar_re = f_re[..., None] * br - f_im[..., None] * bi
    bbar_im = f_re[..., None] * bi + f_im[..., None] * br
    j = jnp.arange(q_ + 1, dtype=F32)[:, None, None]
    pmag = jnp.exp(j * (lr * dt)[None])
    pw_re = pmag * jnp.cos(j * (li * dt)[None])
    pw_im = pmag * jnp.sin(j * (li * dt)[None])
    cr, ci = c_re.astype(F32), c_im.astype(F32)
    w_re = pw_re[..., None] * bbar_re[None] - pw_im[..., None] * bbar_im[None]
    w_im = pw_re[..., None] * bbar_im[None] + pw_im[..., None] * bbar_re[None]
    kj = (jnp.einsum('gon,jgni->jgoi', cr, w_re[:q_], precision=hp)
          - jnp.einsum('gon,jgni->jgoi', ci, w_im[:q_], precision=hp))
    kj = kj.at[0].add(d_skip.astype(F32).reshape(g_, c_)[:, :, None] * jnp.eye(c_, dtype=F32)[None])
    s_idx = jnp.arange(q_)[:, None]
    t_idx = jnp.arange(q_)[None, :]
    lag = t_idx - s_idx
    kts = jnp.where((lag >= 0)[:, :, None, None, None], kj[jnp.clip(lag, 0, q_ - 1)], 0.0)
    kts = kts.reshape(q_, q_, N_OCT, o_, c_, c_).transpose(2, 0, 3, 5, 1, 4)
    m_tab = _expand_block_diag(kts.reshape(N_OCT * q_ * o_ * c_, q_ * c_), c_, q_, c_)
    m_tab = m_tab.reshape(N_OCT, q_, o_ * c_, q_ * o_ * c_)
    pst = jnp.stack([w_re[:q_][::-1], w_im[:q_][::-1]], axis=0)
    pst = pst.reshape(2, q_, N_OCT, o_, n_, c_).transpose(2, 1, 3, 5, 0, 4)
    p_tab = _expand_block_diag(pst.reshape(N_OCT * q_ * o_ * c_, 2 * n_), c_, 2, n_)
    p_tab = p_tab.reshape(N_OCT, q_, o_ * c_, 2 * o_ * n_)
    ar = pw_re[1:].reshape(q_, N_OCT, o_, n_)
    ai = pw_im[1:].reshape(q_, N_OCT, o_, n_)
    cr_o = cr.reshape(N_OCT, o_, c_, n_)
    ci_o = ci.reshape(N_OCT, o_, c_, n_)
    r_re = cr_o[None] * ar[:, :, :, None, :] - ci_o[None] * ai[:, :, :, None, :]
    r_im = -(cr_o[None] * ai[:, :, :, None, :] + ci_o[None] * ar[:, :, :, None, :])
    rst = jnp.stack([r_re, r_im], axis=0).transpose(2, 0, 3, 5, 1, 4)
    r_tab = _expand_block_diag(rst.reshape(N_OCT * 2 * o_ * n_, q_ * c_), n_, q_, c_)
    r_tab = r_tab.reshape(N_OCT, 2 * o_ * n_, q_ * o_ * c_)
    a16_re = pw_re[q_].reshape(N_OCT, o_ * n_)
    a16_im = pw_im[q_].reshape(N_OCT, o_ * n_)
    return m_tab, p_tab, r_tab, a16_re, a16_im


def _s5_chunk_rows(u_ref, s, tr, valid):
    us = u_ref[pl.ds(s, tr, stride=S5_CHUNK), :]
    return jnp.where(valid, us, 0.0).astype(BF16)


def _s5_valid(tr, chunks_per_batch, pad_chunks):
    chunk = pl.program_id(1) * tr + lax.broadcasted_iota(jnp.int32, (tr, 1), 0)
    return (chunk % chunks_per_batch) >= pad_chunks


def _s5_state_kernel(u_ref, p_ref, s_ref, *, tr, chunks_per_batch, pad_chunks):
    valid = _s5_valid(tr, chunks_per_batch, pad_chunks)
    acc = jnp.zeros(s_ref.shape, F32)
    for s in range(S5_CHUNK):
        acc += jnp.dot(_s5_chunk_rows(u_ref, s, tr, valid), p_ref[0, s], preferred_element_type=F32)
    s_ref[...] = acc


def _s5_scan_kernel(s_ref, ar_ref, ai_ref, x_ref, st_ref, *, tc):
    half = st_ref.shape[1] // 2

    @pl.when(pl.program_id(1) == 0)
    def _():
        st_ref[...] = jnp.zeros(st_ref.shape, F32)

    ar = ar_ref[...]
    ai = ai_ref[...]

    def body(c, carry):
        xr, xi = carry
        x_ref[c, :, :half] = xr
        x_ref[c, :, half:] = xi
        s = s_ref[c]
        return ar * xr - ai * xi + s[:, :half], ar * xi + ai * xr + s[:, half:]

    xr, xi = lax.fori_loop(0, tc, body, (st_ref[:, :half], st_ref[:, half:]))
    st_ref[:, :half] = xr
    st_ref[:, half:] = xi


def _gelu_tanh(y):
    return 0.5 * y * (1.0 + jnp.tanh(math.sqrt(2.0 / math.pi) * (y + 0.044715 * (y * y * y))))


def _s5_out_kernel(u_ref, xp_ref, m_ref, r_ref, z_ref, acc_ref, *, tr, chunks_per_batch, pad_chunks):
    valid = _s5_valid(tr, chunks_per_batch, pad_chunks)
    acc_ref[...] = jnp.dot(xp_ref[...].astype(BF16), r_ref[0], preferred_element_type=F32)
    for s in range(S5_CHUNK):
        acc_ref[:, s * LANES:] += jnp.dot(_s5_chunk_rows(u_ref, s, tr, valid), m_ref[0, s, :, s * LANES:],
                                          preferred_element_type=F32)
    for t in range(S5_CHUNK):
        z_ref[pl.ds(t, tr, stride=S5_CHUNK), :] = _gelu_tanh(acc_ref[:, t * LANES:(t + 1) * LANES])


def _s5_tail_kernel(z_ref, wg_ref, bg_ref, wo_ref, res_ref, g_ref, b_ref, o_ref):
    z = z_ref[...]
    t = jnp.dot(z.astype(BF16), wg_ref[...], preferred_element_type=F32) + bg_ref[...]
    glu = z * jax.nn.sigmoid(t)
    y = jnp.dot(glu.astype(BF16), wo_ref[...], preferred_element_type=F32)
    o_ref[...] = _layer_norm_rows(DEEPNORM_ALPHA * res_ref[...] + y, g_ref[...], b_ref[...])


def _s5_mixer(h, lp, pad, w_in, lam_re, lam_im, log_step, b_re, b_im, c_re, c_im, d_skip,
              w_glu, b_glu, w_out, ln_g, ln_b, tm):
    tp, d = h.shape
    n_batch = tp // lp
    m_tab, p_tab, r_tab, a16_re, a16_im = _s5_tables(lam_re, lam_im, log_step, b_re, b_im,
                                                     c_re, c_im, d_skip)
    u = _matmul(h, w_in.astype(BF16), F32, tm, d)
    n_chunks = tp // S5_CHUNK
    chunks_per_batch = lp // S5_CHUNK
    pad_chunks = pad // S5_CHUNK
    tr = _tile(n_chunks, 344)
    st_w = 2 * S5_OCT * S5_STATE
    kw = dict(tr=tr, chunks_per_batch=chunks_per_batch, pad_chunks=pad_chunks)
    s_all = pl.pallas_call(
        functools.partial(_s5_state_kernel, **kw),
        grid=(N_OCT, n_chunks // tr),
        in_specs=[pl.BlockSpec((tr * S5_CHUNK, LANES), lambda o, i: (i, o)),
                  pl.BlockSpec((1, S5_CHUNK, LANES, st_w), lambda o, i: (o, 0, 0, 0))],
        out_specs=pl.BlockSpec((tr, st_w), lambda o, i: (i, o)),
        out_shape=jax.ShapeDtypeStruct((n_chunks, N_OCT * st_w), F32),
        compiler_params=_cparams("parallel", "parallel"),
        name="s5_state",
    )(u, p_tab)
    tc = _tile(chunks_per_batch, 129, 1)
    nt = chunks_per_batch // tc
    x_prev = pl.pallas_call(
        functools.partial(_s5_scan_kernel, tc=tc),
        grid=(n_batch, nt),
        in_specs=[pl.BlockSpec((tc, N_OCT, st_w), lambda b, i: (b * nt + i, 0, 0)),
                  pl.BlockSpec((N_OCT, st_w // 2), lambda b, i: (0, 0)),
                  pl.BlockSpec((N_OCT, st_w // 2), lambda b, i: (0, 0))],
        out_specs=pl.BlockSpec((tc, N_OCT, st_w), lambda b, i: (b * nt + i, 0, 0)),
        out_shape=jax.ShapeDtypeStruct((n_chunks, N_OCT, st_w), F32),
        scratch_shapes=[pltpu.VMEM((N_OCT, st_w), F32)],
        compiler_params=_cparams("arbitrary", "arbitrary"),
        name="s5_scan",
    )(s_all.reshape(n_chunks, N_OCT, st_w), a16_re, a16_im)
    z = pl.pallas_call(
        functools.partial(_s5_out_kernel, **kw),
        grid=(N_OCT, n_chunks // tr),
        in_specs=[pl.BlockSpec((tr * S5_CHUNK, LANES), lambda o, i: (i, o)),
                  pl.BlockSpec((tr, st_w), lambda o, i: (i, o)),
                  pl.BlockSpec((1, S5_CHUNK, LANES, S5_CHUNK * LANES), lambda o, i: (o, 0, 0, 0)),
                  pl.BlockSpec((1, st_w, S5_CHUNK * LANES), lambda o, i: (o, 0, 0))],
        out_specs=pl.BlockSpec((tr * S5_CHUNK, LANES), lambda o, i: (i, o)),
        out_shape=jax.ShapeDtypeStruct((tp, d), F32),
        scratch_shapes=[pltpu.VMEM((tr, S5_CHUNK * LANES), F32)],
        compiler_params=_cparams("parallel", "parallel"),
        name="s5_out",
    )(u, x_prev.reshape(n_chunks, N_OCT * st_w), m_tab, r_tab)
    row = lambda i: (i, 0)
    fix = lambda i: (0, 0)
    return pl.pallas_call(
        _s5_tail_kernel,
        grid=(tp // tm,),
        in_specs=[pl.BlockSpec((tm, d), row), pl.BlockSpec((d, d), fix), pl.BlockSpec((1, d), fix),
                  pl.BlockSpec((d, d), fix), pl.BlockSpec((tm, d), row),
                  pl.BlockSpec((1, d), fix), pl.BlockSpec((1, d), fix)],
        out_specs=pl.BlockSpec((tm, d), row),
        out_shape=jax.ShapeDtypeStruct((tp, d), F32),
        compiler_params=_cparams("parallel"),
        name="s5_tail",
    )(z, w_glu.astype(BF16), b_glu.reshape(1, d).astype(F32), w_out.astype(BF16), h,
      ln_g.reshape(1, d), ln_b.reshape(1, d))


def _ret_kernel(q_ref, k_ref, v_ref, gate_ref, cos_ref, sin_ref, dm_ref, qd_ref, kd_ref, cd_ref, g_ref,
                o_ref, st_ref, *, chunk, pad):
    c = pl.program_id(2)

    @pl.when(c == 0)
    def _():
        st_ref[...] = jnp.zeros(st_ref.shape, F32)

    cos = cos_ref[...]
    sin = sin_ref[...]
    half = RET_DK // 2

    def rot(t):
        t1 = t[:, :half]
        t2 = t[:, half:]
        return jnp.concatenate([t1 * cos - t2 * sin, t1 * sin + t2 * cos], axis=-1)

    q = rot(q_ref[...])
    k = rot(k_ref[...]) * (RET_DK ** -0.5)
    pos = c * chunk + lax.broadcasted_iota(jnp.int32, (chunk, 1), 0)
    k = jnp.where(pos >= pad, k, 0.0)
    qb = q.astype(BF16)
    vb = v_ref[...].astype(BF16)
    scores = lax.dot_general(qb, k.astype(BF16), (((1,), (1,)), ((), ())),
                             preferred_element_type=F32) * dm_ref[0]
    intra = jnp.dot(scores.astype(BF16), vb, preferred_element_type=F32)
    state = st_ref[...]
    inter = jnp.dot(qb, state.astype(BF16), preferred_element_type=F32) * qd_ref[0]
    st_ref[...] = state * cd_ref[0] + lax.dot_general((k * kd_ref[0]).astype(BF16), vb,
                                                      (((0,), (0,)), ((), ())),
                                                      preferred_element_type=F32)
    o = intra + inter
    mu = jnp.mean(o, axis=-1, keepdims=True)
    oc = o - mu
    var = jnp.mean(oc * oc, axis=-1, keepdims=True)
    o = oc * lax.rsqrt(var + LN_EPS) * g_ref[...]
    gate = gate_ref[...]
    o_ref[...] = (gate * jax.nn.sigmoid(gate) * o).astype(o_ref.dtype)


def _retention_mixer(h, lp, pad, w_in, gn_g, w_out, ln_g, ln_b, tm):
    tp, d = h.shape
    n_batch = tp // lp
    chunk = SEQ_ALIGN
    ncb = lp // chunk
    qd = RET_HEADS * RET_DK
    vd = RET_HEADS * RET_DV
    proj = _matmul(h, w_in.astype(BF16), F32, tm, _tile(w_in.shape[1], 1536, LANES))
    half = RET_DK // 2
    pos = jnp.arange(lp, dtype=F32) - pad
    inv_freq = jnp.power(RET_THETA, -jnp.arange(half, dtype=F32) / half)
    ang = pos[:, None] * inv_freq[None, :]
    cos, sin = jnp.cos(ang), jnp.sin(ang)
    log_gamma = jnp.log1p(-jnp.power(2.0, -5.0 - jnp.arange(RET_HEADS, dtype=F32)))
    idx = jnp.arange(chunk, dtype=F32)
    rel = idx[:, None] - idx[None, :]
    dmask = jnp.where(rel[None] >= 0, jnp.exp(log_gamma[:, None, None] * jnp.maximum(rel, 0.0)[None]), 0.0)
    q_decay = jnp.exp(log_gamma[:, None] * (idx[None, :] + 1.0))[:, :, None]
    k_decay = jnp.exp(log_gamma[:, None] * (chunk - 1.0 - idx[None, :]))[:, :, None]
    chunk_decay = jnp.exp(log_gamma * chunk).reshape(RET_HEADS, 1, 1)
    kb = qd // RET_DK
    vb = 2 * qd // RET_DV
    gb = (2 * qd + vd) // RET_DV
    o = pl.pallas_call(
        functools.partial(_ret_kernel, chunk=chunk, pad=pad),
        grid=(n_batch, RET_HEADS, ncb),
        in_specs=[pl.BlockSpec((chunk, RET_DK), lambda b, hh, c: (b * ncb + c, hh)),
                  pl.BlockSpec((chunk, RET_DK), lambda b, hh, c: (b * ncb + c, kb + hh)),
                  pl.BlockSpec((chunk, RET_DV), lambda b, hh, c: (b * ncb + c, vb + hh)),
                  pl.BlockSpec((chunk, RET_DV), lambda b, hh, c: (b * ncb + c, gb + hh)),
                  pl.BlockSpec((chunk, half), lambda b, hh, c: (c, 0)),
                  pl.BlockSpec((chunk, half), lambda b, hh, c: (c, 0)),
                  pl.BlockSpec((1, chunk, chunk), lambda b, hh, c: (hh, 0, 0)),
                  pl.BlockSpec((1, chunk, 1), lambda b, hh, c: (hh, 0, 0)),
                  pl.BlockSpec((1, chunk, 1), lambda b, hh, c: (hh, 0, 0)),
                  pl.BlockSpec((1, 1, 1), lambda b, hh, c: (hh, 0, 0)),
                  pl.BlockSpec((1, RET_DV), lambda b, hh, c: (0, hh))],
        out_specs=pl.BlockSpec((chunk, RET_DV), lambda b, hh, c: (b * ncb + c, hh)),
        out_shape=jax.ShapeDtypeStruct((tp, vd), BF16),
        scratch_shapes=[pltpu.VMEM((RET_DK, RET_DV), F32)],
        compiler_params=_cparams("parallel", "parallel", "arbitrary"),
        name="retention",
    )(proj, proj, proj, proj, cos, sin, dmask, q_decay, k_decay, chunk_decay,
      gn_g.reshape(1, vd).astype(F32))
    return _matmul_res_ln(o, w_out.astype(BF16), h, ln_g, ln_b, tm)


ONES_ROWS = 16


def _diff_kernel(q_ref, k_ref, vt_ref, lam_ref, g_ref, o_ref, vext_ref, m_ref, acc_ref, s0_ref, s1_ref,
                 *, tq, pad, lambda_init):
    i = pl.program_id(2)
    nkb = vext_ref.shape[0]

    @pl.when(i == 0)
    def _():
        for jb in range(nkb):
            vext_ref[jb, 0:DIFF_DV, :] = vt_ref[:, jb * tq:(jb + 1) * tq]
            vext_ref[jb, DIFF_DV:, :] = jnp.ones((ONES_ROWS, tq), BF16)

    q = q_ref[...]
    lane = lax.broadcasted_iota(jnp.int32, q.shape, 1)
    zero = jnp.zeros_like(q)
    q_both = jnp.concatenate([jnp.where(lane < DIFF_DQK, q, zero), jnp.where(lane >= DIFF_DQK, q, zero)],
                             axis=0)
    m_ref[...] = jnp.full(m_ref.shape, MASK_VALUE, F32)
    acc_ref[...] = jnp.zeros(acc_ref.shape, F32)

    def scores(j):
        start = j * tq if isinstance(j, int) else pl.multiple_of(j * tq, tq)
        k = k_ref[pl.ds(start, tq), :]
        return lax.dot_general(k, q_both, (((1,), (1,)), ((), ())), preferred_element_type=F32)

    def consume(s, j, masked):
        if masked:
            kpos = j * tq + lax.broadcasted_iota(jnp.int32, (tq, tq), 0)
            qpos = i * tq + lax.broadcasted_iota(jnp.int32, (tq, tq), 1)
            allowed = (kpos <= qpos) & (kpos >= pad)
            allowed = jnp.concatenate([allowed, allowed], axis=1)
            s = jnp.where(allowed, s, MASK_VALUE)
        m_prev = m_ref[...]
        m_new = jnp.maximum(m_prev, jnp.max(s, axis=0, keepdims=True))
        alpha = jnp.exp2(m_prev - m_new)
        pe = jnp.exp2((s - m_new).astype(BF16))
        acc_ref[...] = alpha * acc_ref[...] + jnp.dot(vext_ref[j], pe, preferred_element_type=F32)
        m_ref[...] = m_new

    consume(scores(0), 0, True)

    @pl.when(i > 0)
    def _():
        n = i - 1
        s0_ref[...] = scores(1)

        def pair(jj, carry):
            a = 1 + 2 * jj
            s1_ref[...] = scores(a + 1)
            consume(s0_ref[...], a, False)
            s0_ref[...] = scores(a + 2)
            consume(s1_ref[...], a + 1, False)
            return carry

        lax.fori_loop(0, n // 2, pair, 0)

        @pl.when(n % 2 == 1)
        def _():
            s1_ref[...] = scores(i)
            consume(s0_ref[...], i - 1, False)
            consume(s1_ref[...], i, True)

        @pl.when(n % 2 == 0)
        def _():
            consume(s0_ref[...], i, True)

    lam_p = lam_ref[...]
    lam = (jnp.exp(jnp.sum(lam_p[0:1] * lam_p[1:2], axis=-1, keepdims=True))
           - jnp.exp(jnp.sum(lam_p[2:3] * lam_p[3:4], axis=-1, keepdims=True)) + lambda_init)
    acc = acc_ref[...]
    o_both = acc[0:DIFF_DV] / acc[DIFF_DV:DIFF_DV + 1]
    o = o_both[:, :tq] - lam * o_both[:, tq:]
    o = o * lax.rsqrt(jnp.mean(o * o, axis=0, keepdims=True) + LN_EPS)
    o = o * (g_ref[...] * (1.0 - lambda_init))
    o_ref[...] = o.T.astype(o_ref.dtype)


def _mm_nt_kernel(wt_ref, x_ref, o_ref):
    o_ref[...] = lax.dot_general(wt_ref[...], x_ref[...].astype(BF16), (((1,), (1,)), ((), ())),
                                 preferred_element_type=F32).astype(o_ref.dtype)


def _matmul_nt(wt, x, out_dtype, tm):
    n, k = wt.shape
    m = x.shape[0]
    return pl.pallas_call(
        _mm_nt_kernel,
        grid=(m // tm,),
        in_specs=[pl.BlockSpec((n, k), lambda i: (0, 0)), pl.BlockSpec((tm, k), lambda i: (i, 0))],
        out_specs=pl.BlockSpec((n, tm), lambda i: (0, i)),
        out_shape=jax.ShapeDtypeStruct((n, m), out_dtype),
        compiler_params=_cparams("parallel"),
        name="matmul_nt",
    )(wt, x)


def _diff_attn_mixer(h, lp, pad, w_in, lq1, lk1, lq2, lk2, subln_g, w_out, lambda_init, ln_g, ln_b, tm):
    tp, d = h.shape
    n_batch = tp // lp
    qk = DIFF_HEADS * 2 * DIFF_DQK
    col_scale = jnp.concatenate([jnp.full((qk,), DIFF_DQK ** -0.5 * math.log2(math.e), F32),
                                 jnp.ones((qk,), F32)])
    proj = _matmul(h, (w_in[:, :2 * qk] * col_scale[None, :]).astype(BF16), BF16, tm, _tile(2 * qk, 2048, LANES))
    vt = _matmul_nt(w_in[:, 2 * qk:].T.astype(BF16), h, BF16, _tile(tp, 768, LANES))
    tq = _tile(lp, 384, SEQ_ALIGN)
    nq = lp // tq
    lam_p = jnp.stack([lq1, lk1, lq2, lk2]).astype(F32)
    o = pl.pallas_call(
        functools.partial(_diff_kernel, tq=tq, pad=pad, lambda_init=lambda_init),
        grid=(n_batch, DIFF_HEADS, nq),
        in_specs=[pl.BlockSpec((tq, LANES), lambda b, hh, i: (b * nq + i, hh)),
                  pl.BlockSpec((lp, LANES), lambda b, hh, i: (b, DIFF_HEADS + hh)),
                  pl.BlockSpec((DIFF_DV, lp), lambda b, hh, i: (hh, b)),
                  pl.BlockSpec((4, DIFF_DQK), lambda b, hh, i: (0, 0)),
                  pl.BlockSpec((DIFF_DV, 1), lambda b, hh, i: (0, 0))],
        out_specs=pl.BlockSpec((tq, DIFF_DV), lambda b, hh, i: (b * nq + i, hh)),
        out_shape=jax.ShapeDtypeStruct((tp, DIFF_HEADS * DIFF_DV), BF16),
        scratch_shapes=[pltpu.VMEM((nq, DIFF_DV + ONES_ROWS, tq), BF16),
                        pltpu.VMEM((1, 2 * tq), F32),
                        pltpu.VMEM((DIFF_DV + ONES_ROWS, 2 * tq), F32),
                        pltpu.VMEM((tq, 2 * tq), F32), pltpu.VMEM((tq, 2 * tq), F32)],
        compiler_params=_cparams("parallel", "parallel", "arbitrary"),
        name="diff_attn",
    )(proj, proj, vt, lam_p, subln_g.reshape(DIFF_DV, 1).astype(F32))
    return _matmul_res_ln(o, w_out.astype(BF16), h, ln_g, ln_b, tm)


def _router_kernel(x_ref, w_ref, b_ref, idx_ref, gate_ref, rank_ref, cnt_ref, run_ref, *, tm):
    i = pl.program_id(0)

    @pl.when(i == 0)
    def _():
        run_ref[...] = jnp.zeros(run_ref.shape, F32)

    logits = jnp.dot(x_ref[...], w_ref[...], preferred_element_type=F32,
                     precision=lax.Precision.HIGHEST) + b_ref[...]
    lane = lax.broadcasted_iota(jnp.int32, logits.shape, 1).astype(F32)
    work = jnp.where(lane < N_EXPERTS, logits, -jnp.inf)
    vals, idxs = [], []
    picked = jnp.zeros(logits.shape, F32)
    for _ in range(TOP_K):
        m = jnp.max(work, axis=-1, keepdims=True)
        sel = jnp.min(jnp.where(work == m, lane, float(LANES)), axis=-1, keepdims=True)
        hit = lane == sel
        work = jnp.where(hit, -jnp.inf, work)
        picked = jnp.where(hit, 1.0, picked)
        vals.append(m)
        idxs.append(sel)
    exps = [jnp.exp(v - vals[0]) for v in vals]
    tot = exps[0] + exps[1] + exps[2] + exps[3]
    r = lax.broadcasted_iota(jnp.int32, (tm, tm), 0)
    c = lax.broadcasted_iota(jnp.int32, (tm, tm), 1)
    lower = jnp.where(c < r, 1.0, 0.0).astype(BF16)
    before = jnp.dot(lower, picked.astype(BF16), preferred_element_type=F32) + run_ref[...]
    for kk in range(TOP_K):
        idx_ref[:, kk:kk + 1] = idxs[kk].astype(jnp.int32)
        gate_ref[:, kk:kk + 1] = exps[kk] / tot
        rank_ref[:, kk:kk + 1] = jnp.sum(jnp.where(lane == idxs[kk], before, 0.0), axis=-1,
                                         keepdims=True).astype(jnp.int32)
    run_ref[...] += jnp.sum(picked, axis=0, keepdims=True)
    cnt_ref[...] = run_ref[...]


def _expert_kernel(be_ref, nu_ref, x_ref, wu_ref, bu_ref, wd_ref, bd_ref, o_ref, wu_bf_ref, wd_bf_ref):
    i = pl.program_id(0)
    live = i < nu_ref[0]

    @pl.when(live & ((i == 0) | (be_ref[i] != be_ref[jnp.maximum(i - 1, 0)])))
    def _():
        wu_bf_ref[...] = wu_ref[0, 0].astype(BF16)
        wd_bf_ref[...] = wd_ref[0, 0].astype(BF16)

    @pl.when(live)
    def _():
        hb = jnp.dot(x_ref[...], wu_bf_ref[...], preferred_element_type=F32) + bu_ref[0]
        x_glu = jnp.minimum(hb[:, :D_EXPERT], SWIGLU_LIMIT)
        x_lin = jnp.clip(hb[:, D_EXPERT:], -SWIGLU_LIMIT, SWIGLU_LIMIT)
        act = x_glu * jax.nn.sigmoid(SWIGLU_ALPHA * x_glu) * (x_lin + 1.0)
        o_ref[...] = jnp.dot(act.astype(BF16), wd_bf_ref[...], preferred_element_type=F32) + bd_ref[0]


def _combine_ln_kernel(res_ref, y0_ref, y1_ref, y2_ref, y3_ref, gate_ref, g_ref, b_ref, o_ref):
    gates = gate_ref[...]
    y = (gates[:, 0:1] * y0_ref[...] + gates[:, 1:2] * y1_ref[...]
         + gates[:, 2:3] * y2_ref[...] + gates[:, 3:4] * y3_ref[...])
    o_ref[...] = _layer_norm_rows(DEEPNORM_ALPHA * res_ref[...] + y, g_ref[...], b_ref[...])


def _moe_ffn(h, layer, w_router, b_router, w_up_all, b_up, w_down_all, b_down, ln_g, ln_b, tm, bm):
    tp, d = h.shape
    a = tp * TOP_K
    wr = jnp.zeros((d, LANES), F32).at[:, :N_EXPERTS].set(w_router.astype(F32))
    br = jnp.zeros((1, LANES), F32).at[0, :N_EXPERTS].set(b_router.astype(F32))
    row = lambda i: (i, 0)
    fix = lambda i: (0, 0)
    idx, gates, rank, cnt = pl.pallas_call(
        functools.partial(_router_kernel, tm=tm),
        grid=(tp // tm,),
        in_specs=[pl.BlockSpec((tm, d), row), pl.BlockSpec((d, LANES), fix), pl.BlockSpec((1, LANES), fix)],
        out_specs=[pl.BlockSpec((tm, TOP_K), row), pl.BlockSpec((tm, TOP_K), row),
                   pl.BlockSpec((tm, TOP_K), row), pl.BlockSpec((1, LANES), fix)],
        out_shape=[jax.ShapeDtypeStruct((tp, TOP_K), jnp.int32), jax.ShapeDtypeStruct((tp, TOP_K), F32),
                   jax.ShapeDtypeStruct((tp, TOP_K), jnp.int32), jax.ShapeDtypeStruct((1, LANES), F32)],
        scratch_shapes=[pltpu.VMEM((1, LANES), F32)],
        compiler_params=_cparams("arbitrary"),
        name="router",
    )(h, wr, br)
    counts = cnt[0, :N_EXPERTS].astype(jnp.int32)
    padded = ((counts + bm - 1) // bm) * bm
    pad_end = jnp.cumsum(padded)
    pad_start = pad_end - padded
    dest = pad_start[idx] + rank
    n_blocks = -(-a // bm) + N_EXPERTS
    n_slots = n_blocks * bm
    block_start = jnp.arange(n_blocks, dtype=jnp.int32) * bm
    block_e = jnp.minimum(jnp.sum((pad_end[None, :] <= block_start[:, None]).astype(jnp.int32), axis=1),
                          N_EXPERTS - 1)
    n_used = (pad_end[-1:] // bm).astype(jnp.int32)
    tok = jnp.repeat(jnp.arange(tp, dtype=jnp.int32), TOP_K)
    slot_tok = jnp.zeros((n_slots,), jnp.int32).at[dest.reshape(a)].set(tok)
    x_slots = h.astype(BF16)[slot_tok]
    y_slots = pl.pallas_call(
        _expert_kernel,
        grid_spec=pltpu.PrefetchScalarGridSpec(
            num_scalar_prefetch=2,
            grid=(n_blocks,),
            in_specs=[pl.BlockSpec((bm, d), lambda i, be, nu: (i, 0)),
                      pl.BlockSpec((1, 1, d, 2 * D_EXPERT), lambda i, be, nu: (layer, be[i], 0, 0)),
                      pl.BlockSpec((1, 1, 2 * D_EXPERT), lambda i, be, nu: (be[i], 0, 0)),
                      pl.BlockSpec((1, 1, D_EXPERT, d), lambda i, be, nu: (layer, be[i], 0, 0)),
                      pl.BlockSpec((1, 1, d), lambda i, be, nu: (be[i], 0, 0))],
            out_specs=pl.BlockSpec((bm, d), lambda i, be, nu: (i, 0)),
            scratch_shapes=[pltpu.VMEM((d, 2 * D_EXPERT), BF16), pltpu.VMEM((D_EXPERT, d), BF16)],
        ),
        out_shape=jax.ShapeDtypeStruct((n_slots, d), F32),
        compiler_params=_cparams("arbitrary"),
        name="experts",
    )(block_e, n_used, x_slots, w_up_all, b_up.reshape(N_EXPERTS, 1, 2 * D_EXPERT).astype(F32),
      w_down_all, b_down.reshape(N_EXPERTS, 1, d).astype(F32))
    y_k = [y_slots[dest[:, kk]] for kk in range(TOP_K)]
    return pl.pallas_call(
        _combine_ln_kernel,
        grid=(tp // tm,),
        in_specs=[pl.BlockSpec((tm, d), row)] * (1 + TOP_K)
                 + [pl.BlockSpec((tm, TOP_K), row), pl.BlockSpec((1, d), fix), pl.BlockSpec((1, d), fix)],
        out_specs=pl.BlockSpec((tm, d), row),
        out_shape=jax.ShapeDtypeStruct((tp, d), F32),
        compiler_params=_cparams("parallel"),
        name="moe_combine_ln",
    )(h, *y_k, gates, ln_g.reshape(1, d), ln_b.reshape(1, d))


def kernel(x, meta_tokens, s5_w_in, s5_lambda_re, s5_lambda_im, s5_log_step, s5_b_re, s5_b_im, s5_c_re, s5_c_im, s5_d, s5_w_glu, s5_b_glu, s5_w_out, ret_w_in, ret_gn_g, ret_w_out, diff_w_in, diff_lambda_q1, diff_lambda_k1, diff_lambda_q2, diff_lambda_k2, diff_subln_g, diff_w_out, ln_mix_g, ln_mix_b, moe_w_router, moe_b_router, moe_w_up, moe_b_up, moe_w_down, moe_b_down, ln_ffn_g, ln_ffn_b):
    n_batch, seq, d = x.shape
    length = seq + N_META
    pad = (-length) % SEQ_ALIGN
    lp = length + pad
    assert pad % S5_CHUNK == 0 and d == D_MODEL
    tp = n_batch * lp
    tm = _tile(tp, 768)
    meta = jnp.broadcast_to(meta_tokens[None].astype(x.dtype), (n_batch, N_META, d))
    h = jnp.concatenate([jnp.zeros((n_batch, pad, d), x.dtype), meta, x], axis=1).reshape(tp, d)
    for i in range(DEPTH):
        kind = i % N_MIXERS
        j = i // N_MIXERS
        if kind == 0:
            h = _s5_mixer(h, lp, pad, s5_w_in[j], s5_lambda_re[j], s5_lambda_im[j], s5_log_step[j],
                          s5_b_re[j], s5_b_im[j], s5_c_re[j], s5_c_im[j], s5_d[j],
                          s5_w_glu[j], s5_b_glu[j], s5_w_out[j], ln_mix_g[i], ln_mix_b[i], tm)
        elif kind == 1:
            h = _retention_mixer(h, lp, pad, ret_w_in[j], ret_gn_g[j], ret_w_out[j],
                                 ln_mix_g[i], ln_mix_b[i], tm)
        else:
            lambda_init = 0.8 - 0.6 * math.exp(-0.3 * i)
            h = _diff_attn_mixer(h, lp, pad, diff_w_in[j], diff_lambda_q1[j], diff_lambda_k1[j],
                                 diff_lambda_q2[j], diff_lambda_k2[j], diff_subln_g[j], diff_w_out[j],
                                 lambda_init, ln_mix_g[i], ln_mix_b[i], tm)
        h = _moe_ffn(h, i, moe_w_router[i], moe_b_router[i], moe_w_up, moe_b_up[i],
                     moe_w_down, moe_b_down[i], ln_ffn_g[i], ln_ffn_b[i], tm, MOE_BLOCK)
    return h.reshape(n_batch, lp, d)[:, pad + N_META:, :]
```

```python
import functools
import math

import jax
import jax.numpy as jnp
from jax import lax
from jax.experimental import pallas as pl
from jax.experimental.pallas import tpu as pltpu

F32 = jnp.float32
BF16 = jnp.bfloat16

D_MODEL = 1024
DEPTH = 4
N_META = 16
N_MIXERS = 3
S5_GROUP_CH = 16
S5_GROUPS = D_MODEL // S5_GROUP_CH
S5_STATE = 64
S5_CHUNK = 16
S5_OCT = 8
N_OCT = S5_GROUPS // S5_OCT
RET_HEADS = 4
RET_DK = D_MODEL // RET_HEADS
RET_DV = 2 * RET_DK
RET_THETA = 10000.0
DIFF_HEADS = D_MODEL // 128
DIFF_DQK = 64
DIFF_DV = 128
N_EXPERTS = 32
TOP_K = 4
D_EXPERT = D_MODEL
SWIGLU_LIMIT = 7.0
SWIGLU_ALPHA = 1.702
MOE_BLOCK = 512
LN_EPS = 1e-5
MASK_VALUE = -1e30
DEEPNORM_ALPHA = (2 * DEPTH) ** 0.25
SEQ_ALIGN = 128
LANES = 128
VMEM_LIMIT = 56 * 1024 * 1024


def _cparams(*sem):
    return pltpu.CompilerParams(dimension_semantics=sem, vmem_limit_bytes=VMEM_LIMIT)


def _tile(n, target, mult=8):
    best = None
    for t in range(mult, min(n, target) + 1, mult):
        if n % t == 0:
            best = t
    assert best is not None, (n, target, mult)
    return best


def _layer_norm_rows(v, g, b):
    mu = jnp.mean(v, axis=-1, keepdims=True)
    c = v - mu
    var = jnp.mean(c * c, axis=-1, keepdims=True)
    return c * lax.rsqrt(var + LN_EPS) * g + b


def _mm_kernel(x_ref, w_ref, o_ref):
    o_ref[...] = jnp.dot(x_ref[...].astype(BF16), w_ref[...],
                         preferred_element_type=F32).astype(o_ref.dtype)


def _matmul(x, w, out_dtype, tm, tn):
    m, k = x.shape
    n = w.shape[1]
    return pl.pallas_call(
        _mm_kernel,
        grid=(n // tn, m // tm),
        in_specs=[pl.BlockSpec((tm, k), lambda j, i: (i, 0)),
                  pl.BlockSpec((k, tn), lambda j, i: (0, j))],
        out_specs=pl.BlockSpec((tm, tn), lambda j, i: (i, j)),
        out_shape=jax.ShapeDtypeStruct((m, n), out_dtype),
        compiler_params=_cparams("parallel", "parallel"),
        name="matmul",
    )(x, w)


def _mm_res_ln_kernel(x_ref, w_ref, res_ref, g_ref, b_ref, o_ref):
    y = jnp.dot(x_ref[...].astype(BF16), w_ref[...], preferred_element_type=F32)
    o_ref[...] = _layer_norm_rows(DEEPNORM_ALPHA * res_ref[...] + y, g_ref[...], b_ref[...])


def _matmul_res_ln(x, w, res, g, b, tm):
    m, k = x.shape
    n = w.shape[1]
    row = lambda i: (i, 0)
    fix = lambda i: (0, 0)
    return pl.pallas_call(
        _mm_res_ln_kernel,
        grid=(m // tm,),
        in_specs=[pl.BlockSpec((tm, k), row), pl.BlockSpec((k, n), fix), pl.BlockSpec((tm, n), row),
                  pl.BlockSpec((1, n), fix), pl.BlockSpec((1, n), fix)],
        out_specs=pl.BlockSpec((tm, n), row),
        out_shape=jax.ShapeDtypeStruct((m, n), F32),
        compiler_params=_cparams("parallel"),
        name="matmul_res_ln",
    )(x, w, res, g.reshape(1, n), b.reshape(1, n))


def _expand_block_diag(a, row_inner, outer, inner):
    rows, k = a.shape
    cols = outer * S5_OCT * inner
    r = jnp.arange(k)[:, None]
    c = jnp.arange(cols)[None, :]
    rep = ((r // inner == c // (S5_OCT * inner)) & (r % inner == c % inner)).astype(BF16)
    out = jnp.dot(a.astype(BF16), rep, preferred_element_type=F32)
    g_row = (jnp.arange(rows)[:, None] // row_inner) % S5_OCT
    h_col = (c // inner) % S5_OCT
    return jnp.where(g_row == h_col, out, 0.0).astype(BF16)


def _s5_tables(lam_re, lam_im, log_step, b_re, b_im, c_re, c_im, d_skip):
    hp = lax.Precision.HIGHEST
    g_, n_, c_, q_, o_ = S5_GROUPS, S5_STATE, S5_GROUP_CH, S5_CHUNK, S5_OCT
    dt = jnp.exp(log_step.astype(F32))[:, None]
    lr = jnp.minimum(lam_re.astype(F32), -1e-4)
    li = lam_im.astype(F32)
    mag = jnp.exp(lr * dt)
    ab_re = mag * jnp.cos(li * dt)
    ab_im = mag * jnp.sin(li * dt)
    den = lr * lr + li * li
    f_re = ((ab_re - 1.0) * lr + ab_im * li) / den
    f_im = (ab_im * lr - (ab_re - 1.0) * li) / den
    br, bi = b_re.astype(F32), b_im.astype(F32)
    bbar_re = f_re[..., None] * br - f_im[..., None] * bi
    bbar_im = f_re[..., None] * bi + f_im[..., None] * br
    j = jnp.arange(q_ + 1, dtype=F32)[:, None, None]
    pmag = jnp.exp(j * (lr * dt)[None])
    pw_re = pmag * jnp.cos(j * (li * dt)[None])
    pw_im = pmag * jnp.sin(j * (li * dt)[None])
    cr, ci = c_re.astype(F32), c_im.astype(F32)
    w_re = pw_re[..., None] * bbar_re[None] - pw_im[..., None] * bbar_im[None]
    w_im = pw_re[..., None] * bbar_im[None] + pw_im[..., None] * bbar_re[None]
    kj = (jnp.einsum('gon,jgni->jgoi', cr, w_re[:q_], precision=hp)
          - jnp.einsum('gon,jgni->jgoi', ci, w_im[:q_], precision=hp))
    kj = kj.at[0].add(d_skip.astype(F32).reshape(g_, c_)[:, :, None] * jnp.eye(c_, dtype=F32)[None])
    s_idx = jnp.arange(q_)[:, None]
    t_idx = jnp.arange(q_)[None, :]
    lag = t_idx - s_idx
    kts = jnp.where((lag >= 0)[:, :, None, None, None], kj[jnp.clip(lag, 0, q_ - 1)], 0.0)
    kts = kts.reshape(q_, q_, N_OCT, o_, c_, c_).transpose(2, 0, 3, 5, 1, 4)
    m_tab = _expand_block_diag(kts.reshape(N_OCT * q_ * o_ * c_, q_ * c_), c_, q_, c_)
    m_tab = m_tab.reshape(N_OCT, q_, o_ * c_, q_ * o_ * c_)
    pst = jnp.stack([w_re[:q_][::-1], w_im[:q_][::-1]], axis=0)
    pst = pst.reshape(2, q_, N_OCT, o_, n_, c_).transpose(2, 1, 3, 5, 0, 4)
    p_tab = _expand_block_diag(pst.reshape(N_OCT * q_ * o_ * c_, 2 * n_), c_, 2, n_)
    p_tab = p_tab.reshape(N_OCT, q_, o_ * c_, 2 * o_ * n_)
    ar = pw_re[1:].reshape(q_, N_OCT, o_, n_)
    ai = pw_im[1:].reshape(q_, N_OCT, o_, n_)
    cr_o = cr.reshape(N_OCT, o_, c_, n_)
    ci_o = ci.reshape(N_OCT, o_, c_, n_)
    r_re = cr_o[None] * ar[:, :, :, None, :] - ci_o[None] * ai[:, :, :, None, :]
    r_im = -(cr_o[None] * ai[:, :, :, None, :] + ci_o[None] * ar[:, :, :, None, :])
    rst = jnp.stack([r_re, r_im], axis=0).transpose(2, 0, 3, 5, 1, 4)
    r_tab = _expand_block_diag(rst.reshape(N_OCT * 2 * o_ * n_, q_ * c_), n_, q_, c_)
    r_tab = r_tab.reshape(N_OCT, 2 * o_ * n_, q_ * o_ * c_)
    a16_re = pw_re[q_].reshape(N_OCT, o_ * n_)
    a16_im = pw_im[q_].reshape(N_OCT, o_ * n_)
    return m_tab, p_tab, r_tab, a16_re, a16_im


def _s5_chunk_rows(u_ref, s, tr, valid):
    us = u_ref[pl.ds(s, tr, stride=S5_CHUNK), :]
    return jnp.where(valid, us, 0.0).astype(BF16)


def _s5_valid(tr, chunks_per_batch, pad_chunks):
    chunk = pl.program_id(1) * tr + lax.broadcasted_iota(jnp.int32, (tr, 1), 0)
    return (chunk % chunks_per_batch) >= pad_chunks


def _s5_state_kernel(u_ref, p_ref, s_ref, *, tr, chunks_per_batch, pad_chunks):
    valid = _s5_valid(tr, chunks_per_batch, pad_chunks)
    acc = jnp.zeros(s_ref.shape, F32)
    for s in range(S5_CHUNK):
        acc += jnp.dot(_s5_chunk_rows(u_ref, s, tr, valid), p_ref[0, s], preferred_element_type=F32)
    s_ref[...] = acc


def _s5_scan_kernel(s_ref, ar_ref, ai_ref, x_ref, st_ref, *, tc):
    half = st_ref.shape[1] // 2

    @pl.when(pl.program_id(1) == 0)
    def _():
        st_ref[...] = jnp.zeros(st_ref.shape, F32)

    ar = ar_ref[...]
    ai = ai_ref[...]

    def body(c, carry):
        xr, xi = carry
        x_ref[c, :, :half] = xr
        x_ref[c, :, half:] = xi
        s = s_ref[c]
        return ar * xr - ai * xi + s[:, :half], ar * xi + ai * xr + s[:, half:]

    xr, xi = lax.fori_loop(0, tc, body, (st_ref[:, :half], st_ref[:, half:]))
    st_ref[:, :half] = xr
    st_ref[:, half:] = xi


def _gelu_tanh(y):
    return 0.5 * y * (1.0 + jnp.tanh(math.sqrt(2.0 / math.pi) * (y + 0.044715 * (y * y * y))))


def _s5_out_kernel(u_ref, xp_ref, m_ref, r_ref, z_ref, acc_ref, *, tr, chunks_per_batch, pad_chunks):
    valid = _s5_valid(tr, chunks_per_batch, pad_chunks)
    acc_ref[...] = jnp.dot(xp_ref[...].astype(BF16), r_ref[0], preferred_element_type=F32)
    for s in range(S5_CHUNK):
        acc_ref[:, s * LANES:] += jnp.dot(_s5_chunk_rows(u_ref, s, tr, valid), m_ref[0, s, :, s * LANES:],
                                          preferred_element_type=F32)
    for t in range(S5_CHUNK):
        z_ref[pl.ds(t, tr, stride=S5_CHUNK), :] = _gelu_tanh(acc_ref[:, t * LANES:(t + 1) * LANES])


def _s5_tail_kernel(z_ref, wg_ref, bg_ref, wo_ref, res_ref, g_ref, b_ref, o_ref):
    z = z_ref[...]
    t = jnp.dot(z.astype(BF16), wg_ref[...], preferred_element_type=F32) + bg_ref[...]
    glu = z * jax.nn.sigmoid(t)
    y = jnp.dot(glu.astype(BF16), wo_ref[...], preferred_element_type=F32)
    o_ref[...] = _layer_norm_rows(DEEPNORM_ALPHA * res_ref[...] + y, g_ref[...], b_ref[...])


def _s5_mixer(h, lp, pad, w_in, lam_re, lam_im, log_step, b_re, b_im, c_re, c_im, d_skip,
              w_glu, b_glu, w_out, ln_g, ln_b, tm):
    tp, d = h.shape
    n_batch = tp // lp
    m_tab, p_tab, r_tab, a16_re, a16_im = _s5_tables(lam_re, lam_im, log_step, b_re, b_im,
                                                     c_re, c_im, d_skip)
    u = _matmul(h, w_in.astype(BF16), F32, tm, d)
    n_chunks = tp // S5_CHUNK
    chunks_per_batch = lp // S5_CHUNK
    pad_chunks = pad // S5_CHUNK
    tr = _tile(n_chunks, 344)
    st_w = 2 * S5_OCT * S5_STATE
    kw = dict(tr=tr, chunks_per_batch=chunks_per_batch, pad_chunks=pad_chunks)
    s_all = pl.pallas_call(
        functools.partial(_s5_state_kernel, **kw),
        grid=(N_OCT, n_chunks // tr),
        in_specs=[pl.BlockSpec((tr * S5_CHUNK, LANES), lambda o, i: (i, o)),
                  pl.BlockSpec((1, S5_CHUNK, LANES, st_w), lambda o, i: (o, 0, 0, 0))],
        out_specs=pl.BlockSpec((tr, st_w), lambda o, i: (i, o)),
        out_shape=jax.ShapeDtypeStruct((n_chunks, N_OCT * st_w), F32),
        compiler_params=_cparams("parallel", "parallel"),
        name="s5_state",
    )(u, p_tab)
    tc = _tile(chunks_per_batch, 129, 1)
    nt = chunks_per_batch // tc
    x_prev = pl.pallas_call(
        functools.partial(_s5_scan_kernel, tc=tc),
        grid=(n_batch, nt),
        in_specs=[pl.BlockSpec((tc, N_OCT, st_w), lambda b, i: (b * nt + i, 0, 0)),
                  pl.BlockSpec((N_OCT, st_w // 2), lambda b, i: (0, 0)),
                  pl.BlockSpec((N_OCT, st_w // 2), lambda b, i: (0, 0))],
        out_specs=pl.BlockSpec((tc, N_OCT, st_w), lambda b, i: (b * nt + i, 0, 0)),
        out_shape=jax.ShapeDtypeStruct((n_chunks, N_OCT, st_w), F32),
        scratch_shapes=[pltpu.VMEM((N_OCT, st_w), F32)],
        compiler_params=_cparams("arbitrary", "arbitrary"),
        name="s5_scan",
    )(s_all.reshape(n_chunks, N_OCT, st_w), a16_re, a16_im)
    z = pl.pallas_call(
        functools.partial(_s5_out_kernel, **kw),
        grid=(N_OCT, n_chunks // tr),
        in_specs=[pl.BlockSpec((tr * S5_CHUNK, LANES), lambda o, i: (i, o)),
                  pl.BlockSpec((tr, st_w), lambda o, i: (i, o)),
                  pl.BlockSpec((1, S5_CHUNK, LANES, S5_CHUNK * LANES), lambda o, i: (o, 0, 0, 0)),
                  pl.BlockSpec((1, st_w, S5_CHUNK * LANES), lambda o, i: (o, 0, 0))],
        out_specs=pl.BlockSpec((tr * S5_CHUNK, LANES), lambda o, i: (i, o)),
        out_shape=jax.ShapeDtypeStruct((tp, d), F32),
        scratch_shapes=[pltpu.VMEM((tr, S5_CHUNK * LANES), F32)],
        compiler_params=_cparams("parallel", "parallel"),
        name="s5_out",
    )(u, x_prev.reshape(n_chunks, N_OCT * st_w), m_tab, r_tab)
    row = lambda i: (i, 0)
    fix = lambda i: (0, 0)
    return pl.pallas_call(
        _s5_tail_kernel,
        grid=(tp // tm,),
        in_specs=[pl.BlockSpec((tm, d), row), pl.BlockSpec((d, d), fix), pl.BlockSpec((1, d), fix),
                  pl.BlockSpec((d, d), fix), pl.BlockSpec((tm, d), row),
                  pl.BlockSpec((1, d), fix), pl.BlockSpec((1, d), fix)],
        out_specs=pl.BlockSpec((tm, d), row),
        out_shape=jax.ShapeDtypeStruct((tp, d), F32),
        compiler_params=_cparams("parallel"),
        name="s5_tail",
    )(z, w_glu.astype(BF16), b_glu.reshape(1, d).astype(F32), w_out.astype(BF16), h,
      ln_g.reshape(1, d), ln_b.reshape(1, d))


def _ret_kernel(q_ref, k_ref, v_ref, gate_ref, cos_ref, sin_ref, dm_ref, qd_ref, kd_ref, cd_ref, g_ref,
                o_ref, st_ref, *, chunk, pad):
    c = pl.program_id(2)

    @pl.when(c == 0)
    def _():
        st_ref[...] = jnp.zeros(st_ref.shape, F32)

    cos = cos_ref[...]
    sin = sin_ref[...]
    half = RET_DK // 2

    def rot(t):
        t1 = t[:, :half]
        t2 = t[:, half:]
        return jnp.concatenate([t1 * cos - t2 * sin, t1 * sin + t2 * cos], axis=-1)

    q = rot(q_ref[...])
    k = rot(k_ref[...]) * (RET_DK ** -0.5)
    pos = c * chunk + lax.broadcasted_iota(jnp.int32, (chunk, 1), 0)
    k = jnp.where(pos >= pad, k, 0.0)
    qb = q.astype(BF16)
    vb = v_ref[...].astype(BF16)
    scores = lax.dot_general(qb, k.astype(BF16), (((1,), (1,)), ((), ())),
                             preferred_element_type=F32) * dm_ref[0]
    intra = jnp.dot(scores.astype(BF16), vb, preferred_element_type=F32)
    state = st_ref[...]
    inter = jnp.dot(qb, state.astype(BF16), preferred_element_type=F32) * qd_ref[0]
    st_ref[...] = state * cd_ref[0] + lax.dot_general((k * kd_ref[0]).astype(BF16), vb,
                                                      (((0,), (0,)), ((), ())),
                                                      preferred_element_type=F32)
    o = intra + inter
    mu = jnp.mean(o, axis=-1, keepdims=True)
    oc = o - mu
    var = jnp.mean(oc * oc, axis=-1, keepdims=True)
    o = oc * lax.rsqrt(var + LN_EPS) * g_ref[...]
    gate = gate_ref[...]
    o_ref[...] = (gate * jax.nn.sigmoid(gate) * o).astype(o_ref.dtype)


def _retention_mixer(h, lp, pad, w_in, gn_g, w_out, ln_g, ln_b, tm):
    tp, d = h.shape
    n_batch = tp // lp
    chunk = SEQ_ALIGN
    ncb = lp // chunk
    qd = RET_HEADS * RET_DK
    vd = RET_HEADS * RET_DV
    proj = _matmul(h, w_in.astype(BF16), F32, tm, _tile(w_in.shape[1], 1536, LANES))
    half = RET_DK // 2
    pos = jnp.arange(lp, dtype=F32) - pad
    inv_freq = jnp.power(RET_THETA, -jnp.arange(half, dtype=F32) / half)
    ang = pos[:, None] * inv_freq[None, :]
    cos, sin = jnp.cos(ang), jnp.sin(ang)
    log_gamma = jnp.log1p(-jnp.power(2.0, -5.0 - jnp.arange(RET_HEADS, dtype=F32)))
    idx = jnp.arange(chunk, dtype=F32)
    rel = idx[:, None] - idx[None, :]
    dmask = jnp.where(rel[None] >= 0, jnp.exp(log_gamma[:, None, None] * jnp.maximum(rel, 0.0)[None]), 0.0)
    q_decay = jnp.exp(log_gamma[:, None] * (idx[None, :] + 1.0))[:, :, None]
    k_decay = jnp.exp(log_gamma[:, None] * (chunk - 1.0 - idx[None, :]))[:, :, None]
    chunk_decay = jnp.exp(log_gamma * chunk).reshape(RET_HEADS, 1, 1)
    kb = qd // RET_DK
    vb = 2 * qd // RET_DV
    gb = (2 * qd + vd) // RET_DV
    o = pl.pallas_call(
        functools.partial(_ret_kernel, chunk=chunk, pad=pad),
        grid=(n_batch, RET_HEADS, ncb),
        in_specs=[pl.BlockSpec((chunk, RET_DK), lambda b, hh, c: (b * ncb + c, hh)),
                  pl.BlockSpec((chunk, RET_DK), lambda b, hh, c: (b * ncb + c, kb + hh)),
                  pl.BlockSpec((chunk, RET_DV), lambda b, hh, c: (b * ncb + c, vb + hh)),
                  pl.BlockSpec((chunk, RET_DV), lambda b, hh, c: (b * ncb + c, gb + hh)),
                  pl.BlockSpec((chunk, half), lambda b, hh, c: (c, 0)),
                  pl.BlockSpec((chunk, half), lambda b, hh, c: (c, 0)),
                  pl.BlockSpec((1, chunk, chunk), lambda b, hh, c: (hh, 0, 0)),
                  pl.BlockSpec((1, chunk, 1), lambda b, hh, c: (hh, 0, 0)),
                  pl.BlockSpec((1, chunk, 1), lambda b, hh, c: (hh, 0, 0)),
                  pl.BlockSpec((1, 1, 1), lambda b, hh, c: (hh, 0, 0)),
                  pl.BlockSpec((1, RET_DV), lambda b, hh, c: (0, hh))],
        out_specs=pl.BlockSpec((chunk, RET_DV), lambda b, hh, c: (b * ncb + c, hh)),
        out_shape=jax.ShapeDtypeStruct((tp, vd), BF16),
        scratch_shapes=[pltpu.VMEM((RET_DK, RET_DV), F32)],
        compiler_params=_cparams("parallel", "parallel", "arbitrary"),
        name="retention",
    )(proj, proj, proj, proj, cos, sin, dmask, q_decay, k_decay, chunk_decay,
      gn_g.reshape(1, vd).astype(F32))
    return _matmul_res_ln(o, w_out.astype(BF16), h, ln_g, ln_b, tm)


ONES_ROWS = 16


def _diff_kernel(q_ref, k_ref, vt_ref, lam_ref, g_ref, o_ref, vext_ref, m_ref, acc_ref, s0_ref, s1_ref,
                 *, tq, pad, lambda_init):
    i = pl.program_id(2)
    nkb = vext_ref.shape[0]

    @pl.when(i == 0)
    def _():
        for jb in range(nkb):
            vext_ref[jb, 0:DIFF_DV, :] = vt_ref[:, jb * tq:(jb + 1) * tq]
            vext_ref[jb, DIFF_DV:, :] = jnp.ones((ONES_ROWS, tq), BF16)

    q = q_ref[...]
    lane = lax.broadcasted_iota(jnp.int32, q.shape, 1)
    zero = jnp.zeros_like(q)
    q_both = jnp.concatenate([jnp.where(lane < DIFF_DQK, q, zero), jnp.where(lane >= DIFF_DQK, q, zero)],
                             axis=0)
    m_ref[...] = jnp.full(m_ref.shape, MASK_VALUE, F32)
    acc_ref[...] = jnp.zeros(acc_ref.shape, F32)

    def scores(j):
        start = j * tq if isinstance(j, int) else pl.multiple_of(j * tq, tq)
        k = k_ref[pl.ds(start, tq), :]
        return lax.dot_general(k, q_both, (((1,), (1,)), ((), ())), preferred_element_type=F32)

    def consume(s, j, masked):
        if masked:
            kpos = j * tq + lax.broadcasted_iota(jnp.int32, (tq, tq), 0)
            qpos = i * tq + lax.broadcasted_iota(jnp.int32, (tq, tq), 1)
            allowed = (kpos <= qpos) & (kpos >= pad)
            allowed = jnp.concatenate([allowed, allowed], axis=1)
            s = jnp.where(allowed, s, MASK_VALUE)
        m_prev = m_ref[...]
        m_new = jnp.maximum(m_prev, jnp.max(s, axis=0, keepdims=True))
        alpha = jnp.exp2(m_prev - m_new)
        pe = jnp.exp2((s - m_new).astype(BF16))
        acc_ref[...] = alpha * acc_ref[...] + jnp.dot(vext_ref[j], pe, preferred_element_type=F32)
        m_ref[...] = m_new

    consume(scores(0), 0, True)

    @pl.when(i > 0)
    def _():
        n = i - 1
        s0_ref[...] = scores(1)

        def pair(jj, carry):
            a = 1 + 2 * jj
            s1_ref[...] = scores(a + 1)
            consume(s0_ref[...], a, False)
            s0_ref[...] = scores(a + 2)
            consume(s1_ref[...], a + 1, False)
            return carry

        lax.fori_loop(0, n // 2, pair, 0)

        @pl.when(n % 2 == 1)
        def _():
            s1_ref[...] = scores(i)
            consume(s0_ref[...], i - 1, False)
            consume(s1_ref[...], i, True)

        @pl.when(n % 2 == 0)
        def _():
            consume(s0_ref[...], i, True)

    lam_p = lam_ref[...]
    lam = (jnp.exp(jnp.sum(lam_p[0:1] * lam_p[1:2], axis=-1, keepdims=True))
           - jnp.exp(jnp.sum(lam_p[2:3] * lam_p[3:4], axis=-1, keepdims=True)) + lambda_init)
    acc = acc_ref[...]
    o_both = acc[0:DIFF_DV] / acc[DIFF_DV:DIFF_DV + 1]
    o = o_both[:, :tq] - lam * o_both[:, tq:]
    o = o * lax.rsqrt(jnp.mean(o * o, axis=0, keepdims=True) + LN_EPS)
    o = o * (g_ref[...] * (1.0 - lambda_init))
    o_ref[...] = o.T.astype(o_ref.dtype)


def _mm_nt_kernel(wt_ref, x_ref, o_ref):
    o_ref[...] = lax.dot_general(wt_ref[...], x_ref[...].astype(BF16), (((1,), (1,)), ((), ())),
                                 preferred_element_type=F32).astype(o_ref.dtype)


def _matmul_nt(wt, x, out_dtype, tm):
    n, k = wt.shape
    m = x.shape[0]
    return pl.pallas_call(
        _mm_nt_kernel,
        grid=(m // tm,),
        in_specs=[pl.BlockSpec((n, k), lambda i: (0, 0)), pl.BlockSpec((tm, k), lambda i: (i, 0))],
        out_specs=pl.BlockSpec((n, tm), lambda i: (0, i)),
        out_shape=jax.ShapeDtypeStruct((n, m), out_dtype),
        compiler_params=_cparams("parallel"),
        name="matmul_nt",
    )(wt, x)


def _diff_attn_mixer(h, lp, pad, w_in, lq1, lk1, lq2, lk2, subln_g, w_out, lambda_init, ln_g, ln_b, tm):
    tp, d = h.shape
    n_batch = tp // lp
    qk = DIFF_HEADS * 2 * DIFF_DQK
    col_scale = jnp.concatenate([jnp.full((qk,), DIFF_DQK ** -0.5 * math.log2(math.e), F32),
                                 jnp.ones((qk,), F32)])
    proj = _matmul(h, (w_in[:, :2 * qk] * col_scale[None, :]).astype(BF16), BF16, tm, _tile(2 * qk, 2048, LANES))
    vt = _matmul_nt(w_in[:, 2 * qk:].T.astype(BF16), h, BF16, _tile(tp, 768, LANES))
    tq = _tile(lp, 384, SEQ_ALIGN)
    nq = lp // tq
    lam_p = jnp.stack([lq1, lk1, lq2, lk2]).astype(F32)
    o = pl.pallas_call(
        functools.partial(_diff_kernel, tq=tq, pad=pad, lambda_init=lambda_init),
        grid=(n_batch, DIFF_HEADS, nq),
        in_specs=[pl.BlockSpec((tq, LANES), lambda b, hh, i: (b * nq + i, hh)),
                  pl.BlockSpec((lp, LANES), lambda b, hh, i: (b, DIFF_HEADS + hh)),
                  pl.BlockSpec((DIFF_DV, lp), lambda b, hh, i: (hh, b)),
                  pl.BlockSpec((4, DIFF_DQK), lambda b, hh, i: (0, 0)),
                  pl.BlockSpec((DIFF_DV, 1), lambda b, hh, i: (0, 0))],
        out_specs=pl.BlockSpec((tq, DIFF_DV), lambda b, hh, i: (b * nq + i, hh)),
        out_shape=jax.ShapeDtypeStruct((tp, DIFF_HEADS * DIFF_DV), BF16),
        scratch_shapes=[pltpu.VMEM((nq, DIFF_DV + ONES_ROWS, tq), BF16),
                        pltpu.VMEM((1, 2 * tq), F32),
                        pltpu.VMEM((DIFF_DV + ONES_ROWS, 2 * tq), F32),
                        pltpu.VMEM((tq, 2 * tq), F32), pltpu.VMEM((tq, 2 * tq), F32)],
        compiler_params=_cparams("parallel", "parallel", "arbitrary"),
        name="diff_attn",
    )(proj, proj, vt, lam_p, subln_g.reshape(DIFF_DV, 1).astype(F32))
    return _matmul_res_ln(o, w_out.astype(BF16), h, ln_g, ln_b, tm)


SORT_ALIGN = 16
SORT_CHUNK = 512


def _router_kernel(x_ref, w_ref, b_ref, idx_ref, gate_ref, rank_ref, cnt_ref, *, tm):
    logits = jnp.dot(x_ref[...], w_ref[...], preferred_element_type=F32,
                     precision=lax.Precision.HIGHEST) + b_ref[...]
    lane = lax.broadcasted_iota(jnp.int32, logits.shape, 1).astype(F32)
    work = jnp.where(lane < N_EXPERTS, logits, -jnp.inf)
    vals, idxs = [], []
    picked = jnp.zeros(logits.shape, F32)
    for _ in range(TOP_K):
        m = jnp.max(work, axis=-1, keepdims=True)
        sel = jnp.min(jnp.where(work == m, lane, float(LANES)), axis=-1, keepdims=True)
        hit = lane == sel
        work = jnp.where(hit, -jnp.inf, work)
        picked = jnp.where(hit, 1.0, picked)
        vals.append(m)
        idxs.append(sel)
    exps = [jnp.exp(v - vals[0]) for v in vals]
    tot = exps[0] + exps[1] + exps[2] + exps[3]
    r = lax.broadcasted_iota(jnp.int32, (tm, tm), 0)
    c = lax.broadcasted_iota(jnp.int32, (tm, tm), 1)
    lower = jnp.where(c < r, 1.0, 0.0).astype(BF16)
    before = jnp.dot(lower, picked.astype(BF16), preferred_element_type=F32)
    for kk in range(TOP_K):
        idx_ref[:, kk:kk + 1] = idxs[kk].astype(jnp.int32)
        gate_ref[:, kk:kk + 1] = exps[kk] / tot
        rank_ref[:, kk:kk + 1] = jnp.sum(jnp.where(lane == idxs[kk], before, 0.0), axis=-1,
                                         keepdims=True).astype(jnp.int32)
    cnt_ref[0] = jnp.sum(picked, axis=0, keepdims=True)


def _segment_copies(tab_ref, tile, local_ref, slots_ref, sem, to_slots):
    n_seg = tab_ref.shape[0] // 3

    def copy(e, c):
        lo = pl.multiple_of(tab_ref[tile * N_EXPERTS + e] + c * SORT_ALIGN, SORT_ALIGN)
        go = pl.multiple_of(tab_ref[n_seg + tile * N_EXPERTS + e] + c * SORT_ALIGN, SORT_ALIGN)
        local = local_ref.at[pl.ds(lo, SORT_ALIGN), :]
        slots = slots_ref.at[pl.ds(go, SORT_ALIGN), :]
        return pltpu.make_async_copy(local, slots, sem) if to_slots else pltpu.make_async_copy(slots, local, sem)

    for wait in (False, True):
        def per_expert(e, carry):
            def per_chunk(c, carry2):
                if wait:
                    copy(e, c).wait()
                else:
                    copy(e, c).start()
                return carry2
            return lax.fori_loop(0, tab_ref[2 * n_seg + tile * N_EXPERTS + e], per_chunk, carry)
        lax.fori_loop(0, N_EXPERTS, per_expert, 0)


def _dispatch_kernel(tab_ref, x_ref, lrow_ref, slots_in_ref, slots_ref, sorted_ref, sem, *, tm, rs):
    del slots_in_ref
    xb = x_ref[...].astype(BF16)
    lr = lrow_ref[...]
    for rc in range(rs // SORT_CHUNK):
        rows = rc * SORT_CHUNK + lax.broadcasted_iota(jnp.int32, (SORT_CHUNK, tm), 0)
        hit = (rows == lr[0:1]) | (rows == lr[1:2]) | (rows == lr[2:3]) | (rows == lr[3:4])
        perm = jnp.where(hit, 1.0, 0.0).astype(BF16)
        sorted_ref[rc * SORT_CHUNK:(rc + 1) * SORT_CHUNK, :] = jnp.dot(
            perm, xb, preferred_element_type=F32).astype(BF16)
    _segment_copies(tab_ref, pl.program_id(0), sorted_ref, slots_ref, sem, True)


def _combine_kernel(tab_ref, res_ref, lrow_ref, gate_ref, g_ref, b_ref, y_ref, o_ref, ysort_ref, sem,
                    *, tm, rs):
    @pl.when(pl.program_id(0) == 0)
    def _():
        ysort_ref[...] = jnp.zeros(ysort_ref.shape, BF16)

    _segment_copies(tab_ref, pl.program_id(0), ysort_ref, y_ref, sem, False)
    lr = lrow_ref[...]
    gt = gate_ref[...]
    y = jnp.zeros(res_ref.shape, F32)
    for rc in range(rs // SORT_CHUNK):
        cols = rc * SORT_CHUNK + lax.broadcasted_iota(jnp.int32, (tm, SORT_CHUNK), 1)
        w = jnp.zeros((tm, SORT_CHUNK), F32)
        for kk in range(TOP_K):
            w = jnp.where(cols == lr[:, kk:kk + 1], gt[:, kk:kk + 1], w)
        y += jnp.dot(w.astype(BF16), ysort_ref[rc * SORT_CHUNK:(rc + 1) * SORT_CHUNK, :],
                     preferred_element_type=F32)
    o_ref[...] = _layer_norm_rows(DEEPNORM_ALPHA * res_ref[...] + y, g_ref[...], b_ref[...])


def _expert_kernel(be_ref, nu_ref, x_ref, wu_ref, bu_ref, wd_ref, bd_ref, o_ref, wu_bf_ref, wd_bf_ref):
    i = pl.program_id(0)
    live = i < nu_ref[0]

    @pl.when(live & ((i == 0) | (be_ref[i] != be_ref[jnp.maximum(i - 1, 0)])))
    def _():
        wu_bf_ref[...] = wu_ref[0, 0].astype(BF16)
        wd_bf_ref[...] = wd_ref[0, 0].astype(BF16)

    @pl.when(live)
    def _():
        hb = jnp.dot(x_ref[...], wu_bf_ref[...], preferred_element_type=F32) + bu_ref[0]
        x_glu = jnp.minimum(hb[:, :D_EXPERT], SWIGLU_LIMIT)
        x_lin = jnp.clip(hb[:, D_EXPERT:], -SWIGLU_LIMIT, SWIGLU_LIMIT)
        act = x_glu * jax.nn.sigmoid(SWIGLU_ALPHA * x_glu) * (x_lin + 1.0)
        y = jnp.dot(act.astype(BF16), wd_bf_ref[...], preferred_element_type=F32) + bd_ref[0]
        o_ref[...] = y.astype(o_ref.dtype)

    @pl.when(jnp.logical_not(live))
    def _():
        o_ref[...] = jnp.zeros(o_ref.shape, o_ref.dtype)


def _moe_ffn(h, layer, w_router, b_router, w_up_all, b_up, w_down_all, b_down, ln_g, ln_b, tm, bm):
    tp, d = h.shape
    a = tp * TOP_K
    wr = jnp.zeros((d, LANES), F32).at[:, :N_EXPERTS].set(w_router.astype(F32))
    br = jnp.zeros((1, LANES), F32).at[0, :N_EXPERTS].set(b_router.astype(F32))
    row = lambda i: (i, 0)
    fix = lambda i: (0, 0)
    nt = tp // tm
    idx, gates, rank, cnt = pl.pallas_call(
        functools.partial(_router_kernel, tm=tm),
        grid=(nt,),
        in_specs=[pl.BlockSpec((tm, d), row), pl.BlockSpec((d, LANES), fix), pl.BlockSpec((1, LANES), fix)],
        out_specs=[pl.BlockSpec((tm, TOP_K), row), pl.BlockSpec((tm, TOP_K), row),
                   pl.BlockSpec((tm, TOP_K), row), pl.BlockSpec((1, 1, LANES), lambda i: (i, 0, 0))],
        out_shape=[jax.ShapeDtypeStruct((tp, TOP_K), jnp.int32), jax.ShapeDtypeStruct((tp, TOP_K), F32),
                   jax.ShapeDtypeStruct((tp, TOP_K), jnp.int32), jax.ShapeDtypeStruct((nt, 1, LANES), F32)],
        compiler_params=_cparams("parallel"),
        name="router",
    )(h, wr, br)
    counts = cnt[:, 0, :N_EXPERTS].astype(jnp.int32)
    seg = ((counts + SORT_ALIGN - 1) // SORT_ALIGN) * SORT_ALIGN
    tile_off = jnp.cumsum(seg, axis=0) - seg
    region = ((jnp.sum(seg, axis=0) + bm - 1) // bm) * bm
    region_end = jnp.cumsum(region)
    slot_off = (region_end - region)[None, :] + tile_off
    local_off = jnp.cumsum(seg, axis=1) - seg
    table = jnp.concatenate([local_off.reshape(-1), slot_off.reshape(-1),
                             (seg // SORT_ALIGN).reshape(-1)]).astype(jnp.int32)
    tile_id = jnp.arange(tp, dtype=jnp.int32)[:, None] // tm
    lrow = local_off[tile_id, idx] + rank
    lrow_t = jnp.concatenate([lrow.T, jnp.full((8 - TOP_K, tp), -1, jnp.int32)], axis=0)
    rs = -(-(TOP_K * tm + N_EXPERTS * (SORT_ALIGN - 1)) // SORT_CHUNK) * SORT_CHUNK
    n_blocks = -(-(a + nt * N_EXPERTS * (SORT_ALIGN - 1)) // bm) + N_EXPERTS
    n_slots = n_blocks * bm
    block_start = jnp.arange(n_blocks, dtype=jnp.int32) * bm
    block_e = jnp.minimum(jnp.sum((region_end[None, :] <= block_start[:, None]).astype(jnp.int32), axis=1),
                          N_EXPERTS - 1)
    n_used = (region_end[-1:] // bm).astype(jnp.int32)
    x_slots = pl.pallas_call(
        functools.partial(_dispatch_kernel, tm=tm, rs=rs),
        grid_spec=pltpu.PrefetchScalarGridSpec(
            num_scalar_prefetch=1,
            grid=(nt,),
            in_specs=[pl.BlockSpec((tm, d), lambda i, tab: (i, 0)),
                      pl.BlockSpec((8, tm), lambda i, tab: (0, i)),
                      pl.BlockSpec(memory_space=pl.ANY)],
            out_specs=pl.BlockSpec(memory_space=pl.ANY),
            scratch_shapes=[pltpu.VMEM((rs, d), BF16), pltpu.SemaphoreType.DMA(())],
        ),
        out_shape=jax.ShapeDtypeStruct((n_slots, d), BF16),
        input_output_aliases={3: 0},
        compiler_params=_cparams("arbitrary"),
        name="moe_dispatch",
    )(table, h, lrow_t, jnp.zeros((n_slots, d), BF16))
    y_slots = pl.pallas_call(
        _expert_kernel,
        grid_spec=pltpu.PrefetchScalarGridSpec(
            num_scalar_prefetch=2,
            grid=(n_blocks,),
            in_specs=[pl.BlockSpec((bm, d), lambda i, be, nu: (i, 0)),
                      pl.BlockSpec((1, 1, d, 2 * D_EXPERT), lambda i, be, nu: (layer, be[i], 0, 0)),
                      pl.BlockSpec((1, 1, 2 * D_EXPERT), lambda i, be, nu: (be[i], 0, 0)),
                      pl.BlockSpec((1, 1, D_EXPERT, d), lambda i, be, nu: (layer, be[i], 0, 0)),
                      pl.BlockSpec((1, 1, d), lambda i, be, nu: (be[i], 0, 0))],
            out_specs=pl.BlockSpec((bm, d), lambda i, be, nu: (i, 0)),
            scratch_shapes=[pltpu.VMEM((d, 2 * D_EXPERT), BF16), pltpu.VMEM((D_EXPERT, d), BF16)],
        ),
        out_shape=jax.ShapeDtypeStruct((n_slots, d), BF16),
        compiler_params=_cparams("arbitrary"),
        name="experts",
    )(block_e, n_used, x_slots, w_up_all, b_up.reshape(N_EXPERTS, 1, 2 * D_EXPERT).astype(F32),
      w_down_all, b_down.reshape(N_EXPERTS, 1, d).astype(F32))
    return pl.pallas_call(
        functools.partial(_combine_kernel, tm=tm, rs=rs),
        grid_spec=pltpu.PrefetchScalarGridSpec(
            num_scalar_prefetch=1,
            grid=(nt,),
            in_specs=[pl.BlockSpec((tm, d), lambda i, tab: (i, 0)),
                      pl.BlockSpec((tm, TOP_K), lambda i, tab: (i, 0)),
                      pl.BlockSpec((tm, TOP_K), lambda i, tab: (i, 0)),
                      pl.BlockSpec((1, d), lambda i, tab: (0, 0)),
                      pl.BlockSpec((1, d), lambda i, tab: (0, 0)),
                      pl.BlockSpec(memory_space=pl.ANY)],
            out_specs=pl.BlockSpec((tm, d), lambda i, tab: (i, 0)),
            scratch_shapes=[pltpu.VMEM((rs, d), BF16), pltpu.SemaphoreType.DMA(())],
        ),
        out_shape=jax.ShapeDtypeStruct((tp, d), F32),
        compiler_params=_cparams("arbitrary"),
        name="moe_combine_ln",
    )(table, h, lrow, gates, ln_g.reshape(1, d), ln_b.reshape(1, d), y_slots)


def kernel(x, meta_tokens, s5_w_in, s5_lambda_re, s5_lambda_im, s5_log_step, s5_b_re, s5_b_im, s5_c_re, s5_c_im, s5_d, s5_w_glu, s5_b_glu, s5_w_out, ret_w_in, ret_gn_g, ret_w_out, diff_w_in, diff_lambda_q1, diff_lambda_k1, diff_lambda_q2, diff_lambda_k2, diff_subln_g, diff_w_out, ln_mix_g, ln_mix_b, moe_w_router, moe_b_router, moe_w_up, moe_b_up, moe_w_down, moe_b_down, ln_ffn_g, ln_ffn_b):
    n_batch, seq, d = x.shape
    length = seq + N_META
    pad = (-length) % SEQ_ALIGN
    lp = length + pad
    assert pad % S5_CHUNK == 0 and d == D_MODEL
    tp = n_batch * lp
    tm = _tile(tp, 768)
    meta = jnp.broadcast_to(meta_tokens[None].astype(x.dtype), (n_batch, N_META, d))
    h = jnp.concatenate([jnp.zeros((n_batch, pad, d), x.dtype), meta, x], axis=1).reshape(tp, d)
    for i in range(DEPTH):
        kind = i % N_MIXERS
        j = i // N_MIXERS
        if kind == 0:
            h = _s5_mixer(h, lp, pad, s5_w_in[j], s5_lambda_re[j], s5_lambda_im[j], s5_log_step[j],
                          s5_b_re[j], s5_b_im[j], s5_c_re[j], s5_c_im[j], s5_d[j],
                          s5_w_glu[j], s5_b_glu[j], s5_w_out[j], ln_mix_g[i], ln_mix_b[i], tm)
        elif kind == 1:
            h = _retention_mixer(h, lp, pad, ret_w_in[j], ret_gn_g[j], ret_w_out[j],
                                 ln_mix_g[i], ln_mix_b[i], tm)
        else:
            lambda_init = 0.8 - 0.6 * math.exp(-0.3 * i)
            h = _diff_attn_mixer(h, lp, pad, diff_w_in[j], diff_lambda_q1[j], diff_lambda_k1[j],
                                 diff_lambda_q2[j], diff_lambda_k2[j], diff_subln_g[j], diff_w_out[j],
                                 lambda_init, ln_mix_g[i], ln_mix_b[i], tm)
        h = _moe_ffn(h, i, moe_w_router[i], moe_b_router[i], moe_w_up, moe_b_up[i],
                     moe_w_down, moe_b_down[i], ln_ffn_g[i], ln_ffn_b[i], tm, MOE_BLOCK)
    return h.reshape(n_batch, lp, d)[:, pad + N_META:, :]
```

```python
import functools
import math

import jax
import jax.numpy as jnp
from jax import lax
from jax.experimental import pallas as pl
from jax.experimental.pallas import tpu as pltpu

F32 = jnp.float32
BF16 = jnp.bfloat16

D_MODEL = 1024
DEPTH = 4
N_META = 16
N_MIXERS = 3
S5_GROUP_CH = 16
S5_GROUPS = D_MODEL // S5_GROUP_CH
S5_STATE = 64
S5_CHUNK = 16
S5_OCT = 8
N_OCT = S5_GROUPS // S5_OCT
RET_HEADS = 4
RET_DK = D_MODEL // RET_HEADS
RET_DV = 2 * RET_DK
RET_THETA = 10000.0
DIFF_HEADS = D_MODEL // 128
DIFF_DQK = 64
DIFF_DV = 128
N_EXPERTS = 32
TOP_K = 4
D_EXPERT = D_MODEL
SWIGLU_LIMIT = 7.0
SWIGLU_ALPHA = 1.702
MOE_BLOCK = 512
LN_EPS = 1e-5
MASK_VALUE = -1e30
DEEPNORM_ALPHA = (2 * DEPTH) ** 0.25
SEQ_ALIGN = 128
LANES = 128
VMEM_LIMIT = 56 * 1024 * 1024


def _cparams(*sem):
    return pltpu.CompilerParams(dimension_semantics=sem, vmem_limit_bytes=VMEM_LIMIT)


def _tile(n, target, mult=8):
    best = None
    for t in range(mult, min(n, target) + 1, mult):
        if n % t == 0:
            best = t
    assert best is not None, (n, target, mult)
    return best


def _layer_norm_rows(v, g, b):
    mu = jnp.mean(v, axis=-1, keepdims=True)
    c = v - mu
    var = jnp.mean(c * c, axis=-1, keepdims=True)
    return c * lax.rsqrt(var + LN_EPS) * g + b


def _mm_kernel(x_ref, w_ref, o_ref):
    o_ref[...] = jnp.dot(x_ref[...].astype(BF16), w_ref[...],
                         preferred_element_type=F32).astype(o_ref.dtype)


def _matmul(x, w, out_dtype, tm, tn):
    m, k = x.shape
    n = w.shape[1]
    return pl.pallas_call(
        _mm_kernel,
        grid=(n // tn, m // tm),
        in_specs=[pl.BlockSpec((tm, k), lambda j, i: (i, 0)),
                  pl.BlockSpec((k, tn), lambda j, i: (0, j))],
        out_specs=pl.BlockSpec((tm, tn), lambda j, i: (i, j)),
        out_shape=jax.ShapeDtypeStruct((m, n), out_dtype),
        compiler_params=_cparams("parallel", "parallel"),
        name="matmul",
    )(x, w)


def _mm_res_ln_kernel(x_ref, w_ref, res_ref, g_ref, b_ref, o_ref):
    y = jnp.dot(x_ref[...].astype(BF16), w_ref[...], preferred_element_type=F32)
    o_ref[...] = _layer_norm_rows(DEEPNORM_ALPHA * res_ref[...] + y, g_ref[...], b_ref[...])


def _matmul_res_ln(x, w, res, g, b, tm):
    m, k = x.shape
    n = w.shape[1]
    row = lambda i: (i, 0)
    fix = lambda i: (0, 0)
    return pl.pallas_call(
        _mm_res_ln_kernel,
        grid=(m // tm,),
        in_specs=[pl.BlockSpec((tm, k), row), pl.BlockSpec((k, n), fix), pl.BlockSpec((tm, n), row),
                  pl.BlockSpec((1, n), fix), pl.BlockSpec((1, n), fix)],
        out_specs=pl.BlockSpec((tm, n), row),
        out_shape=jax.ShapeDtypeStruct((m, n), F32),
        compiler_params=_cparams("parallel"),
        name="matmul_res_ln",
    )(x, w, res, g.reshape(1, n), b.reshape(1, n))


def _expand_block_diag(a, row_inner, outer, inner):
    rows, k = a.shape
    cols = outer * S5_OCT * inner
    r = jnp.arange(k)[:, None]
    c = jnp.arange(cols)[None, :]
    rep = ((r // inner == c // (S5_OCT * inner)) & (r % inner == c % inner)).astype(BF16)
    out = jnp.dot(a.astype(BF16), rep, preferred_element_type=F32)
    g_row = (jnp.arange(rows)[:, None] // row_inner) % S5_OCT
    h_col = (c // inner) % S5_OCT
    return jnp.where(g_row == h_col, out, 0.0).astype(BF16)


def _s5_tables(lam_re, lam_im, log_step, b_re, b_im, c_re, c_im, d_skip):
    hp = lax.Precision.HIGHEST
    g_, n_, c_, q_, o_ = S5_GROUPS, S5_STATE, S5_GROUP_CH, S5_CHUNK, S5_OCT
    dt = jnp.exp(log_step.astype(F32))[:, None]
    lr = jnp.minimum(lam_re.astype(F32), -1e-4)
    li = lam_im.astype(F32)
    mag = jnp.exp(lr * dt)
    ab_re = mag * jnp.cos(li * dt)
    ab_im = mag * jnp.sin(li * dt)
    den = lr * lr + li * li
    f_re = ((ab_re - 1.0) * lr + ab_im * li) / den
    f_im = (ab_im * lr - (ab_re - 1.0) * li) / den
    br, bi = b_re.astype(F32), b_im.astype(F32)
    bbar_re = f_re[..., None] * br - f_im[..., None] * bi
    bbar_im = f_re[..., None] * bi + f_im[..., None] * br
    j = jnp.arange(q_ + 1, dtype=F32)[:, None, None]
    pmag = jnp.exp(j * (lr * dt)[None])
    pw_re = pmag * jnp.cos(j * (li * dt)[None])
    pw_im = pmag * jnp.sin(j * (li * dt)[None])
    cr, ci = c_re.astype(F32), c_im.astype(F32)
    w_re = pw_re[..., None] * bbar_re[None] - pw_im[..., None] * bbar_im[None]
    w_im = pw_re[..., None] * bbar_im[None] + pw_im[..., None] * bbar_re[None]
    kj = (jnp.einsum('gon,jgni->jgoi', cr, w_re[:q_], precision=hp)
          - jnp.einsum('gon,jgni->jgoi', ci, w_im[:q_], precision=hp))
    kj = kj.at[0].add(d_skip.astype(F32).reshape(g_, c_)[:, :, None] * jnp.eye(c_, dtype=F32)[None])
    s_idx = jnp.arange(q_)[:, None]
    t_idx = jnp.arange(q_)[None, :]
    lag = t_idx - s_idx
    kts = jnp.where((lag >= 0)[:, :, None, None, None], kj[jnp.clip(lag, 0, q_ - 1)], 0.0)
    kts = kts.reshape(q_, q_, N_OCT, o_, c_, c_).transpose(2, 0, 3, 5, 1, 4)
    m_tab = _expand_block_diag(kts.reshape(N_OCT * q_ * o_ * c_, q_ * c_), c_, q_, c_)
    m_tab = m_tab.reshape(N_OCT, q_, o_ * c_, q_ * o_ * c_)
    pst = jnp.stack([w_re[:q_][::-1], w_im[:q_][::-1]], axis=0)
    pst = pst.reshape(2, q_, N_OCT, o_, n_, c_).transpose(2, 1, 3, 5, 0, 4)
    p_tab = _expand_block_diag(pst.reshape(N_OCT * q_ * o_ * c_, 2 * n_), c_, 2, n_)
    p_tab = p_tab.reshape(N_OCT, q_, o_ * c_, 2 * o_ * n_)
    ar = pw_re[1:].reshape(q_, N_OCT, o_, n_)
    ai = pw_im[1:].reshape(q_, N_OCT, o_, n_)
    cr_o = cr.reshape(N_OCT, o_, c_, n_)
    ci_o = ci.reshape(N_OCT, o_, c_, n_)
    r_re = cr_o[None] * ar[:, :, :, None, :] - ci_o[None] * ai[:, :, :, None, :]
    r_im = -(cr_o[None] * ai[:, :, :, None, :] + ci_o[None] * ar[:, :, :, None, :])
    rst = jnp.stack([r_re, r_im], axis=0).transpose(2, 0, 3, 5, 1, 4)
    r_tab = _expand_block_diag(rst.reshape(N_OCT * 2 * o_ * n_, q_ * c_), n_, q_, c_)
    r_tab = r_tab.reshape(N_OCT, 2 * o_ * n_, q_ * o_ * c_)
    a16_re = pw_re[q_].reshape(N_OCT, o_ * n_)
    a16_im = pw_im[q_].reshape(N_OCT, o_ * n_)
    return m_tab, p_tab, r_tab, a16_re, a16_im


def _s5_chunk_rows(u_ref, s, tr, valid):
    us = u_ref[pl.ds(s, tr, stride=S5_CHUNK), :]
    return jnp.where(valid, us, 0.0).astype(BF16)


def _s5_valid(tr, chunks_per_batch, pad_chunks):
    chunk = pl.program_id(1) * tr + lax.broadcasted_iota(jnp.int32, (tr, 1), 0)
    return (chunk % chunks_per_batch) >= pad_chunks


def _s5_state_kernel(u_ref, p_ref, s_ref, *, tr, chunks_per_batch, pad_chunks):
    valid = _s5_valid(tr, chunks_per_batch, pad_chunks)
    acc = jnp.zeros(s_ref.shape, F32)
    for s in range(S5_CHUNK):
        acc += jnp.dot(_s5_chunk_rows(u_ref, s, tr, valid), p_ref[0, s], preferred_element_type=F32)
    s_ref[...] = acc


def _s5_scan_kernel(s_ref, ar_ref, ai_ref, x_ref, st_ref, *, tc):
    half = st_ref.shape[1] // 2

    @pl.when(pl.program_id(1) == 0)
    def _():
        st_ref[...] = jnp.zeros(st_ref.shape, F32)

    ar = ar_ref[...]
    ai = ai_ref[...]

    def body(c, carry):
        xr, xi = carry
        x_ref[c, :, :half] = xr
        x_ref[c, :, half:] = xi
        s = s_ref[c]
        return ar * xr - ai * xi + s[:, :half], ar * xi + ai * xr + s[:, half:]

    xr, xi = lax.fori_loop(0, tc, body, (st_ref[:, :half], st_ref[:, half:]))
    st_ref[:, :half] = xr
    st_ref[:, half:] = xi


def _gelu_tanh(y):
    return 0.5 * y * (1.0 + jnp.tanh(math.sqrt(2.0 / math.pi) * (y + 0.044715 * (y * y * y))))


def _s5_out_kernel(u_ref, xp_ref, m_ref, r_ref, z_ref, acc_ref, *, tr, chunks_per_batch, pad_chunks):
    valid = _s5_valid(tr, chunks_per_batch, pad_chunks)
    acc_ref[...] = jnp.dot(xp_ref[...].astype(BF16), r_ref[0], preferred_element_type=F32)
    for s in range(S5_CHUNK):
        acc_ref[:, s * LANES:] += jnp.dot(_s5_chunk_rows(u_ref, s, tr, valid), m_ref[0, s, :, s * LANES:],
                                          preferred_element_type=F32)
    for t in range(S5_CHUNK):
        z_ref[pl.ds(t, tr, stride=S5_CHUNK), :] = _gelu_tanh(acc_ref[:, t * LANES:(t + 1) * LANES])


def _s5_tail_kernel(z_ref, wg_ref, bg_ref, wo_ref, res_ref, g_ref, b_ref, o_ref):
    z = z_ref[...]
    t = jnp.dot(z.astype(BF16), wg_ref[...], preferred_element_type=F32) + bg_ref[...]
    glu = z * jax.nn.sigmoid(t)
    y = jnp.dot(glu.astype(BF16), wo_ref[...], preferred_element_type=F32)
    o_ref[...] = _layer_norm_rows(DEEPNORM_ALPHA * res_ref[...] + y, g_ref[...], b_ref[...])


def _s5_mixer(h, lp, pad, w_in, lam_re, lam_im, log_step, b_re, b_im, c_re, c_im, d_skip,
              w_glu, b_glu, w_out, ln_g, ln_b, tm):
    tp, d = h.shape
    n_batch = tp // lp
    m_tab, p_tab, r_tab, a16_re, a16_im = _s5_tables(lam_re, lam_im, log_step, b_re, b_im,
                                                     c_re, c_im, d_skip)
    u = _matmul(h, w_in.astype(BF16), F32, tm, d)
    n_chunks = tp // S5_CHUNK
    chunks_per_batch = lp // S5_CHUNK
    pad_chunks = pad // S5_CHUNK
    tr = _tile(n_chunks, 344)
    st_w = 2 * S5_OCT * S5_STATE
    kw = dict(tr=tr, chunks_per_batch=chunks_per_batch, pad_chunks=pad_chunks)
    s_all = pl.pallas_call(
        functools.partial(_s5_state_kernel, **kw),
        grid=(N_OCT, n_chunks // tr),
        in_specs=[pl.BlockSpec((tr * S5_CHUNK, LANES), lambda o, i: (i, o)),
                  pl.BlockSpec((1, S5_CHUNK, LANES, st_w), lambda o, i: (o, 0, 0, 0))],
        out_specs=pl.BlockSpec((tr, st_w), lambda o, i: (i, o)),
        out_shape=jax.ShapeDtypeStruct((n_chunks, N_OCT * st_w), F32),
        compiler_params=_cparams("parallel", "parallel"),
        name="s5_state",
    )(u, p_tab)
    tc = _tile(chunks_per_batch, 129, 1)
    nt = chunks_per_batch // tc
    x_prev = pl.pallas_call(
        functools.partial(_s5_scan_kernel, tc=tc),
        grid=(n_batch, nt),
        in_specs=[pl.BlockSpec((tc, N_OCT, st_w), lambda b, i: (b * nt + i, 0, 0)),
                  pl.BlockSpec((N_OCT, st_w // 2), lambda b, i: (0, 0)),
                  pl.BlockSpec((N_OCT, st_w // 2), lambda b, i: (0, 0))],
        out_specs=pl.BlockSpec((tc, N_OCT, st_w), lambda b, i: (b * nt + i, 0, 0)),
        out_shape=jax.ShapeDtypeStruct((n_chunks, N_OCT, st_w), F32),
        scratch_shapes=[pltpu.VMEM((N_OCT, st_w), F32)],
        compiler_params=_cparams("arbitrary", "arbitrary"),
        name="s5_scan",
    )(s_all.reshape(n_chunks, N_OCT, st_w), a16_re, a16_im)
    z = pl.pallas_call(
        functools.partial(_s5_out_kernel, **kw),
        grid=(N_OCT, n_chunks // tr),
        in_specs=[pl.BlockSpec((tr * S5_CHUNK, LANES), lambda o, i: (i, o)),
                  pl.BlockSpec((tr, st_w), lambda o, i: (i, o)),
                  pl.BlockSpec((1, S5_CHUNK, LANES, S5_CHUNK * LANES), lambda o, i: (o, 0, 0, 0)),
                  pl.BlockSpec((1, st_w, S5_CHUNK * LANES), lambda o, i: (o, 0, 0))],
        out_specs=pl.BlockSpec((tr * S5_CHUNK, LANES), lambda o, i: (i, o)),
        out_shape=jax.ShapeDtypeStruct((tp, d), F32),
        scratch_shapes=[pltpu.VMEM((tr, S5_CHUNK * LANES), F32)],
        compiler_params=_cparams("parallel", "parallel"),
        name="s5_out",
    )(u, x_prev.reshape(n_chunks, N_OCT * st_w), m_tab, r_tab)
    row = lambda i: (i, 0)
    fix = lambda i: (0, 0)
    return pl.pallas_call(
        _s5_tail_kernel,
        grid=(tp // tm,),
        in_specs=[pl.BlockSpec((tm, d), row), pl.BlockSpec((d, d), fix), pl.BlockSpec((1, d), fix),
                  pl.BlockSpec((d, d), fix), pl.BlockSpec((tm, d), row),
                  pl.BlockSpec((1, d), fix), pl.BlockSpec((1, d), fix)],
        out_specs=pl.BlockSpec((tm, d), row),
        out_shape=jax.ShapeDtypeStruct((tp, d), F32),
        compiler_params=_cparams("parallel"),
        name="s5_tail",
    )(z, w_glu.astype(BF16), b_glu.reshape(1, d).astype(F32), w_out.astype(BF16), h,
      ln_g.reshape(1, d), ln_b.reshape(1, d))


def _ret_kernel(q_ref, k_ref, v_ref, gate_ref, cos_ref, sin_ref, dm_ref, qd_ref, kd_ref, cd_ref, g_ref,
                o_ref, st_ref, *, chunk, pad):
    c = pl.program_id(2)

    @pl.when(c == 0)
    def _():
        st_ref[...] = jnp.zeros(st_ref.shape, F32)

    cos = cos_ref[...]
    sin = sin_ref[...]
    half = RET_DK // 2

    def rot(t):
        t1 = t[:, :half]
        t2 = t[:, half:]
        return jnp.concatenate([t1 * cos - t2 * sin, t1 * sin + t2 * cos], axis=-1)

    q = rot(q_ref[...])
    k = rot(k_ref[...]) * (RET_DK ** -0.5)
    pos = c * chunk + lax.broadcasted_iota(jnp.int32, (chunk, 1), 0)
    k = jnp.where(pos >= pad, k, 0.0)
    qb = q.astype(BF16)
    vb = v_ref[...].astype(BF16)
    scores = lax.dot_general(qb, k.astype(BF16), (((1,), (1,)), ((), ())),
                             preferred_element_type=F32) * dm_ref[0]
    intra = jnp.dot(scores.astype(BF16), vb, preferred_element_type=F32)
    state = st_ref[...]
    inter = jnp.dot(qb, state.astype(BF16), preferred_element_type=F32) * qd_ref[0]
    st_ref[...] = state * cd_ref[0] + lax.dot_general((k * kd_ref[0]).astype(BF16), vb,
                                                      (((0,), (0,)), ((), ())),
                                                      preferred_element_type=F32)
    o = intra + inter
    mu = jnp.mean(o, axis=-1, keepdims=True)
    oc = o - mu
    var = jnp.mean(oc * oc, axis=-1, keepdims=True)
    o = oc * lax.rsqrt(var + LN_EPS) * g_ref[...]
    gate = gate_ref[...]
    o_ref[...] = (gate * jax.nn.sigmoid(gate) * o).astype(o_ref.dtype)


def _retention_mixer(h, lp, pad, w_in, gn_g, w_out, ln_g, ln_b, tm):
    tp, d = h.shape
    n_batch = tp // lp
    chunk = _tile(lp, 384, SEQ_ALIGN)
    ncb = lp // chunk
    qd = RET_HEADS * RET_DK
    vd = RET_HEADS * RET_DV
    proj = _matmul(h, w_in.astype(BF16), F32, tm, _tile(w_in.shape[1], 1536, LANES))
    half = RET_DK // 2
    pos = jnp.arange(lp, dtype=F32) - pad
    inv_freq = jnp.power(RET_THETA, -jnp.arange(half, dtype=F32) / half)
    ang = pos[:, None] * inv_freq[None, :]
    cos, sin = jnp.cos(ang), jnp.sin(ang)
    log_gamma = jnp.log1p(-jnp.power(2.0, -5.0 - jnp.arange(RET_HEADS, dtype=F32)))
    idx = jnp.arange(chunk, dtype=F32)
    rel = idx[:, None] - idx[None, :]
    dmask = jnp.where(rel[None] >= 0, jnp.exp(log_gamma[:, None, None] * jnp.maximum(rel, 0.0)[None]), 0.0)
    q_decay = jnp.exp(log_gamma[:, None] * (idx[None, :] + 1.0))[:, :, None]
    k_decay = jnp.exp(log_gamma[:, None] * (chunk - 1.0 - idx[None, :]))[:, :, None]
    chunk_decay = jnp.exp(log_gamma * chunk).reshape(RET_HEADS, 1, 1)
    kb = qd // RET_DK
    vb = 2 * qd // RET_DV
    gb = (2 * qd + vd) // RET_DV
    o = pl.pallas_call(
        functools.partial(_ret_kernel, chunk=chunk, pad=pad),
        grid=(n_batch, RET_HEADS, ncb),
        in_specs=[pl.BlockSpec((chunk, RET_DK), lambda b, hh, c: (b * ncb + c, hh)),
                  pl.BlockSpec((chunk, RET_DK), lambda b, hh, c: (b * ncb + c, kb + hh)),
                  pl.BlockSpec((chunk, RET_DV), lambda b, hh, c: (b * ncb + c, vb + hh)),
                  pl.BlockSpec((chunk, RET_DV), lambda b, hh, c: (b * ncb + c, gb + hh)),
                  pl.BlockSpec((chunk, half), lambda b, hh, c: (c, 0)),
                  pl.BlockSpec((chunk, half), lambda b, hh, c: (c, 0)),
                  pl.BlockSpec((1, chunk, chunk), lambda b, hh, c: (hh, 0, 0)),
                  pl.BlockSpec((1, chunk, 1), lambda b, hh, c: (hh, 0, 0)),
                  pl.BlockSpec((1, chunk, 1), lambda b, hh, c: (hh, 0, 0)),
                  pl.BlockSpec((1, 1, 1), lambda b, hh, c: (hh, 0, 0)),
                  pl.BlockSpec((1, RET_DV), lambda b, hh, c: (0, hh))],
        out_specs=pl.BlockSpec((chunk, RET_DV), lambda b, hh, c: (b * ncb + c, hh)),
        out_shape=jax.ShapeDtypeStruct((tp, vd), BF16),
        scratch_shapes=[pltpu.VMEM((RET_DK, RET_DV), F32)],
        compiler_params=_cparams("parallel", "parallel", "arbitrary"),
        name="retention",
    )(proj, proj, proj, proj, cos, sin, dmask, q_decay, k_decay, chunk_decay,
      gn_g.reshape(1, vd).astype(F32))
    return _matmul_res_ln(o, w_out.astype(BF16), h, ln_g, ln_b, tm)


ONES_ROWS = 16


def _diff_kernel(q_ref, k_ref, vt_ref, lam_ref, g_ref, o_ref, vext_ref, m_ref, acc_ref, s0_ref, s1_ref,
                 s2_ref, s3_ref,
                 *, tq, pad, lambda_init):
    i = pl.program_id(2)
    nkb = vext_ref.shape[0]

    @pl.when(i == 0)
    def _():
        for jb in range(nkb):
            vext_ref[jb, 0:DIFF_DV, :] = vt_ref[:, jb * tq:(jb + 1) * tq]
            vext_ref[jb, DIFF_DV:, :] = jnp.ones((ONES_ROWS, tq), BF16)

    q = q_ref[...]
    lane = lax.broadcasted_iota(jnp.int32, q.shape, 1)
    zero = jnp.zeros_like(q)
    q_both = jnp.concatenate([jnp.where(lane < DIFF_DQK, q, zero), jnp.where(lane >= DIFF_DQK, q, zero)],
                             axis=0)
    m_ref[...] = jnp.full(m_ref.shape, MASK_VALUE, F32)
    acc_ref[...] = jnp.zeros(acc_ref.shape, F32)

    def scores(j):
        start = j * tq if isinstance(j, int) else pl.multiple_of(j * tq, tq)
        k = k_ref[pl.ds(start, tq), :]
        return lax.dot_general(k, q_both, (((1,), (1,)), ((), ())), preferred_element_type=F32)

    def consume(s, j, masked):
        if masked:
            kpos = j * tq + lax.broadcasted_iota(jnp.int32, (tq, tq), 0)
            qpos = i * tq + lax.broadcasted_iota(jnp.int32, (tq, tq), 1)
            allowed = (kpos <= qpos) & (kpos >= pad)
            allowed = jnp.concatenate([allowed, allowed], axis=1)
            s = jnp.where(allowed, s, MASK_VALUE)
        m_prev = m_ref[...]
        m_new = jnp.maximum(m_prev, jnp.max(s, axis=0, keepdims=True))
        alpha = jnp.exp2(m_prev - m_new)
        pe = jnp.exp2((s - m_new).astype(BF16))
        acc_ref[...] = alpha * acc_ref[...] + jnp.dot(vext_ref[j], pe, preferred_element_type=F32)
        m_ref[...] = m_new

    def consume2(sa, sb, j):
        m_prev = m_ref[...]
        m_new = jnp.maximum(m_prev, jnp.maximum(jnp.max(sa, axis=0, keepdims=True),
                                                jnp.max(sb, axis=0, keepdims=True)))
        alpha = jnp.exp2(m_prev - m_new)
        pe = jnp.concatenate([jnp.exp2((sa - m_new).astype(BF16)), jnp.exp2((sb - m_new).astype(BF16))], axis=0)
        v2 = jnp.concatenate([vext_ref[j], vext_ref[j + 1]], axis=1)
        acc_ref[...] = alpha * acc_ref[...] + jnp.dot(v2, pe, preferred_element_type=F32)
        m_ref[...] = m_new

    consume(scores(0), 0, True)

    @pl.when(i > 0)
    def _():
        n = i - 1
        n_quads = n // 4
        s0_ref[...] = scores(1)
        s1_ref[...] = scores(jnp.minimum(2, i))

        def quad(u, carry):
            a = 1 + 4 * u
            s2_ref[...] = scores(a + 2)
            s3_ref[...] = scores(a + 3)
            consume2(s0_ref[...], s1_ref[...], a)
            s0_ref[...] = scores(jnp.minimum(a + 4, i))
            s1_ref[...] = scores(jnp.minimum(a + 5, i))
            consume2(s2_ref[...], s3_ref[...], a + 2)
            return carry

        lax.fori_loop(0, n_quads, quad, 0)
        first = 1 + 4 * n_quads

        def pair(jj, carry):
            a = first + 2 * jj
            s1_ref[...] = scores(a + 1)
            consume(s0_ref[...], a, False)
            s0_ref[...] = scores(a + 2)
            consume(s1_ref[...], a + 1, False)
            return carry

        lax.fori_loop(0, (n - 4 * n_quads) // 2, pair, 0)

        @pl.when(n % 2 == 1)
        def _():
            s1_ref[...] = scores(i)
            consume(s0_ref[...], i - 1, False)
            consume(s1_ref[...], i, True)

        @pl.when(n % 2 == 0)
        def _():
            consume(s0_ref[...], i, True)

    lam_p = lam_ref[...]
    lam = (jnp.exp(jnp.sum(lam_p[0:1] * lam_p[1:2], axis=-1, keepdims=True))
           - jnp.exp(jnp.sum(lam_p[2:3] * lam_p[3:4], axis=-1, keepdims=True)) + lambda_init)
    acc = acc_ref[...]
    o_both = acc[0:DIFF_DV] / acc[DIFF_DV:DIFF_DV + 1]
    o = o_both[:, :tq] - lam * o_both[:, tq:]
    o = o * lax.rsqrt(jnp.mean(o * o, axis=0, keepdims=True) + LN_EPS)
    o = o * (g_ref[...] * (1.0 - lambda_init))
    o_ref[...] = o.T.astype(o_ref.dtype)


def _mm_nt_kernel(wt_ref, x_ref, o_ref):
    o_ref[...] = lax.dot_general(wt_ref[...], x_ref[...].astype(BF16), (((1,), (1,)), ((), ())),
                                 preferred_element_type=F32).astype(o_ref.dtype)


def _matmul_nt(wt, x, out_dtype, tm):
    n, k = wt.shape
    m = x.shape[0]
    return pl.pallas_call(
        _mm_nt_kernel,
        grid=(m // tm,),
        in_specs=[pl.BlockSpec((n, k), lambda i: (0, 0)), pl.BlockSpec((tm, k), lambda i: (i, 0))],
        out_specs=pl.BlockSpec((n, tm), lambda i: (0, i)),
        out_shape=jax.ShapeDtypeStruct((n, m), out_dtype),
        compiler_params=_cparams("parallel"),
        name="matmul_nt",
    )(wt, x)


def _diff_attn_mixer(h, lp, pad, w_in, lq1, lk1, lq2, lk2, subln_g, w_out, lambda_init, ln_g, ln_b, tm):
    tp, d = h.shape
    n_batch = tp // lp
    qk = DIFF_HEADS * 2 * DIFF_DQK
    col_scale = jnp.concatenate([jnp.full((qk,), DIFF_DQK ** -0.5 * math.log2(math.e), F32),
                                 jnp.ones((qk,), F32)])
    proj = _matmul(h, (w_in[:, :2 * qk] * col_scale[None, :]).astype(BF16), BF16, tm, _tile(2 * qk, 2048, LANES))
    vt = _matmul_nt(w_in[:, 2 * qk:].T.astype(BF16), h, BF16, _tile(tp, 768, LANES))
    tq = _tile(lp, 384, SEQ_ALIGN)
    nq = lp // tq
    lam_p = jnp.stack([lq1, lk1, lq2, lk2]).astype(F32)
    o = pl.pallas_call(
        functools.partial(_diff_kernel, tq=tq, pad=pad, lambda_init=lambda_init),
        grid=(n_batch, DIFF_HEADS, nq),
        in_specs=[pl.BlockSpec((tq, LANES), lambda b, hh, i: (b * nq + i, hh)),
                  pl.BlockSpec((lp, LANES), lambda b, hh, i: (b, DIFF_HEADS + hh)),
                  pl.BlockSpec((DIFF_DV, lp), lambda b, hh, i: (hh, b)),
                  pl.BlockSpec((4, DIFF_DQK), lambda b, hh, i: (0, 0)),
                  pl.BlockSpec((DIFF_DV, 1), lambda b, hh, i: (0, 0))],
        out_specs=pl.BlockSpec((tq, DIFF_DV), lambda b, hh, i: (b * nq + i, hh)),
        out_shape=jax.ShapeDtypeStruct((tp, DIFF_HEADS * DIFF_DV), BF16),
        scratch_shapes=[pltpu.VMEM((nq, DIFF_DV + ONES_ROWS, tq), BF16),
                        pltpu.VMEM((1, 2 * tq), F32),
                        pltpu.VMEM((DIFF_DV + ONES_ROWS, 2 * tq), F32),
                        ] + [pltpu.VMEM((tq, 2 * tq), F32)] * 4,
        compiler_params=_cparams("parallel", "parallel", "arbitrary"),
        name="diff_attn",
    )(proj, proj, vt, lam_p, subln_g.reshape(DIFF_DV, 1).astype(F32))
    return _matmul_res_ln(o, w_out.astype(BF16), h, ln_g, ln_b, tm)


SORT_ALIGN = 16
SORT_CHUNK = 512


def _router_kernel(x_ref, w_ref, b_ref, idx_ref, gate_ref, rank_ref, cnt_ref, *, tm):
    logits = jnp.dot(x_ref[...], w_ref[...], preferred_element_type=F32,
                     precision=lax.Precision.HIGHEST) + b_ref[...]
    lane = lax.broadcasted_iota(jnp.int32, logits.shape, 1).astype(F32)
    work = jnp.where(lane < N_EXPERTS, logits, -jnp.inf)
    vals, idxs = [], []
    picked = jnp.zeros(logits.shape, F32)
    for _ in range(TOP_K):
        m = jnp.max(work, axis=-1, keepdims=True)
        sel = jnp.min(jnp.where(work == m, lane, float(LANES)), axis=-1, keepdims=True)
        hit = lane == sel
        work = jnp.where(hit, -jnp.inf, work)
        picked = jnp.where(hit, 1.0, picked)
        vals.append(m)
        idxs.append(sel)
    exps = [jnp.exp(v - vals[0]) for v in vals]
    tot = exps[0] + exps[1] + exps[2] + exps[3]
    r = lax.broadcasted_iota(jnp.int32, (tm, tm), 0)
    c = lax.broadcasted_iota(jnp.int32, (tm, tm), 1)
    lower = jnp.where(c < r, 1.0, 0.0).astype(BF16)
    before = jnp.dot(lower, picked.astype(BF16), preferred_element_type=F32)
    for kk in range(TOP_K):
        idx_ref[:, kk:kk + 1] = idxs[kk].astype(jnp.int32)
        gate_ref[:, kk:kk + 1] = exps[kk] / tot
        rank_ref[:, kk:kk + 1] = jnp.sum(jnp.where(lane == idxs[kk], before, 0.0), axis=-1,
                                         keepdims=True).astype(jnp.int32)
    cnt_ref[0] = jnp.sum(picked, axis=0, keepdims=True)


def _segment_copies(tab_ref, tile, local_ref, slots_ref, sem, to_slots):
    n_seg = tab_ref.shape[0] // 3

    def copy(e, c):
        lo = pl.multiple_of(tab_ref[tile * N_EXPERTS + e] + c * SORT_ALIGN, SORT_ALIGN)
        go = pl.multiple_of(tab_ref[n_seg + tile * N_EXPERTS + e] + c * SORT_ALIGN, SORT_ALIGN)
        local = local_ref.at[pl.ds(lo, SORT_ALIGN), :]
        slots = slots_ref.at[pl.ds(go, SORT_ALIGN), :]
        return pltpu.make_async_copy(local, slots, sem) if to_slots else pltpu.make_async_copy(slots, local, sem)

    for wait in (False, True):
        def per_expert(e, carry):
            def per_chunk(c, carry2):
                if wait:
                    copy(e, c).wait()
                else:
                    copy(e, c).start()
                return carry2
            return lax.fori_loop(0, tab_ref[2 * n_seg + tile * N_EXPERTS + e], per_chunk, carry)
        lax.fori_loop(0, N_EXPERTS, per_expert, 0)


def _dispatch_kernel(tab_ref, x_ref, lrow_ref, slots_in_ref, slots_ref, sorted_ref, sem, *, tm, rs):
    del slots_in_ref
    xb = x_ref[...].astype(BF16)
    lr = lrow_ref[...]
    for rc in range(rs // SORT_CHUNK):
        rows = rc * SORT_CHUNK + lax.broadcasted_iota(jnp.int32, (SORT_CHUNK, tm), 0)
        hit = (rows == lr[0:1]) | (rows == lr[1:2]) | (rows == lr[2:3]) | (rows == lr[3:4])
        perm = jnp.where(hit, 1.0, 0.0).astype(BF16)
        sorted_ref[rc * SORT_CHUNK:(rc + 1) * SORT_CHUNK, :] = jnp.dot(
            perm, xb, preferred_element_type=F32).astype(BF16)
    _segment_copies(tab_ref, pl.program_id(0), sorted_ref, slots_ref, sem, True)


def _combine_kernel(tab_ref, res_ref, lrow_ref, gate_ref, g_ref, b_ref, y_ref, o_ref, ysort_ref, sem,
                    *, tm, rs):
    @pl.when(pl.program_id(0) == 0)
    def _():
        ysort_ref[...] = jnp.zeros(ysort_ref.shape, BF16)

    _segment_copies(tab_ref, pl.program_id(0), ysort_ref, y_ref, sem, False)
    lr = lrow_ref[...]
    gt = gate_ref[...]
    y = jnp.zeros(res_ref.shape, F32)
    for rc in range(rs // SORT_CHUNK):
        cols = rc * SORT_CHUNK + lax.broadcasted_iota(jnp.int32, (tm, SORT_CHUNK), 1)
        w = jnp.zeros((tm, SORT_CHUNK), F32)
        for kk in range(TOP_K):
            w = jnp.where(cols == lr[:, kk:kk + 1], gt[:, kk:kk + 1], w)
        y += jnp.dot(w.astype(BF16), ysort_ref[rc * SORT_CHUNK:(rc + 1) * SORT_CHUNK, :],
                     preferred_element_type=F32)
    o_ref[...] = _layer_norm_rows(DEEPNORM_ALPHA * res_ref[...] + y, g_ref[...], b_ref[...])


def _expert_kernel(be_ref, nu_ref, x_ref, wu_ref, bu_ref, wd_ref, bd_ref, o_ref, wu_bf_ref, wd_bf_ref):
    i = pl.program_id(0)
    live = i < nu_ref[0]

    @pl.when(live & ((i == 0) | (be_ref[i] != be_ref[jnp.maximum(i - 1, 0)])))
    def _():
        wu_bf_ref[...] = wu_ref[0, 0].astype(BF16)
        wd_bf_ref[...] = wd_ref[0, 0].astype(BF16)

    @pl.when(live)
    def _():
        hb = jnp.dot(x_ref[...], wu_bf_ref[...], preferred_element_type=F32) + bu_ref[0]
        x_glu = jnp.minimum(hb[:, :D_EXPERT], SWIGLU_LIMIT)
        x_lin = jnp.clip(hb[:, D_EXPERT:], -SWIGLU_LIMIT, SWIGLU_LIMIT)
        act = x_glu * jax.nn.sigmoid(SWIGLU_ALPHA * x_glu) * (x_lin + 1.0)
        y = jnp.dot(act.astype(BF16), wd_bf_ref[...], preferred_element_type=F32) + bd_ref[0]
        o_ref[...] = y.astype(o_ref.dtype)

    @pl.when(jnp.logical_not(live))
    def _():
        o_ref[...] = jnp.zeros(o_ref.shape, o_ref.dtype)


def _moe_ffn(h, layer, w_router, b_router, w_up_all, b_up, w_down_all, b_down, ln_g, ln_b, tm, bm):
    tp, d = h.shape
    a = tp * TOP_K
    wr = jnp.zeros((d, LANES), F32).at[:, :N_EXPERTS].set(w_router.astype(F32))
    br = jnp.zeros((1, LANES), F32).at[0, :N_EXPERTS].set(b_router.astype(F32))
    row = lambda i: (i, 0)
    fix = lambda i: (0, 0)
    nt = tp // tm
    idx, gates, rank, cnt = pl.pallas_call(
        functools.partial(_router_kernel, tm=tm),
        grid=(nt,),
        in_specs=[pl.BlockSpec((tm, d), row), pl.BlockSpec((d, LANES), fix), pl.BlockSpec((1, LANES), fix)],
        out_specs=[pl.BlockSpec((tm, TOP_K), row), pl.BlockSpec((tm, TOP_K), row),
                   pl.BlockSpec((tm, TOP_K), row), pl.BlockSpec((1, 1, LANES), lambda i: (i, 0, 0))],
        out_shape=[jax.ShapeDtypeStruct((tp, TOP_K), jnp.int32), jax.ShapeDtypeStruct((tp, TOP_K), F32),
                   jax.ShapeDtypeStruct((tp, TOP_K), jnp.int32), jax.ShapeDtypeStruct((nt, 1, LANES), F32)],
        compiler_params=_cparams("parallel"),
        name="router",
    )(h, wr, br)
    counts = cnt[:, 0, :N_EXPERTS].astype(jnp.int32)
    seg = ((counts + SORT_ALIGN - 1) // SORT_ALIGN) * SORT_ALIGN
    tile_off = jnp.cumsum(seg, axis=0) - seg
    region = ((jnp.sum(seg, axis=0) + bm - 1) // bm) * bm
    region_end = jnp.cumsum(region)
    slot_off = (region_end - region)[None, :] + tile_off
    local_off = jnp.cumsum(seg, axis=1) - seg
    table = jnp.concatenate([local_off.reshape(-1), slot_off.reshape(-1),
                             (seg // SORT_ALIGN).reshape(-1)]).astype(jnp.int32)
    onehot = idx.reshape(nt, tm, TOP_K, 1) == jnp.arange(N_EXPERTS, dtype=jnp.int32)
    lrow = jnp.sum(jnp.where(onehot, local_off[:, None, None, :], 0), axis=-1).reshape(tp, TOP_K) + rank
    lrow_t = jnp.concatenate([lrow.T, jnp.full((8 - TOP_K, tp), -1, jnp.int32)], axis=0)
    rs = -(-(TOP_K * tm + N_EXPERTS * (SORT_ALIGN - 1)) // SORT_CHUNK) * SORT_CHUNK
    n_blocks = -(-(a + nt * N_EXPERTS * (SORT_ALIGN - 1)) // bm) + N_EXPERTS
    n_slots = n_blocks * bm
    block_start = jnp.arange(n_blocks, dtype=jnp.int32) * bm
    block_e = jnp.minimum(jnp.sum((region_end[None, :] <= block_start[:, None]).astype(jnp.int32), axis=1),
                          N_EXPERTS - 1)
    n_used = (region_end[-1:] // bm).astype(jnp.int32)
    x_slots = pl.pallas_call(
        functools.partial(_dispatch_kernel, tm=tm, rs=rs),
        grid_spec=pltpu.PrefetchScalarGridSpec(
            num_scalar_prefetch=1,
            grid=(nt,),
            in_specs=[pl.BlockSpec((tm, d), lambda i, tab: (i, 0)),
                      pl.BlockSpec((8, tm), lambda i, tab: (0, i)),
                      pl.BlockSpec(memory_space=pl.ANY)],
            out_specs=pl.BlockSpec(memory_space=pl.ANY),
            scratch_shapes=[pltpu.VMEM((rs, d), BF16), pltpu.SemaphoreType.DMA(())],
        ),
        out_shape=jax.ShapeDtypeStruct((n_slots, d), BF16),
        input_output_aliases={3: 0},
        compiler_params=_cparams("arbitrary"),
        name="moe_dispatch",
    )(table, h, lrow_t, jnp.zeros((n_slots, d), BF16))
    y_slots = pl.pallas_call(
        _expert_kernel,
        grid_spec=pltpu.PrefetchScalarGridSpec(
            num_scalar_prefetch=2,
            grid=(n_blocks,),
            in_specs=[pl.BlockSpec((bm, d), lambda i, be, nu: (i, 0)),
                      pl.BlockSpec((1, 1, d, 2 * D_EXPERT), lambda i, be, nu: (layer, be[i], 0, 0)),
                      pl.BlockSpec((1, 1, 2 * D_EXPERT), lambda i, be, nu: (be[i], 0, 0)),
                      pl.BlockSpec((1, 1, D_EXPERT, d), lambda i, be, nu: (layer, be[i], 0, 0)),
                      pl.BlockSpec((1, 1, d), lambda i, be, nu: (be[i], 0, 0))],
            out_specs=pl.BlockSpec((bm, d), lambda i, be, nu: (i, 0)),
            scratch_shapes=[pltpu.VMEM((d, 2 * D_EXPERT), BF16), pltpu.VMEM((D_EXPERT, d), BF16)],
        ),
        out_shape=jax.ShapeDtypeStruct((n_slots, d), BF16),
        compiler_params=_cparams("arbitrary"),
        name="experts",
    )(block_e, n_used, x_slots, w_up_all, b_up.reshape(N_EXPERTS, 1, 2 * D_EXPERT).astype(F32),
      w_down_all, b_down.reshape(N_EXPERTS, 1, d).astype(F32))
    return pl.pallas_call(
        functools.partial(_combine_kernel, tm=tm, rs=rs),
        grid_spec=pltpu.PrefetchScalarGridSpec(
            num_scalar_prefetch=1,
            grid=(nt,),
            in_specs=[pl.BlockSpec((tm, d), lambda i, tab: (i, 0)),
                      pl.BlockSpec((tm, TOP_K), lambda i, tab: (i, 0)),
                      pl.BlockSpec((tm, TOP_K), lambda i, tab: (i, 0)),
                      pl.BlockSpec((1, d), lambda i, tab: (0, 0)),
                      pl.BlockSpec((1, d), lambda i, tab: (0, 0)),
                      pl.BlockSpec(memory_space=pl.ANY)],
            out_specs=pl.BlockSpec((tm, d), lambda i, tab: (i, 0)),
            scratch_shapes=[pltpu.VMEM((rs, d), BF16), pltpu.SemaphoreType.DMA(())],
        ),
        out_shape=jax.ShapeDtypeStruct((tp, d), F32),
        compiler_params=_cparams("arbitrary"),
        name="moe_combine_ln",
    )(table, h, lrow, gates, ln_g.reshape(1, d), ln_b.reshape(1, d), y_slots)


def kernel(x, meta_tokens, s5_w_in, s5_lambda_re, s5_lambda_im, s5_log_step, s5_b_re, s5_b_im, s5_c_re, s5_c_im, s5_d, s5_w_glu, s5_b_glu, s5_w_out, ret_w_in, ret_gn_g, ret_w_out, diff_w_in, diff_lambda_q1, diff_lambda_k1, diff_lambda_q2, diff_lambda_k2, diff_subln_g, diff_w_out, ln_mix_g, ln_mix_b, moe_w_router, moe_b_router, moe_w_up, moe_b_up, moe_w_down, moe_b_down, ln_ffn_g, ln_ffn_b):
    n_batch, seq, d = x.shape
    length = seq + N_META
    pad = (-length) % SEQ_ALIGN
    lp = length + pad
    assert pad % S5_CHUNK == 0 and d == D_MODEL
    tp = n_batch * lp
    tm = _tile(tp, 768)
    meta = jnp.broadcast_to(meta_tokens[None].astype(x.dtype), (n_batch, N_META, d))
    h = jnp.concatenate([jnp.zeros((n_batch, pad, d), x.dtype), meta, x], axis=1).reshape(tp, d)
    for i in range(DEPTH):
        kind = i % N_MIXERS
        j = i // N_MIXERS
        if kind == 0:
            h = _s5_mixer(h, lp, pad, s5_w_in[j], s5_lambda_re[j], s5_lambda_im[j], s5_log_step[j],
                          s5_b_re[j], s5_b_im[j], s5_c_re[j], s5_c_im[j], s5_d[j],
                          s5_w_glu[j], s5_b_glu[j], s5_w_out[j], ln_mix_g[i], ln_mix_b[i], tm)
        elif kind == 1:
            h = _retention_mixer(h, lp, pad, ret_w_in[j], ret_gn_g[j], ret_w_out[j],
                                 ln_mix_g[i], ln_mix_b[i], tm)
        else:
            lambda_init = 0.8 - 0.6 * math.exp(-0.3 * i)
            h = _diff_attn_mixer(h, lp, pad, diff_w_in[j], diff_lambda_q1[j], diff_lambda_k1[j],
                                 diff_lambda_q2[j], diff_lambda_k2[j], diff_subln_g[j], diff_w_out[j],
                                 lambda_init, ln_mix_g[i], ln_mix_b[i], tm)
        h = _moe_ffn(h, i, moe_w_router[i], moe_b_router[i], moe_w_up, moe_b_up[i],
                     moe_w_down, moe_b_down[i], ln_ffn_g[i], ln_ffn_b[i], tm, MOE_BLOCK)
    return h.reshape(n_batch, lp, d)[:, pad + N_META:, :]
```

```python
import functools
import math

import jax
import jax.numpy as jnp
from jax import lax
from jax.experimental import pallas as pl
from jax.experimental.pallas import tpu as pltpu

F32 = jnp.float32
BF16 = jnp.bfloat16

D_MODEL = 1024
DEPTH = 4
N_META = 16
N_MIXERS = 3
S5_GROUP_CH = 16
S5_GROUPS = D_MODEL // S5_GROUP_CH
S5_STATE = 64
S5_CHUNK = 16
S5_OCT = 8
N_OCT = S5_GROUPS // S5_OCT
RET_HEADS = 4
RET_DK = D_MODEL // RET_HEADS
RET_DV = 2 * RET_DK
RET_THETA = 10000.0
DIFF_HEADS = D_MODEL // 128
DIFF_DQK = 64
DIFF_DV = 128
N_EXPERTS = 32
TOP_K = 4
D_EXPERT = D_MODEL
SWIGLU_LIMIT = 7.0
SWIGLU_ALPHA = 1.702
MOE_BLOCK = 512
LN_EPS = 1e-5
MASK_VALUE = -1e30
DEEPNORM_ALPHA = (2 * DEPTH) ** 0.25
SEQ_ALIGN = 128
LANES = 128
VMEM_LIMIT = 56 * 1024 * 1024


def _cparams(*sem):
    return pltpu.CompilerParams(dimension_semantics=sem, vmem_limit_bytes=VMEM_LIMIT)


def _tile(n, target, mult=8):
    best = None
    for t in range(mult, min(n, target) + 1, mult):
        if n % t == 0:
            best = t
    assert best is not None, (n, target, mult)
    return best


def _layer_norm_rows(v, g, b):
    mu = jnp.mean(v, axis=-1, keepdims=True)
    c = v - mu
    var = jnp.mean(c * c, axis=-1, keepdims=True)
    return c * lax.rsqrt(var + LN_EPS) * g + b


def _mm_kernel(x_ref, w_ref, o_ref):
    o_ref[...] = jnp.dot(x_ref[...].astype(BF16), w_ref[...],
                         preferred_element_type=F32).astype(o_ref.dtype)


def _matmul(x, w, out_dtype, tm, tn):
    m, k = x.shape
    n = w.shape[1]
    return pl.pallas_call(
        _mm_kernel,
        grid=(n // tn, m // tm),
        in_specs=[pl.BlockSpec((tm, k), lambda j, i: (i, 0)),
                  pl.BlockSpec((k, tn), lambda j, i: (0, j))],
        out_specs=pl.BlockSpec((tm, tn), lambda j, i: (i, j)),
        out_shape=jax.ShapeDtypeStruct((m, n), out_dtype),
        compiler_params=_cparams("parallel", "parallel"),
        name="matmul",
    )(x, w)


def _mm_res_ln_kernel(x_ref, w_ref, res_ref, g_ref, b_ref, o_ref):
    y = jnp.dot(x_ref[...].astype(BF16), w_ref[...], preferred_element_type=F32)
    o_ref[...] = _layer_norm_rows(DEEPNORM_ALPHA * res_ref[...] + y, g_ref[...], b_ref[...])


def _matmul_res_ln(x, w, res, g, b, tm):
    m, k = x.shape
    n = w.shape[1]
    row = lambda i: (i, 0)
    fix = lambda i: (0, 0)
    return pl.pallas_call(
        _mm_res_ln_kernel,
        grid=(m // tm,),
        in_specs=[pl.BlockSpec((tm, k), row), pl.BlockSpec((k, n), fix), pl.BlockSpec((tm, n), row),
                  pl.BlockSpec((1, n), fix), pl.BlockSpec((1, n), fix)],
        out_specs=pl.BlockSpec((tm, n), row),
        out_shape=jax.ShapeDtypeStruct((m, n), F32),
        compiler_params=_cparams("parallel"),
        name="matmul_res_ln",
    )(x, w, res, g.reshape(1, n), b.reshape(1, n))


def _expand_block_diag(a, row_inner, outer, inner):
    rows, k = a.shape
    cols = outer * S5_OCT * inner
    r = jnp.arange(k)[:, None]
    c = jnp.arange(cols)[None, :]
    rep = ((r // inner == c // (S5_OCT * inner)) & (r % inner == c % inner)).astype(BF16)
    out = jnp.dot(a.astype(BF16), rep, preferred_element_type=F32)
    g_row = (jnp.arange(rows)[:, None] // row_inner) % S5_OCT
    h_col = (c // inner) % S5_OCT
    return jnp.where(g_row == h_col, out, 0.0).astype(BF16)


def _s5_tables(lam_re, lam_im, log_step, b_re, b_im, c_re, c_im, d_skip):
    hp = lax.Precision.HIGHEST
    g_, n_, c_, q_, o_ = S5_GROUPS, S5_STATE, S5_GROUP_CH, S5_CHUNK, S5_OCT
    dt = jnp.exp(log_step.astype(F32))[:, None]
    lr = jnp.minimum(lam_re.astype(F32), -1e-4)
    li = lam_im.astype(F32)
    mag = jnp.exp(lr * dt)
    ab_re = mag * jnp.cos(li * dt)
    ab_im = mag * jnp.sin(li * dt)
    den = lr * lr + li * li
    f_re = ((ab_re - 1.0) * lr + ab_im * li) / den
    f_im = (ab_im * lr - (ab_re - 1.0) * li) / den
    br, bi = b_re.astype(F32), b_im.astype(F32)
    bbar_re = f_re[..., None] * br - f_im[..., None] * bi
    bbar_im = f_re[..., None] * bi + f_im[..., None] * br
    j = jnp.arange(q_ + 1, dtype=F32)[:, None, None]
    pmag = jnp.exp(j * (lr * dt)[None])
    pw_re = pmag * jnp.cos(j * (li * dt)[None])
    pw_im = pmag * jnp.sin(j * (li * dt)[None])
    cr, ci = c_re.astype(F32), c_im.astype(F32)
    w_re = pw_re[..., None] * bbar_re[None] - pw_im[..., None] * bbar_im[None]
    w_im = pw_re[..., None] * bbar_im[None] + pw_im[..., None] * bbar_re[None]
    kj = (jnp.einsum('gon,jgni->jgoi', cr, w_re[:q_], precision=hp)
          - jnp.einsum('gon,jgni->jgoi', ci, w_im[:q_], precision=hp))
    kj = kj.at[0].add(d_skip.astype(F32).reshape(g_, c_)[:, :, None] * jnp.eye(c_, dtype=F32)[None])
    s_idx = jnp.arange(q_)[:, None]
    t_idx = jnp.arange(q_)[None, :]
    lag = t_idx - s_idx
    kts = jnp.where((lag >= 0)[:, :, None, None, None], kj[jnp.clip(lag, 0, q_ - 1)], 0.0)
    kts = kts.reshape(q_, q_, N_OCT, o_, c_, c_).transpose(2, 0, 3, 5, 1, 4)
    m_tab = _expand_block_diag(kts.reshape(N_OCT * q_ * o_ * c_, q_ * c_), c_, q_, c_)
    m_tab = m_tab.reshape(N_OCT, q_, o_ * c_, q_ * o_ * c_)
    pst = jnp.stack([w_re[:q_][::-1], w_im[:q_][::-1]], axis=0)
    pst = pst.reshape(2, q_, N_OCT, o_, n_, c_).transpose(2, 1, 3, 5, 0, 4)
    p_tab = _expand_block_diag(pst.reshape(N_OCT * q_ * o_ * c_, 2 * n_), c_, 2, n_)
    p_tab = p_tab.reshape(N_OCT, q_, o_ * c_, 2 * o_ * n_)
    ar = pw_re[1:].reshape(q_, N_OCT, o_, n_)
    ai = pw_im[1:].reshape(q_, N_OCT, o_, n_)
    cr_o = cr.reshape(N_OCT, o_, c_, n_)
    ci_o = ci.reshape(N_OCT, o_, c_, n_)
    r_re = cr_o[None] * ar[:, :, :, None, :] - ci_o[None] * ai[:, :, :, None, :]
    r_im = -(cr_o[None] * ai[:, :, :, None, :] + ci_o[None] * ar[:, :, :, None, :])
    rst = jnp.stack([r_re, r_im], axis=0).transpose(2, 0, 3, 5, 1, 4)
    r_tab = _expand_block_diag(rst.reshape(N_OCT * 2 * o_ * n_, q_ * c_), n_, q_, c_)
    r_tab = r_tab.reshape(N_OCT, 2 * o_ * n_, q_ * o_ * c_)
    a16_re = pw_re[q_].reshape(N_OCT, o_ * n_)
    a16_im = pw_im[q_].reshape(N_OCT, o_ * n_)
    return m_tab, p_tab, r_tab, a16_re, a16_im


def _s5_chunk_rows(u_ref, s, tr, valid):
    us = u_ref[pl.ds(s, tr, stride=S5_CHUNK), :]
    return jnp.where(valid, us, 0.0).astype(BF16)


def _s5_valid(tr, chunks_per_batch, pad_chunks):
    chunk = pl.program_id(1) * tr + lax.broadcasted_iota(jnp.int32, (tr, 1), 0)
    return (chunk % chunks_per_batch) >= pad_chunks


def _s5_state_kernel(u_ref, p_ref, s_ref, *, tr, chunks_per_batch, pad_chunks):
    valid = _s5_valid(tr, chunks_per_batch, pad_chunks)
    acc = jnp.zeros(s_ref.shape, F32)
    for s in range(S5_CHUNK):
        acc += jnp.dot(_s5_chunk_rows(u_ref, s, tr, valid), p_ref[0, s], preferred_element_type=F32)
    s_ref[...] = acc


def _s5_scan_kernel(s_ref, ar_ref, ai_ref, x_ref, st_ref, *, tc):
    half = st_ref.shape[1] // 2

    @pl.when(pl.program_id(1) == 0)
    def _():
        st_ref[...] = jnp.zeros(st_ref.shape, F32)

    ar = ar_ref[...]
    ai = ai_ref[...]

    def body(c, carry):
        xr, xi = carry
        x_ref[c, :, :half] = xr
        x_ref[c, :, half:] = xi
        s = s_ref[c]
        return ar * xr - ai * xi + s[:, :half], ar * xi + ai * xr + s[:, half:]

    xr, xi = lax.fori_loop(0, tc, body, (st_ref[:, :half], st_ref[:, half:]))
    st_ref[:, :half] = xr
    st_ref[:, half:] = xi


def _gelu_tanh(y):
    return 0.5 * y * (1.0 + jnp.tanh(math.sqrt(2.0 / math.pi) * (y + 0.044715 * (y * y * y))))


def _s5_out_kernel(u_ref, xp_ref, m_ref, r_ref, z_ref, acc_ref, *, tr, chunks_per_batch, pad_chunks):
    valid = _s5_valid(tr, chunks_per_batch, pad_chunks)
    acc_ref[...] = jnp.dot(xp_ref[...].astype(BF16), r_ref[0], preferred_element_type=F32)
    for s in range(S5_CHUNK):
        acc_ref[:, s * LANES:] += jnp.dot(_s5_chunk_rows(u_ref, s, tr, valid), m_ref[0, s, :, s * LANES:],
                                          preferred_element_type=F32)
    for t in range(S5_CHUNK):
        z_ref[pl.ds(t, tr, stride=S5_CHUNK), :] = _gelu_tanh(acc_ref[:, t * LANES:(t + 1) * LANES])


def _s5_tail_kernel(z_ref, wg_ref, bg_ref, wo_ref, res_ref, g_ref, b_ref, o_ref):
    z = z_ref[...]
    t = jnp.dot(z.astype(BF16), wg_ref[...], preferred_element_type=F32) + bg_ref[...]
    glu = z * jax.nn.sigmoid(t)
    y = jnp.dot(glu.astype(BF16), wo_ref[...], preferred_element_type=F32)
    o_ref[...] = _layer_norm_rows(DEEPNORM_ALPHA * res_ref[...] + y, g_ref[...], b_ref[...])


def _s5_mixer(h, lp, pad, w_in, lam_re, lam_im, log_step, b_re, b_im, c_re, c_im, d_skip,
              w_glu, b_glu, w_out, ln_g, ln_b, tm):
    tp, d = h.shape
    n_batch = tp // lp
    m_tab, p_tab, r_tab, a16_re, a16_im = _s5_tables(lam_re, lam_im, log_step, b_re, b_im,
                                                     c_re, c_im, d_skip)
    u = _matmul(h, w_in.astype(BF16), F32, tm, d)
    n_chunks = tp // S5_CHUNK
    chunks_per_batch = lp // S5_CHUNK
    pad_chunks = pad // S5_CHUNK
    tr = _tile(n_chunks, 344)
    st_w = 2 * S5_OCT * S5_STATE
    kw = dict(tr=tr, chunks_per_batch=chunks_per_batch, pad_chunks=pad_chunks)
    s_all = pl.pallas_call(
        functools.partial(_s5_state_kernel, **kw),
        grid=(N_OCT, n_chunks // tr),
        in_specs=[pl.BlockSpec((tr * S5_CHUNK, LANES), lambda o, i: (i, o)),
                  pl.BlockSpec((1, S5_CHUNK, LANES, st_w), lambda o, i: (o, 0, 0, 0))],
        out_specs=pl.BlockSpec((tr, st_w), lambda o, i: (i, o)),
        out_shape=jax.ShapeDtypeStruct((n_chunks, N_OCT * st_w), F32),
        compiler_params=_cparams("parallel", "parallel"),
        name="s5_state",
    )(u, p_tab)
    tc = _tile(chunks_per_batch, 129, 1)
    nt = chunks_per_batch // tc
    x_prev = pl.pallas_call(
        functools.partial(_s5_scan_kernel, tc=tc),
        grid=(n_batch, nt),
        in_specs=[pl.BlockSpec((tc, N_OCT, st_w), lambda b, i: (b * nt + i, 0, 0)),
                  pl.BlockSpec((N_OCT, st_w // 2), lambda b, i: (0, 0)),
                  pl.BlockSpec((N_OCT, st_w // 2), lambda b, i: (0, 0))],
        out_specs=pl.BlockSpec((tc, N_OCT, st_w), lambda b, i: (b * nt + i, 0, 0)),
        out_shape=jax.ShapeDtypeStruct((n_chunks, N_OCT, st_w), F32),
        scratch_shapes=[pltpu.VMEM((N_OCT, st_w), F32)],
        compiler_params=_cparams("arbitrary", "arbitrary"),
        name="s5_scan",
    )(s_all.reshape(n_chunks, N_OCT, st_w), a16_re, a16_im)
    z = pl.pallas_call(
        functools.partial(_s5_out_kernel, **kw),
        grid=(N_OCT, n_chunks // tr),
        in_specs=[pl.BlockSpec((tr * S5_CHUNK, LANES), lambda o, i: (i, o)),
                  pl.BlockSpec((tr, st_w), lambda o, i: (i, o)),
                  pl.BlockSpec((1, S5_CHUNK, LANES, S5_CHUNK * LANES), lambda o, i: (o, 0, 0, 0)),
                  pl.BlockSpec((1, st_w, S5_CHUNK * LANES), lambda o, i: (o, 0, 0))],
        out_specs=pl.BlockSpec((tr * S5_CHUNK, LANES), lambda o, i: (i, o)),
        out_shape=jax.ShapeDtypeStruct((tp, d), F32),
        scratch_shapes=[pltpu.VMEM((tr, S5_CHUNK * LANES), F32)],
        compiler_params=_cparams("parallel", "parallel"),
        name="s5_out",
    )(u, x_prev.reshape(n_chunks, N_OCT * st_w), m_tab, r_tab)
    row = lambda i: (i, 0)
    fix = lambda i: (0, 0)
    return pl.pallas_call(
        _s5_tail_kernel,
        grid=(tp // tm,),
        in_specs=[pl.BlockSpec((tm, d), row), pl.BlockSpec((d, d), fix), pl.BlockSpec((1, d), fix),
                  pl.BlockSpec((d, d), fix), pl.BlockSpec((tm, d), row),
                  pl.BlockSpec((1, d), fix), pl.BlockSpec((1, d), fix)],
        out_specs=pl.BlockSpec((tm, d), row),
        out_shape=jax.ShapeDtypeStruct((tp, d), F32),
        compiler_params=_cparams("parallel"),
        name="s5_tail",
    )(z, w_glu.astype(BF16), b_glu.reshape(1, d).astype(F32), w_out.astype(BF16), h,
      ln_g.reshape(1, d), ln_b.reshape(1, d))


def _ret_kernel(q_ref, k_ref, v_ref, gate_ref, cos_ref, sin_ref, dm_ref, qd_ref, kd_ref, cd_ref, g_ref,
                o_ref, st_ref, *, chunk, pad):
    c = pl.program_id(2)

    @pl.when(c == 0)
    def _():
        st_ref[...] = jnp.zeros(st_ref.shape, F32)

    cos = cos_ref[...]
    sin = sin_ref[...]
    half = RET_DK // 2

    def rot(t):
        t1 = t[:, :half]
        t2 = t[:, half:]
        return jnp.concatenate([t1 * cos - t2 * sin, t1 * sin + t2 * cos], axis=-1)

    q = rot(q_ref[...])
    k = rot(k_ref[...]) * (RET_DK ** -0.5)
    pos = c * chunk + lax.broadcasted_iota(jnp.int32, (chunk, 1), 0)
    k = jnp.where(pos >= pad, k, 0.0)
    qb = q.astype(BF16)
    vb = v_ref[...].astype(BF16)
    scores = lax.dot_general(qb, k.astype(BF16), (((1,), (1,)), ((), ())),
                             preferred_element_type=F32) * dm_ref[0]
    intra = jnp.dot(scores.astype(BF16), vb, preferred_element_type=F32)
    state = st_ref[...]
    inter = jnp.dot(qb, state.astype(BF16), preferred_element_type=F32) * qd_ref[0]
    st_ref[...] = state * cd_ref[0] + lax.dot_general((k * kd_ref[0]).astype(BF16), vb,
                                                      (((0,), (0,)), ((), ())),
                                                      preferred_element_type=F32)
    o = intra + inter
    mu = jnp.mean(o, axis=-1, keepdims=True)
    oc = o - mu
    var = jnp.mean(oc * oc, axis=-1, keepdims=True)
    o = oc * lax.rsqrt(var + LN_EPS) * g_ref[...]
    gate = gate_ref[...]
    o_ref[...] = (gate * jax.nn.sigmoid(gate) * o).astype(o_ref.dtype)


def _retention_mixer(h, lp, pad, w_in, gn_g, w_out, ln_g, ln_b, tm):
    tp, d = h.shape
    n_batch = tp // lp
    chunk = _tile(lp, 384, SEQ_ALIGN)
    ncb = lp // chunk
    qd = RET_HEADS * RET_DK
    vd = RET_HEADS * RET_DV
    proj = _matmul(h, w_in.astype(BF16), F32, tm, _tile(w_in.shape[1], 1536, LANES))
    half = RET_DK // 2
    pos = jnp.arange(lp, dtype=F32) - pad
    inv_freq = jnp.power(RET_THETA, -jnp.arange(half, dtype=F32) / half)
    ang = pos[:, None] * inv_freq[None, :]
    cos, sin = jnp.cos(ang), jnp.sin(ang)
    log_gamma = jnp.log1p(-jnp.power(2.0, -5.0 - jnp.arange(RET_HEADS, dtype=F32)))
    idx = jnp.arange(chunk, dtype=F32)
    rel = idx[:, None] - idx[None, :]
    dmask = jnp.where(rel[None] >= 0, jnp.exp(log_gamma[:, None, None] * jnp.maximum(rel, 0.0)[None]), 0.0)
    q_decay = jnp.exp(log_gamma[:, None] * (idx[None, :] + 1.0))[:, :, None]
    k_decay = jnp.exp(log_gamma[:, None] * (chunk - 1.0 - idx[None, :]))[:, :, None]
    chunk_decay = jnp.exp(log_gamma * chunk).reshape(RET_HEADS, 1, 1)
    kb = qd // RET_DK
    vb = 2 * qd // RET_DV
    gb = (2 * qd + vd) // RET_DV
    o = pl.pallas_call(
        functools.partial(_ret_kernel, chunk=chunk, pad=pad),
        grid=(n_batch, RET_HEADS, ncb),
        in_specs=[pl.BlockSpec((chunk, RET_DK), lambda b, hh, c: (b * ncb + c, hh)),
                  pl.BlockSpec((chunk, RET_DK), lambda b, hh, c: (b * ncb + c, kb + hh)),
                  pl.BlockSpec((chunk, RET_DV), lambda b, hh, c: (b * ncb + c, vb + hh)),
                  pl.BlockSpec((chunk, RET_DV), lambda b, hh, c: (b * ncb + c, gb + hh)),
                  pl.BlockSpec((chunk, half), lambda b, hh, c: (c, 0)),
                  pl.BlockSpec((chunk, half), lambda b, hh, c: (c, 0)),
                  pl.BlockSpec((1, chunk, chunk), lambda b, hh, c: (hh, 0, 0)),
                  pl.BlockSpec((1, chunk, 1), lambda b, hh, c: (hh, 0, 0)),
                  pl.BlockSpec((1, chunk, 1), lambda b, hh, c: (hh, 0, 0)),
                  pl.BlockSpec((1, 1, 1), lambda b, hh, c: (hh, 0, 0)),
                  pl.BlockSpec((1, RET_DV), lambda b, hh, c: (0, hh))],
        out_specs=pl.BlockSpec((chunk, RET_DV), lambda b, hh, c: (b * ncb + c, hh)),
        out_shape=jax.ShapeDtypeStruct((tp, vd), BF16),
        scratch_shapes=[pltpu.VMEM((RET_DK, RET_DV), F32)],
        compiler_params=_cparams("parallel", "parallel", "arbitrary"),
        name="retention",
    )(proj, proj, proj, proj, cos, sin, dmask, q_decay, k_decay, chunk_decay,
      gn_g.reshape(1, vd).astype(F32))
    return _matmul_res_ln(o, w_out.astype(BF16), h, ln_g, ln_b, tm)


ONES_ROWS = 16


def _diff_kernel(q_ref, k_ref, vt_ref, lam_ref, g_ref, o_ref, vext_ref, m_ref, acc_ref, s0_ref, s1_ref,
                 s2_ref, s3_ref, mx_ref,
                 *, tq, pad, lambda_init):
    i = pl.program_id(2)
    nkb = vext_ref.shape[0]

    @pl.when(i == 0)
    def _():
        for jb in range(nkb):
            vext_ref[jb, 0:DIFF_DV, :] = vt_ref[:, jb * tq:(jb + 1) * tq]
            vext_ref[jb, DIFF_DV:, :] = jnp.ones((ONES_ROWS, tq), BF16)

    q = q_ref[...]
    lane = lax.broadcasted_iota(jnp.int32, q.shape, 1)
    zero = jnp.zeros_like(q)
    q_both = jnp.concatenate([jnp.where(lane < DIFF_DQK, q, zero), jnp.where(lane >= DIFF_DQK, q, zero)],
                             axis=0)
    m_ref[...] = jnp.full(m_ref.shape, MASK_VALUE, F32)
    acc_ref[...] = jnp.zeros(acc_ref.shape, F32)

    def scores(j):
        start = j * tq if isinstance(j, int) else pl.multiple_of(j * tq, tq)
        k = k_ref[pl.ds(start, tq), :]
        return lax.dot_general(k, q_both, (((1,), (1,)), ((), ())), preferred_element_type=F32)

    s_refs = (s0_ref, s1_ref, s2_ref, s3_ref)

    def produce(slot, j):
        s = scores(j)
        s_refs[slot][...] = s
        mx_ref[slot] = jnp.max(s, axis=0, keepdims=True)

    def accumulate(m_new, pe, v):
        m_prev = m_ref[...]
        acc_ref[...] = jnp.exp2(m_prev - m_new) * acc_ref[...] + jnp.dot(v, pe, preferred_element_type=F32)
        m_ref[...] = m_new

    def consume_masked(s, j):
        kpos = j * tq + lax.broadcasted_iota(jnp.int32, (tq, tq), 0)
        qpos = i * tq + lax.broadcasted_iota(jnp.int32, (tq, tq), 1)
        allowed = (kpos <= qpos) & (kpos >= pad)
        s = jnp.where(jnp.concatenate([allowed, allowed], axis=1), s, MASK_VALUE)
        m_new = jnp.maximum(m_ref[...], jnp.max(s, axis=0, keepdims=True))
        accumulate(m_new, jnp.exp2((s - m_new).astype(BF16)), vext_ref[j])

    def consume(slot, j):
        m_new = jnp.maximum(m_ref[...], mx_ref[slot])
        accumulate(m_new, jnp.exp2((s_refs[slot][...] - m_new).astype(BF16)), vext_ref[j])

    def consume2(slot_a, slot_b, j):
        m_new = jnp.maximum(m_ref[...], jnp.maximum(mx_ref[slot_a], mx_ref[slot_b]))
        pe = jnp.concatenate([jnp.exp2((s_refs[slot_a][...] - m_new).astype(BF16)),
                              jnp.exp2((s_refs[slot_b][...] - m_new).astype(BF16))], axis=0)
        accumulate(m_new, pe, jnp.concatenate([vext_ref[j], vext_ref[j + 1]], axis=1))

    @pl.when(i == 0)
    def _():
        consume_masked(scores(0), 0)

    @pl.when(i > 0)
    def _():
        n = i - 1
        n_quads = n // 4
        s3_ref[...] = scores(0)
        produce(0, 1)
        produce(1, jnp.minimum(2, i))
        consume_masked(s3_ref[...], 0)

        def quad(u, carry):
            a = 1 + 4 * u
            produce(2, a + 2)
            produce(3, a + 3)
            consume2(0, 1, a)
            produce(0, jnp.minimum(a + 4, i))
            produce(1, jnp.minimum(a + 5, i))
            consume2(2, 3, a + 2)
            return carry

        lax.fori_loop(0, n_quads, quad, 0)
        first = 1 + 4 * n_quads

        def pair(jj, carry):
            a = first + 2 * jj
            produce(1, a + 1)
            consume(0, a)
            produce(0, a + 2)
            consume(1, a + 1)
            return carry

        lax.fori_loop(0, (n - 4 * n_quads) // 2, pair, 0)

        @pl.when(n % 2 == 1)
        def _():
            produce(1, i)
            consume(0, i - 1)
            consume_masked(s1_ref[...], i)

        @pl.when(n % 2 == 0)
        def _():
            consume_masked(s0_ref[...], i)

    lam_p = lam_ref[...]
    lam = (jnp.exp(jnp.sum(lam_p[0:1] * lam_p[1:2], axis=-1, keepdims=True))
           - jnp.exp(jnp.sum(lam_p[2:3] * lam_p[3:4], axis=-1, keepdims=True)) + lambda_init)
    acc = acc_ref[...]
    o_both = acc[0:DIFF_DV] / acc[DIFF_DV:DIFF_DV + 1]
    o = o_both[:, :tq] - lam * o_both[:, tq:]
    o = o * lax.rsqrt(jnp.mean(o * o, axis=0, keepdims=True) + LN_EPS)
    o = o * (g_ref[...] * (1.0 - lambda_init))
    o_ref[...] = o.T.astype(o_ref.dtype)


def _mm_nt_kernel(wt_ref, x_ref, o_ref):
    o_ref[...] = lax.dot_general(wt_ref[...], x_ref[...].astype(BF16), (((1,), (1,)), ((), ())),
                                 preferred_element_type=F32).astype(o_ref.dtype)


def _matmul_nt(wt, x, out_dtype, tm):
    n, k = wt.shape
    m = x.shape[0]
    return pl.pallas_call(
        _mm_nt_kernel,
        grid=(m // tm,),
        in_specs=[pl.BlockSpec((n, k), lambda i: (0, 0)), pl.BlockSpec((tm, k), lambda i: (i, 0))],
        out_specs=pl.BlockSpec((n, tm), lambda i: (0, i)),
        out_shape=jax.ShapeDtypeStruct((n, m), out_dtype),
        compiler_params=_cparams("parallel"),
        name="matmul_nt",
    )(wt, x)


def _diff_attn_mixer(h, lp, pad, w_in, lq1, lk1, lq2, lk2, subln_g, w_out, lambda_init, ln_g, ln_b, tm):
    tp, d = h.shape
    n_batch = tp // lp
    qk = DIFF_HEADS * 2 * DIFF_DQK
    col_scale = jnp.concatenate([jnp.full((qk,), DIFF_DQK ** -0.5 * math.log2(math.e), F32),
                                 jnp.ones((qk,), F32)])
    proj = _matmul(h, (w_in[:, :2 * qk] * col_scale[None, :]).astype(BF16), BF16, tm, _tile(2 * qk, 2048, LANES))
    vt = _matmul_nt(w_in[:, 2 * qk:].T.astype(BF16), h, BF16, _tile(tp, 768, LANES))
    tq = _tile(lp, 384, SEQ_ALIGN)
    nq = lp // tq
    lam_p = jnp.stack([lq1, lk1, lq2, lk2]).astype(F32)
    o = pl.pallas_call(
        functools.partial(_diff_kernel, tq=tq, pad=pad, lambda_init=lambda_init),
        grid=(n_batch, DIFF_HEADS, nq),
        in_specs=[pl.BlockSpec((tq, LANES), lambda b, hh, i: (b * nq + i, hh)),
                  pl.BlockSpec((lp, LANES), lambda b, hh, i: (b, DIFF_HEADS + hh)),
                  pl.BlockSpec((DIFF_DV, lp), lambda b, hh, i: (hh, b)),
                  pl.BlockSpec((4, DIFF_DQK), lambda b, hh, i: (0, 0)),
                  pl.BlockSpec((DIFF_DV, 1), lambda b, hh, i: (0, 0))],
        out_specs=pl.BlockSpec((tq, DIFF_DV), lambda b, hh, i: (b * nq + i, hh)),
        out_shape=jax.ShapeDtypeStruct((tp, DIFF_HEADS * DIFF_DV), BF16),
        scratch_shapes=[pltpu.VMEM((nq, DIFF_DV + ONES_ROWS, tq), BF16),
                        pltpu.VMEM((1, 2 * tq), F32),
                        pltpu.VMEM((DIFF_DV + ONES_ROWS, 2 * tq), F32),
                        ] + [pltpu.VMEM((tq, 2 * tq), F32)] * 4 + [pltpu.VMEM((4, 1, 2 * tq), F32)],
        compiler_params=_cparams("parallel", "parallel", "arbitrary"),
        name="diff_attn",
    )(proj, proj, vt, lam_p, subln_g.reshape(DIFF_DV, 1).astype(F32))
    return _matmul_res_ln(o, w_out.astype(BF16), h, ln_g, ln_b, tm)


SORT_ALIGN = 16
SORT_CHUNK = 512


def _router_kernel(x_ref, w_ref, b_ref, idx_ref, gate_ref, rank_ref, cnt_ref, *, tm):
    logits = jnp.dot(x_ref[...], w_ref[...], preferred_element_type=F32,
                     precision=lax.Precision.HIGHEST) + b_ref[...]
    lane = lax.broadcasted_iota(jnp.int32, logits.shape, 1).astype(F32)
    work = jnp.where(lane < N_EXPERTS, logits, -jnp.inf)
    vals, idxs = [], []
    picked = jnp.zeros(logits.shape, F32)
    for _ in range(TOP_K):
        m = jnp.max(work, axis=-1, keepdims=True)
        sel = jnp.min(jnp.where(work == m, lane, float(LANES)), axis=-1, keepdims=True)
        hit = lane == sel
        work = jnp.where(hit, -jnp.inf, work)
        picked = jnp.where(hit, 1.0, picked)
        vals.append(m)
        idxs.append(sel)
    exps = [jnp.exp(v - vals[0]) for v in vals]
    tot = exps[0] + exps[1] + exps[2] + exps[3]
    r = lax.broadcasted_iota(jnp.int32, (tm, tm), 0)
    c = lax.broadcasted_iota(jnp.int32, (tm, tm), 1)
    lower = jnp.where(c < r, 1.0, 0.0).astype(BF16)
    before = jnp.dot(lower, picked.astype(BF16), preferred_element_type=F32)
    for kk in range(TOP_K):
        idx_ref[:, kk:kk + 1] = idxs[kk].astype(jnp.int32)
        gate_ref[:, kk:kk + 1] = exps[kk] / tot
        rank_ref[:, kk:kk + 1] = jnp.sum(jnp.where(lane == idxs[kk], before, 0.0), axis=-1,
                                         keepdims=True).astype(jnp.int32)
    cnt_ref[0] = jnp.sum(picked, axis=0, keepdims=True)


def _segment_dma(tab_ref, tile, local_ref, slots_ref, sem, to_slots, wait):
    n_seg = tab_ref.shape[0] // 3

    def copy(e, c):
        lo = pl.multiple_of(tab_ref[tile * N_EXPERTS + e] + c * SORT_ALIGN, SORT_ALIGN)
        go = pl.multiple_of(tab_ref[n_seg + tile * N_EXPERTS + e] + c * SORT_ALIGN, SORT_ALIGN)
        local = local_ref.at[pl.ds(lo, SORT_ALIGN), :]
        slots = slots_ref.at[pl.ds(go, SORT_ALIGN), :]
        return pltpu.make_async_copy(local, slots, sem) if to_slots else pltpu.make_async_copy(slots, local, sem)

    def per_expert(e, carry):
        def per_chunk(c, carry2):
            if wait:
                copy(e, c).wait()
            else:
                copy(e, c).start()
            return carry2
        return lax.fori_loop(0, tab_ref[2 * n_seg + tile * N_EXPERTS + e], per_chunk, carry)
    lax.fori_loop(0, N_EXPERTS, per_expert, 0)


def _dispatch_kernel(tab_ref, x_ref, lrow_ref, slots_in_ref, slots_ref, sorted_ref, sem, *, tm, rs):
    del slots_in_ref
    i = pl.program_id(0)
    slot = i % 2
    buf = sorted_ref.at[slot]
    xb = x_ref[...].astype(BF16)
    lr = lrow_ref[...]
    for rc in range(rs // SORT_CHUNK):
        rows = rc * SORT_CHUNK + lax.broadcasted_iota(jnp.int32, (SORT_CHUNK, tm), 0)
        hit = (rows == lr[0:1]) | (rows == lr[1:2]) | (rows == lr[2:3]) | (rows == lr[3:4])
        perm = jnp.where(hit, 1.0, 0.0).astype(BF16)
        buf[rc * SORT_CHUNK:(rc + 1) * SORT_CHUNK, :] = jnp.dot(
            perm, xb, preferred_element_type=F32).astype(BF16)
    _segment_dma(tab_ref, i, buf, slots_ref, sem.at[slot], True, wait=False)

    @pl.when(i > 0)
    def _():
        _segment_dma(tab_ref, i - 1, sorted_ref.at[1 - slot], slots_ref, sem.at[1 - slot], True, wait=True)

    @pl.when(i == pl.num_programs(0) - 1)
    def _():
        _segment_dma(tab_ref, i, buf, slots_ref, sem.at[slot], True, wait=True)


def _combine_kernel(tab_ref, res_ref, lrow_ref, gate_ref, g_ref, b_ref, y_ref, o_ref, ysort_ref, sem,
                    *, tm, rs):
    i = pl.program_id(0)
    slot = i % 2

    @pl.when(i == 0)
    def _():
        ysort_ref[...] = jnp.zeros(ysort_ref.shape, BF16)
        _segment_dma(tab_ref, 0, ysort_ref.at[0], y_ref, sem.at[0], False, wait=False)

    @pl.when(i + 1 < pl.num_programs(0))
    def _():
        _segment_dma(tab_ref, i + 1, ysort_ref.at[1 - slot], y_ref, sem.at[1 - slot], False, wait=False)

    _segment_dma(tab_ref, i, ysort_ref.at[slot], y_ref, sem.at[slot], False, wait=True)
    ysort = ysort_ref.at[slot]
    lr = lrow_ref[...]
    gt = gate_ref[...]
    y = jnp.zeros(res_ref.shape, F32)
    for rc in range(rs // SORT_CHUNK):
        cols = rc * SORT_CHUNK + lax.broadcasted_iota(jnp.int32, (tm, SORT_CHUNK), 1)
        w = jnp.zeros((tm, SORT_CHUNK), F32)
        for kk in range(TOP_K):
            w = jnp.where(cols == lr[:, kk:kk + 1], gt[:, kk:kk + 1], w)
        y += jnp.dot(w.astype(BF16), ysort[rc * SORT_CHUNK:(rc + 1) * SORT_CHUNK, :],
                     preferred_element_type=F32)
    o_ref[...] = _layer_norm_rows(DEEPNORM_ALPHA * res_ref[...] + y, g_ref[...], b_ref[...])


def _expert_kernel(be_ref, nu_ref, x_ref, wu_ref, bu_ref, wd_ref, bd_ref, o_ref, wu_bf_ref, wd_bf_ref):
    i = pl.program_id(0)
    live = i < nu_ref[0]

    @pl.when(live & ((i == 0) | (be_ref[i] != be_ref[jnp.maximum(i - 1, 0)])))
    def _():
        wu_bf_ref[...] = wu_ref[0, 0].astype(BF16)
        wd_bf_ref[...] = wd_ref[0, 0].astype(BF16)

    @pl.when(live)
    def _():
        hb = jnp.dot(x_ref[...], wu_bf_ref[...], preferred_element_type=F32) + bu_ref[0]
        x_glu = jnp.minimum(hb[:, :D_EXPERT], SWIGLU_LIMIT)
        x_lin = jnp.clip(hb[:, D_EXPERT:], -SWIGLU_LIMIT, SWIGLU_LIMIT)
        act = x_glu * jax.nn.sigmoid(SWIGLU_ALPHA * x_glu) * (x_lin + 1.0)
        y = jnp.dot(act.astype(BF16), wd_bf_ref[...], preferred_element_type=F32) + bd_ref[0]
        o_ref[...] = y.astype(o_ref.dtype)

    @pl.when(jnp.logical_not(live))
    def _():
        o_ref[...] = jnp.zeros(o_ref.shape, o_ref.dtype)


def _moe_ffn(h, layer, w_router, b_router, w_up_all, b_up, w_down_all, b_down, ln_g, ln_b, tm, bm):
    tp, d = h.shape
    a = tp * TOP_K
    wr = jnp.zeros((d, LANES), F32).at[:, :N_EXPERTS].set(w_router.astype(F32))
    br = jnp.zeros((1, LANES), F32).at[0, :N_EXPERTS].set(b_router.astype(F32))
    row = lambda i: (i, 0)
    fix = lambda i: (0, 0)
    nt = tp // tm
    idx, gates, rank, cnt = pl.pallas_call(
        functools.partial(_router_kernel, tm=tm),
        grid=(nt,),
        in_specs=[pl.BlockSpec((tm, d), row), pl.BlockSpec((d, LANES), fix), pl.BlockSpec((1, LANES), fix)],
        out_specs=[pl.BlockSpec((tm, TOP_K), row), pl.BlockSpec((tm, TOP_K), row),
                   pl.BlockSpec((tm, TOP_K), row), pl.BlockSpec((1, 1, LANES), lambda i: (i, 0, 0))],
        out_shape=[jax.ShapeDtypeStruct((tp, TOP_K), jnp.int32), jax.ShapeDtypeStruct((tp, TOP_K), F32),
                   jax.ShapeDtypeStruct((tp, TOP_K), jnp.int32), jax.ShapeDtypeStruct((nt, 1, LANES), F32)],
        compiler_params=_cparams("parallel"),
        name="router",
    )(h, wr, br)
    counts = cnt[:, 0, :N_EXPERTS].astype(jnp.int32)
    seg = ((counts + SORT_ALIGN - 1) // SORT_ALIGN) * SORT_ALIGN
    tile_off = jnp.cumsum(seg, axis=0) - seg
    region = ((jnp.sum(seg, axis=0) + bm - 1) // bm) * bm
    region_end = jnp.cumsum(region)
    slot_off = (region_end - region)[None, :] + tile_off
    local_off = jnp.cumsum(seg, axis=1) - seg
    table = jnp.concatenate([local_off.reshape(-1), slot_off.reshape(-1),
                             (seg // SORT_ALIGN).reshape(-1)]).astype(jnp.int32)
    onehot = idx.reshape(nt, tm, TOP_K, 1) == jnp.arange(N_EXPERTS, dtype=jnp.int32)
    lrow = jnp.sum(jnp.where(onehot, local_off[:, None, None, :], 0), axis=-1).reshape(tp, TOP_K) + rank
    lrow_t = jnp.concatenate([lrow.T, jnp.full((8 - TOP_K, tp), -1, jnp.int32)], axis=0)
    rs = -(-(TOP_K * tm + N_EXPERTS * (SORT_ALIGN - 1)) // SORT_CHUNK) * SORT_CHUNK
    n_blocks = -(-(a + nt * N_EXPERTS * (SORT_ALIGN - 1)) // bm) + N_EXPERTS
    n_slots = n_blocks * bm
    block_start = jnp.arange(n_blocks, dtype=jnp.int32) * bm
    block_e = jnp.minimum(jnp.sum((region_end[None, :] <= block_start[:, None]).astype(jnp.int32), axis=1),
                          N_EXPERTS - 1)
    n_used = (region_end[-1:] // bm).astype(jnp.int32)
    x_slots = pl.pallas_call(
        functools.partial(_dispatch_kernel, tm=tm, rs=rs),
        grid_spec=pltpu.PrefetchScalarGridSpec(
            num_scalar_prefetch=1,
            grid=(nt,),
            in_specs=[pl.BlockSpec((tm, d), lambda i, tab: (i, 0)),
                      pl.BlockSpec((8, tm), lambda i, tab: (0, i)),
                      pl.BlockSpec(memory_space=pl.ANY)],
            out_specs=pl.BlockSpec(memory_space=pl.ANY),
            scratch_shapes=[pltpu.VMEM((2, rs, d), BF16), pltpu.SemaphoreType.DMA((2,))],
        ),
        out_shape=jax.ShapeDtypeStruct((n_slots, d), BF16),
        input_output_aliases={3: 0},
        compiler_params=_cparams("arbitrary"),
        name="moe_dispatch",
    )(table, h, lrow_t, jnp.zeros((n_slots, d), BF16))
    y_slots = pl.pallas_call(
        _expert_kernel,
        grid_spec=pltpu.PrefetchScalarGridSpec(
            num_scalar_prefetch=2,
            grid=(n_blocks,),
            in_specs=[pl.BlockSpec((bm, d), lambda i, be, nu: (i, 0)),
                      pl.BlockSpec((1, 1, d, 2 * D_EXPERT), lambda i, be, nu: (layer, be[i], 0, 0)),
                      pl.BlockSpec((1, 1, 2 * D_EXPERT), lambda i, be, nu: (be[i], 0, 0)),
                      pl.BlockSpec((1, 1, D_EXPERT, d), lambda i, be, nu: (layer, be[i], 0, 0)),
                      pl.BlockSpec((1, 1, d), lambda i, be, nu: (be[i], 0, 0))],
            out_specs=pl.BlockSpec((bm, d), lambda i, be, nu: (i, 0)),
            scratch_shapes=[pltpu.VMEM((d, 2 * D_EXPERT), BF16), pltpu.VMEM((D_EXPERT, d), BF16)],
        ),
        out_shape=jax.ShapeDtypeStruct((n_slots, d), BF16),
        compiler_params=_cparams("arbitrary"),
        name="experts",
    )(block_e, n_used, x_slots, w_up_all, b_up.reshape(N_EXPERTS, 1, 2 * D_EXPERT).astype(F32),
      w_down_all, b_down.reshape(N_EXPERTS, 1, d).astype(F32))
    return pl.pallas_call(
        functools.partial(_combine_kernel, tm=tm, rs=rs),
        grid_spec=pltpu.PrefetchScalarGridSpec(
            num_scalar_prefetch=1,
            grid=(nt,),
            in_specs=[pl.BlockSpec((tm, d), lambda i, tab: (i, 0)),
                      pl.BlockSpec((tm, TOP_K), lambda i, tab: (i, 0)),
                      pl.BlockSpec((tm, TOP_K), lambda i, tab: (i, 0)),
                      pl.BlockSpec((1, d), lambda i, tab: (0, 0)),
                      pl.BlockSpec((1, d), lambda i, tab: (0, 0)),
                      pl.BlockSpec(memory_space=pl.ANY)],
            out_specs=pl.BlockSpec((tm, d), lambda i, tab: (i, 0)),
            scratch_shapes=[pltpu.VMEM((2, rs, d), BF16), pltpu.SemaphoreType.DMA((2,))],
        ),
        out_shape=jax.ShapeDtypeStruct((tp, d), F32),
        compiler_params=_cparams("arbitrary"),
        name="moe_combine_ln",
    )(table, h, lrow, gates, ln_g.reshape(1, d), ln_b.reshape(1, d), y_slots)


def kernel(x, meta_tokens, s5_w_in, s5_lambda_re, s5_lambda_im, s5_log_step, s5_b_re, s5_b_im, s5_c_re, s5_c_im, s5_d, s5_w_glu, s5_b_glu, s5_w_out, ret_w_in, ret_gn_g, ret_w_out, diff_w_in, diff_lambda_q1, diff_lambda_k1, diff_lambda_q2, diff_lambda_k2, diff_subln_g, diff_w_out, ln_mix_g, ln_mix_b, moe_w_router, moe_b_router, moe_w_up, moe_b_up, moe_w_down, moe_b_down, ln_ffn_g, ln_ffn_b):
    n_batch, seq, d = x.shape
    length = seq + N_META
    pad = (-length) % SEQ_ALIGN
    lp = length + pad
    assert pad % S5_CHUNK == 0 and d == D_MODEL
    tp = n_batch * lp
    tm = _tile(tp, 768)
    meta = jnp.broadcast_to(meta_tokens[None].astype(x.dtype), (n_batch, N_META, d))
    h = jnp.concatenate([jnp.zeros((n_batch, pad, d), x.dtype), meta, x], axis=1).reshape(tp, d)
    for i in range(DEPTH):
        kind = i % N_MIXERS
        j = i // N_MIXERS
        if kind == 0:
            h = _s5_mixer(h, lp, pad, s5_w_in[j], s5_lambda_re[j], s5_lambda_im[j], s5_log_step[j],
                          s5_b_re[j], s5_b_im[j], s5_c_re[j], s5_c_im[j], s5_d[j],
                          s5_w_glu[j], s5_b_glu[j], s5_w_out[j], ln_mix_g[i], ln_mix_b[i], tm)
        elif kind == 1:
            h = _retention_mixer(h, lp, pad, ret_w_in[j], ret_gn_g[j], ret_w_out[j],
                                 ln_mix_g[i], ln_mix_b[i], tm)
        else:
            lambda_init = 0.8 - 0.6 * math.exp(-0.3 * i)
            h = _diff_attn_mixer(h, lp, pad, diff_w_in[j], diff_lambda_q1[j], diff_lambda_k1[j],
                                 diff_lambda_q2[j], diff_lambda_k2[j], diff_subln_g[j], diff_w_out[j],
                                 lambda_init, ln_mix_g[i], ln_mix_b[i], tm)
        h = _moe_ffn(h, i, moe_w_router[i], moe_b_router[i], moe_w_up, moe_b_up[i],
                     moe_w_down, moe_b_down[i], ln_ffn_g[i], ln_ffn_b[i], tm, MOE_BLOCK)
    return h.reshape(n_batch, lp, d)[:, pad + N_META:, :]
```

```python
import functools
import math

import jax
import jax.numpy as jnp
from jax import lax
from jax.experimental import pallas as pl
from jax.experimental.pallas import tpu as pltpu

F32 = jnp.float32
BF16 = jnp.bfloat16

D_MODEL = 1024
DEPTH = 4
N_META = 16
N_MIXERS = 3
S5_GROUP_CH = 16
S5_GROUPS = D_MODEL // S5_GROUP_CH
S5_STATE = 64
S5_CHUNK = 16
S5_OCT = 8
N_OCT = S5_GROUPS // S5_OCT
RET_HEADS = 4
RET_DK = D_MODEL // RET_HEADS
RET_DV = 2 * RET_DK
RET_THETA = 10000.0
DIFF_HEADS = D_MODEL // 128
DIFF_DQK = 64
DIFF_DV = 128
N_EXPERTS = 32
TOP_K = 4
D_EXPERT = D_MODEL
SWIGLU_LIMIT = 7.0
SWIGLU_ALPHA = 1.702
MOE_BLOCK = 512
LN_EPS = 1e-5
MASK_VALUE = -1e30
DEEPNORM_ALPHA = (2 * DEPTH) ** 0.25
SEQ_ALIGN = 128
LANES = 128
VMEM_LIMIT = 56 * 1024 * 1024


def _cparams(*sem):
    return pltpu.CompilerParams(dimension_semantics=sem, vmem_limit_bytes=VMEM_LIMIT)


def _tile(n, target, mult=8):
    best = None
    for t in range(mult, min(n, target) + 1, mult):
        if n % t == 0:
            best = t
    assert best is not None, (n, target, mult)
    return best


def _layer_norm_rows(v, g, b):
    mu = jnp.mean(v, axis=-1, keepdims=True)
    c = v - mu
    var = jnp.mean(c * c, axis=-1, keepdims=True)
    return c * lax.rsqrt(var + LN_EPS) * g + b


def _mm_kernel(x_ref, w_ref, o_ref):
    o_ref[...] = jnp.dot(x_ref[...].astype(BF16), w_ref[...],
                         preferred_element_type=F32).astype(o_ref.dtype)


def _matmul(x, w, out_dtype, tm, tn):
    m, k = x.shape
    n = w.shape[1]
    return pl.pallas_call(
        _mm_kernel,
        grid=(n // tn, m // tm),
        in_specs=[pl.BlockSpec((tm, k), lambda j, i: (i, 0)),
                  pl.BlockSpec((k, tn), lambda j, i: (0, j))],
        out_specs=pl.BlockSpec((tm, tn), lambda j, i: (i, j)),
        out_shape=jax.ShapeDtypeStruct((m, n), out_dtype),
        compiler_params=_cparams("parallel", "parallel"),
        name="matmul",
    )(x, w)


def _mm_res_ln_kernel(x_ref, w_ref, res_ref, g_ref, b_ref, o_ref):
    y = jnp.dot(x_ref[...].astype(BF16), w_ref[...], preferred_element_type=F32)
    o_ref[...] = _layer_norm_rows(DEEPNORM_ALPHA * res_ref[...] + y, g_ref[...], b_ref[...])


def _matmul_res_ln(x, w, res, g, b, tm):
    m, k = x.shape
    n = w.shape[1]
    row = lambda i: (i, 0)
    fix = lambda i: (0, 0)
    return pl.pallas_call(
        _mm_res_ln_kernel,
        grid=(m // tm,),
        in_specs=[pl.BlockSpec((tm, k), row), pl.BlockSpec((k, n), fix), pl.BlockSpec((tm, n), row),
                  pl.BlockSpec((1, n), fix), pl.BlockSpec((1, n), fix)],
        out_specs=pl.BlockSpec((tm, n), row),
        out_shape=jax.ShapeDtypeStruct((m, n), F32),
        compiler_params=_cparams("parallel"),
        name="matmul_res_ln",
    )(x, w, res, g.reshape(1, n), b.reshape(1, n))


def _expand_block_diag(a, row_inner, outer, inner):
    rows, k = a.shape
    cols = outer * S5_OCT * inner
    r = jnp.arange(k)[:, None]
    c = jnp.arange(cols)[None, :]
    rep = ((r // inner == c // (S5_OCT * inner)) & (r % inner == c % inner)).astype(BF16)
    out = jnp.dot(a.astype(BF16), rep, preferred_element_type=BF16)
    g_row = (jnp.arange(rows)[:, None] // row_inner) % S5_OCT
    h_col = (c // inner) % S5_OCT
    return jnp.where(g_row == h_col, out, jnp.zeros_like(out))


def _s5_tables(lam_re, lam_im, log_step, b_re, b_im, c_re, c_im, d_skip):
    hp = lax.Precision.HIGHEST
    g_, n_, c_, q_, o_ = S5_GROUPS, S5_STATE, S5_GROUP_CH, S5_CHUNK, S5_OCT
    dt = jnp.exp(log_step.astype(F32))[:, None]
    lr = jnp.minimum(lam_re.astype(F32), -1e-4)
    li = lam_im.astype(F32)
    mag = jnp.exp(lr * dt)
    ab_re = mag * jnp.cos(li * dt)
    ab_im = mag * jnp.sin(li * dt)
    den = lr * lr + li * li
    f_re = ((ab_re - 1.0) * lr + ab_im * li) / den
    f_im = (ab_im * lr - (ab_re - 1.0) * li) / den
    br, bi = b_re.astype(F32), b_im.astype(F32)
    bbar_re = f_re[..., None] * br - f_im[..., None] * bi
    bbar_im = f_re[..., None] * bi + f_im[..., None] * br
    j = jnp.arange(q_ + 1, dtype=F32)[:, None, None]
    pmag = jnp.exp(j * (lr * dt)[None])
    pw_re = pmag * jnp.cos(j * (li * dt)[None])
    pw_im = pmag * jnp.sin(j * (li * dt)[None])
    cr, ci = c_re.astype(F32), c_im.astype(F32)
    w_re = pw_re[..., None] * bbar_re[None] - pw_im[..., None] * bbar_im[None]
    w_im = pw_re[..., None] * bbar_im[None] + pw_im[..., None] * bbar_re[None]
    kj = (jnp.einsum('gon,jgni->jgoi', cr, w_re[:q_], precision=hp)
          - jnp.einsum('gon,jgni->jgoi', ci, w_im[:q_], precision=hp))
    kj = kj.at[0].add(d_skip.astype(F32).reshape(g_, c_)[:, :, None] * jnp.eye(c_, dtype=F32)[None])
    s_idx = jnp.arange(q_)[:, None]
    t_idx = jnp.arange(q_)[None, :]
    lag = t_idx - s_idx
    kts = jnp.where((lag >= 0)[:, :, None, None, None], kj[jnp.clip(lag, 0, q_ - 1)], 0.0)
    kts = kts.reshape(q_, q_, N_OCT, o_, c_, c_).transpose(2, 0, 3, 5, 1, 4)
    m_tab = _expand_block_diag(kts.reshape(N_OCT * q_ * o_ * c_, q_ * c_), c_, q_, c_)
    m_tab = m_tab.reshape(N_OCT, q_, o_ * c_, q_ * o_ * c_)
    pst = jnp.stack([w_re[:q_][::-1], w_im[:q_][::-1]], axis=0)
    pst = pst.reshape(2, q_, N_OCT, o_, n_, c_).transpose(2, 1, 3, 5, 0, 4)
    p_tab = _expand_block_diag(pst.reshape(N_OCT * q_ * o_ * c_, 2 * n_), c_, 2, n_)
    p_tab = p_tab.reshape(N_OCT, q_, o_ * c_, 2 * o_ * n_)
    ar = pw_re[1:].reshape(q_, N_OCT, o_, n_)
    ai = pw_im[1:].reshape(q_, N_OCT, o_, n_)
    cr_o = cr.reshape(N_OCT, o_, c_, n_)
    ci_o = ci.reshape(N_OCT, o_, c_, n_)
    r_re = cr_o[None] * ar[:, :, :, None, :] - ci_o[None] * ai[:, :, :, None, :]
    r_im = -(cr_o[None] * ai[:, :, :, None, :] + ci_o[None] * ar[:, :, :, None, :])
    rst = jnp.stack([r_re, r_im], axis=0).transpose(2, 0, 3, 5, 1, 4)
    r_tab = _expand_block_diag(rst.reshape(N_OCT * 2 * o_ * n_, q_ * c_), n_, q_, c_)
    r_tab = r_tab.reshape(N_OCT, 2 * o_ * n_, q_ * o_ * c_)
    a16_re = pw_re[q_].reshape(N_OCT, o_ * n_)
    a16_im = pw_im[q_].reshape(N_OCT, o_ * n_)
    return m_tab, p_tab, r_tab, a16_re, a16_im


def _s5_chunk_rows(u_ref, s, tr, valid):
    us = u_ref[pl.ds(s, tr, stride=S5_CHUNK), :]
    return jnp.where(valid, us, 0.0).astype(BF16)


def _s5_valid(tr, chunks_per_batch, pad_chunks):
    chunk = pl.program_id(1) * tr + lax.broadcasted_iota(jnp.int32, (tr, 1), 0)
    return (chunk % chunks_per_batch) >= pad_chunks


def _s5_state_kernel(u_ref, p_ref, s_ref, *, tr, chunks_per_batch, pad_chunks):
    valid = _s5_valid(tr, chunks_per_batch, pad_chunks)
    acc = jnp.zeros(s_ref.shape, F32)
    for s in range(S5_CHUNK):
        acc += jnp.dot(_s5_chunk_rows(u_ref, s, tr, valid), p_ref[0, s], preferred_element_type=F32)
    s_ref[...] = acc


def _s5_scan_kernel(s_ref, ar_ref, ai_ref, x_ref, st_ref, *, tc):
    half = st_ref.shape[1] // 2

    @pl.when(pl.program_id(1) == 0)
    def _():
        st_ref[...] = jnp.zeros(st_ref.shape, F32)

    ar = ar_ref[...]
    ai = ai_ref[...]

    def body(c, carry):
        xr, xi = carry
        x_ref[c, :, :half] = xr
        x_ref[c, :, half:] = xi
        s = s_ref[c]
        return ar * xr - ai * xi + s[:, :half], ar * xi + ai * xr + s[:, half:]

    xr, xi = lax.fori_loop(0, tc, body, (st_ref[:, :half], st_ref[:, half:]))
    st_ref[:, :half] = xr
    st_ref[:, half:] = xi


def _gelu_tanh(y):
    return 0.5 * y * (1.0 + jnp.tanh(math.sqrt(2.0 / math.pi) * (y + 0.044715 * (y * y * y))))


def _s5_out_kernel(u_ref, xp_ref, m_ref, r_ref, z_ref, acc_ref, *, tr, chunks_per_batch, pad_chunks):
    valid = _s5_valid(tr, chunks_per_batch, pad_chunks)
    acc_ref[...] = jnp.dot(xp_ref[...].astype(BF16), r_ref[0], preferred_element_type=F32)
    for s in range(S5_CHUNK):
        acc_ref[:, s * LANES:] += jnp.dot(_s5_chunk_rows(u_ref, s, tr, valid), m_ref[0, s, :, s * LANES:],
                                          preferred_element_type=F32)
    for t in range(S5_CHUNK):
        z_ref[pl.ds(t, tr, stride=S5_CHUNK), :] = _gelu_tanh(acc_ref[:, t * LANES:(t + 1) * LANES])


def _s5_tail_kernel(z_ref, wg_ref, bg_ref, wo_ref, res_ref, g_ref, b_ref, o_ref):
    z = z_ref[...]
    t = jnp.dot(z.astype(BF16), wg_ref[...], preferred_element_type=F32) + bg_ref[...]
    glu = z * jax.nn.sigmoid(t)
    y = jnp.dot(glu.astype(BF16), wo_ref[...], preferred_element_type=F32)
    o_ref[...] = _layer_norm_rows(DEEPNORM_ALPHA * res_ref[...] + y, g_ref[...], b_ref[...])


def _s5_mixer(h, lp, pad, w_in, lam_re, lam_im, log_step, b_re, b_im, c_re, c_im, d_skip,
              w_glu, b_glu, w_out, ln_g, ln_b, tm):
    tp, d = h.shape
    n_batch = tp // lp
    m_tab, p_tab, r_tab, a16_re, a16_im = _s5_tables(lam_re, lam_im, log_step, b_re, b_im,
                                                     c_re, c_im, d_skip)
    u = _matmul(h, w_in.astype(BF16), F32, tm, d)
    n_chunks = tp // S5_CHUNK
    chunks_per_batch = lp // S5_CHUNK
    pad_chunks = pad // S5_CHUNK
    tr = _tile(n_chunks, 344)
    st_w = 2 * S5_OCT * S5_STATE
    kw = dict(tr=tr, chunks_per_batch=chunks_per_batch, pad_chunks=pad_chunks)
    s_all = pl.pallas_call(
        functools.partial(_s5_state_kernel, **kw),
        grid=(N_OCT, n_chunks // tr),
        in_specs=[pl.BlockSpec((tr * S5_CHUNK, LANES), lambda o, i: (i, o)),
                  pl.BlockSpec((1, S5_CHUNK, LANES, st_w), lambda o, i: (o, 0, 0, 0))],
        out_specs=pl.BlockSpec((tr, st_w), lambda o, i: (i, o)),
        out_shape=jax.ShapeDtypeStruct((n_chunks, N_OCT * st_w), F32),
        compiler_params=_cparams("parallel", "parallel"),
        name="s5_state",
    )(u, p_tab)
    tc = _tile(chunks_per_batch, 129, 1)
    nt = chunks_per_batch // tc
    x_prev = pl.pallas_call(
        functools.partial(_s5_scan_kernel, tc=tc),
        grid=(n_batch, nt),
        in_specs=[pl.BlockSpec((tc, N_OCT, st_w), lambda b, i: (b * nt + i, 0, 0)),
                  pl.BlockSpec((N_OCT, st_w // 2), lambda b, i: (0, 0)),
                  pl.BlockSpec((N_OCT, st_w // 2), lambda b, i: (0, 0))],
        out_specs=pl.BlockSpec((tc, N_OCT, st_w), lambda b, i: (b * nt + i, 0, 0)),
        out_shape=jax.ShapeDtypeStruct((n_chunks, N_OCT, st_w), F32),
        scratch_shapes=[pltpu.VMEM((N_OCT, st_w), F32)],
        compiler_params=_cparams("arbitrary", "arbitrary"),
        name="s5_scan",
    )(s_all.reshape(n_chunks, N_OCT, st_w), a16_re, a16_im)
    z = pl.pallas_call(
        functools.partial(_s5_out_kernel, **kw),
        grid=(N_OCT, n_chunks // tr),
        in_specs=[pl.BlockSpec((tr * S5_CHUNK, LANES), lambda o, i: (i, o)),
                  pl.BlockSpec((tr, st_w), lambda o, i: (i, o)),
                  pl.BlockSpec((1, S5_CHUNK, LANES, S5_CHUNK * LANES), lambda o, i: (o, 0, 0, 0)),
                  pl.BlockSpec((1, st_w, S5_CHUNK * LANES), lambda o, i: (o, 0, 0))],
        out_specs=pl.BlockSpec((tr * S5_CHUNK, LANES), lambda o, i: (i, o)),
        out_shape=jax.ShapeDtypeStruct((tp, d), F32),
        scratch_shapes=[pltpu.VMEM((tr, S5_CHUNK * LANES), F32)],
        compiler_params=_cparams("parallel", "parallel"),
        name="s5_out",
    )(u, x_prev.reshape(n_chunks, N_OCT * st_w), m_tab, r_tab)
    row = lambda i: (i, 0)
    fix = lambda i: (0, 0)
    return pl.pallas_call(
        _s5_tail_kernel,
        grid=(tp // tm,),
        in_specs=[pl.BlockSpec((tm, d), row), pl.BlockSpec((d, d), fix), pl.BlockSpec((1, d), fix),
                  pl.BlockSpec((d, d), fix), pl.BlockSpec((tm, d), row),
                  pl.BlockSpec((1, d), fix), pl.BlockSpec((1, d), fix)],
        out_specs=pl.BlockSpec((tm, d), row),
        out_shape=jax.ShapeDtypeStruct((tp, d), F32),
        compiler_params=_cparams("parallel"),
        name="s5_tail",
    )(z, w_glu.astype(BF16), b_glu.reshape(1, d).astype(F32), w_out.astype(BF16), h,
      ln_g.reshape(1, d), ln_b.reshape(1, d))


def _ret_kernel(q_ref, k_ref, v_ref, gate_ref, cos_ref, sin_ref, dm_ref, qd_ref, kd_ref, cd_ref, g_ref,
                o_ref, st_ref, *, chunk, pad):
    c = pl.program_id(2)

    @pl.when(c == 0)
    def _():
        st_ref[...] = jnp.zeros(st_ref.shape, F32)

    cos = cos_ref[...]
    sin = sin_ref[...]
    half = RET_DK // 2

    def rot(t):
        t1 = t[:, :half]
        t2 = t[:, half:]
        return jnp.concatenate([t1 * cos - t2 * sin, t1 * sin + t2 * cos], axis=-1)

    q = rot(q_ref[...])
    k = rot(k_ref[...]) * (RET_DK ** -0.5)
    pos = c * chunk + lax.broadcasted_iota(jnp.int32, (chunk, 1), 0)
    k = jnp.where(pos >= pad, k, 0.0)
    qb = q.astype(BF16)
    vb = v_ref[...].astype(BF16)
    scores = lax.dot_general(qb, k.astype(BF16), (((1,), (1,)), ((), ())),
                             preferred_element_type=F32) * dm_ref[0]
    intra = jnp.dot(scores.astype(BF16), vb, preferred_element_type=F32)
    state = st_ref[...]
    inter = jnp.dot(qb, state.astype(BF16), preferred_element_type=F32) * qd_ref[0]
    st_ref[...] = state * cd_ref[0] + lax.dot_general((k * kd_ref[0]).astype(BF16), vb,
                                                      (((0,), (0,)), ((), ())),
                                                      preferred_element_type=F32)
    o = intra + inter
    mu = jnp.mean(o, axis=-1, keepdims=True)
    oc = o - mu
    var = jnp.mean(oc * oc, axis=-1, keepdims=True)
    o = oc * lax.rsqrt(var + LN_EPS) * g_ref[...]
    gate = gate_ref[...]
    o_ref[...] = (gate * jax.nn.sigmoid(gate) * o).astype(o_ref.dtype)


def _retention_mixer(h, lp, pad, w_in, gn_g, w_out, ln_g, ln_b, tm):
    tp, d = h.shape
    n_batch = tp // lp
    chunk = _tile(lp, 384, SEQ_ALIGN)
    ncb = lp // chunk
    qd = RET_HEADS * RET_DK
    vd = RET_HEADS * RET_DV
    proj = _matmul(h, w_in.astype(BF16), F32, tm, _tile(w_in.shape[1], 1536, LANES))
    half = RET_DK // 2
    pos = jnp.arange(lp, dtype=F32) - pad
    inv_freq = jnp.power(RET_THETA, -jnp.arange(half, dtype=F32) / half)
    ang = pos[:, None] * inv_freq[None, :]
    cos, sin = jnp.cos(ang), jnp.sin(ang)
    log_gamma = jnp.log1p(-jnp.power(2.0, -5.0 - jnp.arange(RET_HEADS, dtype=F32)))
    idx = jnp.arange(chunk, dtype=F32)
    rel = idx[:, None] - idx[None, :]
    dmask = jnp.where(rel[None] >= 0, jnp.exp(log_gamma[:, None, None] * jnp.maximum(rel, 0.0)[None]), 0.0)
    q_decay = jnp.exp(log_gamma[:, None] * (idx[None, :] + 1.0))[:, :, None]
    k_decay = jnp.exp(log_gamma[:, None] * (chunk - 1.0 - idx[None, :]))[:, :, None]
    chunk_decay = jnp.exp(log_gamma * chunk).reshape(RET_HEADS, 1, 1)
    kb = qd // RET_DK
    vb = 2 * qd // RET_DV
    gb = (2 * qd + vd) // RET_DV
    o = pl.pallas_call(
        functools.partial(_ret_kernel, chunk=chunk, pad=pad),
        grid=(n_batch, RET_HEADS, ncb),
        in_specs=[pl.BlockSpec((chunk, RET_DK), lambda b, hh, c: (b * ncb + c, hh)),
                  pl.BlockSpec((chunk, RET_DK), lambda b, hh, c: (b * ncb + c, kb + hh)),
                  pl.BlockSpec((chunk, RET_DV), lambda b, hh, c: (b * ncb + c, vb + hh)),
                  pl.BlockSpec((chunk, RET_DV), lambda b, hh, c: (b * ncb + c, gb + hh)),
                  pl.BlockSpec((chunk, half), lambda b, hh, c: (c, 0)),
                  pl.BlockSpec((chunk, half), lambda b, hh, c: (c, 0)),
                  pl.BlockSpec((1, chunk, chunk), lambda b, hh, c: (hh, 0, 0)),
                  pl.BlockSpec((1, chunk, 1), lambda b, hh, c: (hh, 0, 0)),
                  pl.BlockSpec((1, chunk, 1), lambda b, hh, c: (hh, 0, 0)),
                  pl.BlockSpec((1, 1, 1), lambda b, hh, c: (hh, 0, 0)),
                  pl.BlockSpec((1, RET_DV), lambda b, hh, c: (0, hh))],
        out_specs=pl.BlockSpec((chunk, RET_DV), lambda b, hh, c: (b * ncb + c, hh)),
        out_shape=jax.ShapeDtypeStruct((tp, vd), BF16),
        scratch_shapes=[pltpu.VMEM((RET_DK, RET_DV), F32)],
        compiler_params=_cparams("parallel", "parallel", "arbitrary"),
        name="retention",
    )(proj, proj, proj, proj, cos, sin, dmask, q_decay, k_decay, chunk_decay,
      gn_g.reshape(1, vd).astype(F32))
    return _matmul_res_ln(o, w_out.astype(BF16), h, ln_g, ln_b, tm)


ONES_ROWS = 16


def _diff_kernel(q_ref, k_ref, vt_ref, lam_ref, g_ref, o_ref, vext_ref, m_ref, acc_ref, s0_ref, s1_ref,
                 s2_ref, s3_ref, mx_ref,
                 *, tq, pad, lambda_init):
    i = pl.program_id(2)
    nkb = vext_ref.shape[0]

    @pl.when(i == 0)
    def _():
        for jb in range(nkb):
            vext_ref[jb, 0:DIFF_DV, :] = vt_ref[:, jb * tq:(jb + 1) * tq]
            vext_ref[jb, DIFF_DV:, :] = jnp.ones((ONES_ROWS, tq), BF16)

    q = q_ref[...]
    lane = lax.broadcasted_iota(jnp.int32, q.shape, 1)
    zero = jnp.zeros_like(q)
    q_both = jnp.concatenate([jnp.where(lane < DIFF_DQK, q, zero), jnp.where(lane >= DIFF_DQK, q, zero)],
                             axis=0)
    m_ref[...] = jnp.full(m_ref.shape, MASK_VALUE, F32)
    acc_ref[...] = jnp.zeros(acc_ref.shape, F32)

    def scores(j):
        start = j * tq if isinstance(j, int) else pl.multiple_of(j * tq, tq)
        k = k_ref[pl.ds(start, tq), :]
        return lax.dot_general(k, q_both, (((1,), (1,)), ((), ())), preferred_element_type=F32)

    s_refs = (s0_ref, s1_ref, s2_ref, s3_ref)

    def produce(slot, j):
        s = scores(j)
        s_refs[slot][...] = s
        mx_ref[slot] = jnp.max(s, axis=0, keepdims=True)

    def accumulate(m_new, pe, v):
        m_prev = m_ref[...]
        acc_ref[...] = jnp.exp2(m_prev - m_new) * acc_ref[...] + jnp.dot(v, pe, preferred_element_type=F32)
        m_ref[...] = m_new

    def consume_masked(s, j):
        kpos = j * tq + lax.broadcasted_iota(jnp.int32, (tq, tq), 0)
        qpos = i * tq + lax.broadcasted_iota(jnp.int32, (tq, tq), 1)
        allowed = (kpos <= qpos) & (kpos >= pad)
        s = jnp.where(jnp.concatenate([allowed, allowed], axis=1), s, MASK_VALUE)
        m_new = jnp.maximum(m_ref[...], jnp.max(s, axis=0, keepdims=True))
        accumulate(m_new, jnp.exp2((s - m_new).astype(BF16)), vext_ref[j])

    def consume(slot, j):
        m_new = jnp.maximum(m_ref[...], mx_ref[slot])
        accumulate(m_new, jnp.exp2((s_refs[slot][...] - m_new).astype(BF16)), vext_ref[j])

    def consume2(slot_a, slot_b, j):
        m_new = jnp.maximum(m_ref[...], jnp.maximum(mx_ref[slot_a], mx_ref[slot_b]))
        pe = jnp.concatenate([jnp.exp2((s_refs[slot_a][...] - m_new).astype(BF16)),
                              jnp.exp2((s_refs[slot_b][...] - m_new).astype(BF16))], axis=0)
        accumulate(m_new, pe, jnp.concatenate([vext_ref[j], vext_ref[j + 1]], axis=1))

    @pl.when(i == 0)
    def _():
        consume_masked(scores(0), 0)

    @pl.when(i > 0)
    def _():
        n = i - 1
        n_quads = n // 4
        s3_ref[...] = scores(0)
        produce(0, 1)
        produce(1, jnp.minimum(2, i))
        consume_masked(s3_ref[...], 0)

        def quad(u, carry):
            a = 1 + 4 * u
            produce(2, a + 2)
            produce(3, a + 3)
            consume2(0, 1, a)
            produce(0, jnp.minimum(a + 4, i))
            produce(1, jnp.minimum(a + 5, i))
            consume2(2, 3, a + 2)
            return carry

        lax.fori_loop(0, n_quads, quad, 0)
        first = 1 + 4 * n_quads

        def pair(jj, carry):
            a = first + 2 * jj
            produce(1, a + 1)
            consume(0, a)
            produce(0, a + 2)
            consume(1, a + 1)
            return carry

        lax.fori_loop(0, (n - 4 * n_quads) // 2, pair, 0)

        @pl.when(n % 2 == 1)
        def _():
            produce(1, i)
            consume(0, i - 1)
            consume_masked(s1_ref[...], i)

        @pl.when(n % 2 == 0)
        def _():
            consume_masked(s0_ref[...], i)

    lam_p = lam_ref[...]
    lam = (jnp.exp(jnp.sum(lam_p[0:1] * lam_p[1:2], axis=-1, keepdims=True))
           - jnp.exp(jnp.sum(lam_p[2:3] * lam_p[3:4], axis=-1, keepdims=True)) + lambda_init)
    acc = acc_ref[...]
    o_both = acc[0:DIFF_DV] / acc[DIFF_DV:DIFF_DV + 1]
    o = o_both[:, :tq] - lam * o_both[:, tq:]
    o = o * lax.rsqrt(jnp.mean(o * o, axis=0, keepdims=True) + LN_EPS)
    o = o * (g_ref[...] * (1.0 - lambda_init))
    o_ref[...] = o.T.astype(o_ref.dtype)


def _mm_nt_kernel(wt_ref, x_ref, o_ref):
    o_ref[...] = lax.dot_general(wt_ref[...], x_ref[...].astype(BF16), (((1,), (1,)), ((), ())),
                                 preferred_element_type=F32).astype(o_ref.dtype)


def _matmul_nt(wt, x, out_dtype, tm):
    n, k = wt.shape
    m = x.shape[0]
    return pl.pallas_call(
        _mm_nt_kernel,
        grid=(m // tm,),
        in_specs=[pl.BlockSpec((n, k), lambda i: (0, 0)), pl.BlockSpec((tm, k), lambda i: (i, 0))],
        out_specs=pl.BlockSpec((n, tm), lambda i: (0, i)),
        out_shape=jax.ShapeDtypeStruct((n, m), out_dtype),
        compiler_params=_cparams("parallel"),
        name="matmul_nt",
    )(wt, x)


def _diff_attn_mixer(h, lp, pad, w_in, lq1, lk1, lq2, lk2, subln_g, w_out, lambda_init, ln_g, ln_b, tm):
    tp, d = h.shape
    n_batch = tp // lp
    qk = DIFF_HEADS * 2 * DIFF_DQK
    col_scale = jnp.concatenate([jnp.full((qk,), DIFF_DQK ** -0.5 * math.log2(math.e), F32),
                                 jnp.ones((qk,), F32)])
    proj = _matmul(h, (w_in[:, :2 * qk] * col_scale[None, :]).astype(BF16), BF16, tm, _tile(2 * qk, 2048, LANES))
    vt = _matmul_nt(w_in[:, 2 * qk:].T.astype(BF16), h, BF16, _tile(tp, 768, LANES))
    tq = _tile(lp, 384, SEQ_ALIGN)
    nq = lp // tq
    lam_p = jnp.stack([lq1, lk1, lq2, lk2]).astype(F32)
    o = pl.pallas_call(
        functools.partial(_diff_kernel, tq=tq, pad=pad, lambda_init=lambda_init),
        grid=(n_batch, DIFF_HEADS, nq),
        in_specs=[pl.BlockSpec((tq, LANES), lambda b, hh, i: (b * nq + i, hh)),
                  pl.BlockSpec((lp, LANES), lambda b, hh, i: (b, DIFF_HEADS + hh)),
                  pl.BlockSpec((DIFF_DV, lp), lambda b, hh, i: (hh, b)),
                  pl.BlockSpec((4, DIFF_DQK), lambda b, hh, i: (0, 0)),
                  pl.BlockSpec((DIFF_DV, 1), lambda b, hh, i: (0, 0))],
        out_specs=pl.BlockSpec((tq, DIFF_DV), lambda b, hh, i: (b * nq + i, hh)),
        out_shape=jax.ShapeDtypeStruct((tp, DIFF_HEADS * DIFF_DV), BF16),
        scratch_shapes=[pltpu.VMEM((nq, DIFF_DV + ONES_ROWS, tq), BF16),
                        pltpu.VMEM((1, 2 * tq), F32),
                        pltpu.VMEM((DIFF_DV + ONES_ROWS, 2 * tq), F32),
                        ] + [pltpu.VMEM((tq, 2 * tq), F32)] * 4 + [pltpu.VMEM((4, 1, 2 * tq), F32)],
        compiler_params=_cparams("parallel", "parallel", "arbitrary"),
        name="diff_attn",
    )(proj, proj, vt, lam_p, subln_g.reshape(DIFF_DV, 1).astype(F32))
    return _matmul_res_ln(o, w_out.astype(BF16), h, ln_g, ln_b, tm)


SORT_ALIGN = 16
SORT_CHUNK = 512


def _router_kernel(x_ref, w_ref, b_ref, idx_ref, gate_ref, rank_ref, cnt_ref, *, tm):
    x = x_ref[...]
    x_hi = x.astype(BF16)
    x_lo = (x - x_hi.astype(F32)).astype(BF16)
    hi = jnp.dot(x_hi, w_ref[...], preferred_element_type=F32)
    lo = jnp.dot(x_lo, w_ref[:, :LANES], preferred_element_type=F32)
    logits = hi[:, :LANES] + hi[:, LANES:] + lo + b_ref[...]
    lane = lax.broadcasted_iota(jnp.int32, logits.shape, 1).astype(F32)
    work = jnp.where(lane < N_EXPERTS, logits, -jnp.inf)
    vals, idxs = [], []
    picked = jnp.zeros(logits.shape, F32)
    for _ in range(TOP_K):
        m = jnp.max(work, axis=-1, keepdims=True)
        sel = jnp.min(jnp.where(work == m, lane, float(LANES)), axis=-1, keepdims=True)
        hit = lane == sel
        work = jnp.where(hit, -jnp.inf, work)
        picked = jnp.where(hit, 1.0, picked)
        vals.append(m)
        idxs.append(sel)
    exps = [jnp.exp(v - vals[0]) for v in vals]
    tot = exps[0] + exps[1] + exps[2] + exps[3]
    r = lax.broadcasted_iota(jnp.int32, (tm, tm), 0)
    c = lax.broadcasted_iota(jnp.int32, (tm, tm), 1)
    lower = jnp.where(c < r, 1.0, 0.0).astype(BF16)
    before = jnp.dot(lower, picked.astype(BF16), preferred_element_type=F32)
    for kk in range(TOP_K):
        idx_ref[:, kk:kk + 1] = idxs[kk].astype(jnp.int32)
        gate_ref[:, kk:kk + 1] = exps[kk] / tot
        rank_ref[:, kk:kk + 1] = jnp.sum(jnp.where(lane == idxs[kk], before, 0.0), axis=-1,
                                         keepdims=True).astype(jnp.int32)
    cnt_ref[0] = jnp.sum(picked, axis=0, keepdims=True)


BIG_CHUNKS = 4


def _row_dmas(n_chunks, make_copy, wait):
    n_big = n_chunks // BIG_CHUNKS
    big = BIG_CHUNKS * SORT_ALIGN

    def run(cp):
        if wait:
            cp.wait()
        else:
            cp.start()

    def big_body(c, carry):
        run(make_copy(c * big, big))
        return carry

    def small_body(c, carry):
        run(make_copy(n_big * big + c * SORT_ALIGN, SORT_ALIGN))
        return carry

    lax.fori_loop(0, n_big, big_body, 0)
    lax.fori_loop(0, n_chunks - n_big * BIG_CHUNKS, small_body, 0)


def _segment_dma(tab_ref, n_seg, tile, local_ref, slots_ref, sem, to_slots, wait):
    def per_expert(e, carry):
        lo0 = tab_ref[tile * N_EXPERTS + e]
        go0 = tab_ref[n_seg + tile * N_EXPERTS + e]

        def make_copy(row, rows):
            local = local_ref.at[pl.ds(pl.multiple_of(lo0 + row, SORT_ALIGN), rows), :]
            slots = slots_ref.at[pl.ds(pl.multiple_of(go0 + row, SORT_ALIGN), rows), :]
            return (pltpu.make_async_copy(local, slots, sem) if to_slots
                    else pltpu.make_async_copy(slots, local, sem))

        _row_dmas(tab_ref[2 * n_seg + tile * N_EXPERTS + e], make_copy, wait)
        return carry
    lax.fori_loop(0, N_EXPERTS, per_expert, 0)


def _zero_fill_dma(tab_ref, n_seg, n_blocks, zero_ref, slots_ref, sem, wait):
    bm = zero_ref.shape[0]

    def per_expert(e, carry):
        off = tab_ref[3 * n_seg + e]

        def make_copy(row, rows):
            dst = slots_ref.at[pl.ds(pl.multiple_of(off + row, SORT_ALIGN), rows), :]
            return pltpu.make_async_copy(zero_ref.at[pl.ds(0, rows), :], dst, sem)

        _row_dmas(tab_ref[3 * n_seg + N_EXPERTS + e], make_copy, wait)
        return carry
    lax.fori_loop(0, N_EXPERTS, per_expert, 0)

    def per_block(blk, carry):
        cp = pltpu.make_async_copy(zero_ref, slots_ref.at[pl.ds(pl.multiple_of(blk * bm, bm), bm), :], sem)
        if wait:
            cp.wait()
        else:
            cp.start()
        return carry
    lax.fori_loop(tab_ref[3 * n_seg + 2 * N_EXPERTS], n_blocks, per_block, 0)


def _dispatch_kernel(tab_ref, x_ref, lrow_ref, slots_ref, sorted_ref, zero_ref, sem, zero_sem,
                     *, tm, rs, n_blocks):
    i = pl.program_id(0)
    n_seg = pl.num_programs(0) * N_EXPERTS
    slot = i % 2
    buf = sorted_ref.at[slot]

    @pl.when(i == 0)
    def _():
        zero_ref[...] = jnp.zeros(zero_ref.shape, BF16)
        _zero_fill_dma(tab_ref, n_seg, n_blocks, zero_ref, slots_ref, zero_sem, wait=False)
        _zero_fill_dma(tab_ref, n_seg, n_blocks, zero_ref, slots_ref, zero_sem, wait=True)

    xb = x_ref[...].astype(BF16)
    lr = lrow_ref[...]
    for rc in range(rs // SORT_CHUNK):
        rows = rc * SORT_CHUNK + lax.broadcasted_iota(jnp.int32, (SORT_CHUNK, tm), 0)
        hit = (rows == lr[0:1]) | (rows == lr[1:2]) | (rows == lr[2:3]) | (rows == lr[3:4])
        perm = jnp.where(hit, 1.0, 0.0).astype(BF16)
        buf[rc * SORT_CHUNK:(rc + 1) * SORT_CHUNK, :] = jnp.dot(
            perm, xb, preferred_element_type=F32).astype(BF16)
    _segment_dma(tab_ref, n_seg, i, buf, slots_ref, sem.at[slot], True, wait=False)

    @pl.when(i > 0)
    def _():
        _segment_dma(tab_ref, n_seg, i - 1, sorted_ref.at[1 - slot], slots_ref, sem.at[1 - slot], True, wait=True)

    @pl.when(i == pl.num_programs(0) - 1)
    def _():
        _segment_dma(tab_ref, n_seg, i, buf, slots_ref, sem.at[slot], True, wait=True)


def _combine_kernel(tab_ref, res_ref, lrow_ref, gate_ref, g_ref, b_ref, y_ref, o_ref, ysort_ref, sem,
                    *, tm, rs):
    i = pl.program_id(0)
    n_seg = pl.num_programs(0) * N_EXPERTS
    slot = i % 2

    @pl.when(i == 0)
    def _():
        ysort_ref[...] = jnp.zeros(ysort_ref.shape, BF16)
        _segment_dma(tab_ref, n_seg, 0, ysort_ref.at[0], y_ref, sem.at[0], False, wait=False)

    @pl.when(i + 1 < pl.num_programs(0))
    def _():
        _segment_dma(tab_ref, n_seg, i + 1, ysort_ref.at[1 - slot], y_ref, sem.at[1 - slot], False, wait=False)

    _segment_dma(tab_ref, n_seg, i, ysort_ref.at[slot], y_ref, sem.at[slot], False, wait=True)
    ysort = ysort_ref.at[slot]
    lr = lrow_ref[...]
    gt = gate_ref[...]
    y = jnp.zeros(res_ref.shape, F32)
    for rc in range(rs // SORT_CHUNK):
        cols = rc * SORT_CHUNK + lax.broadcasted_iota(jnp.int32, (tm, SORT_CHUNK), 1)
        w = jnp.zeros((tm, SORT_CHUNK), F32)
        for kk in range(TOP_K):
            w = jnp.where(cols == lr[:, kk:kk + 1], gt[:, kk:kk + 1], w)
        y += jnp.dot(w.astype(BF16), ysort[rc * SORT_CHUNK:(rc + 1) * SORT_CHUNK, :],
                     preferred_element_type=F32)
    o_ref[...] = _layer_norm_rows(DEEPNORM_ALPHA * res_ref[...] + y, g_ref[...], b_ref[...])


def _expert_kernel(be_ref, nu_ref, x_ref, wu_ref, bu_ref, wd_ref, bd_ref, o_ref, wu_bf_ref, wd_bf_ref):
    i = pl.program_id(0)
    live = i < nu_ref[0]

    @pl.when(live & ((i == 0) | (be_ref[i] != be_ref[jnp.maximum(i - 1, 0)])))
    def _():
        wu_bf_ref[...] = wu_ref[0, 0].astype(BF16)
        wd_bf_ref[...] = wd_ref[0, 0].astype(BF16)

    @pl.when(live)
    def _():
        hb = jnp.dot(x_ref[...], wu_bf_ref[...], preferred_element_type=F32) + bu_ref[0]
        x_glu = jnp.minimum(hb[:, :D_EXPERT], SWIGLU_LIMIT)
        x_lin = jnp.clip(hb[:, D_EXPERT:], -SWIGLU_LIMIT, SWIGLU_LIMIT)
        act = x_glu * jax.nn.sigmoid(SWIGLU_ALPHA * x_glu) * (x_lin + 1.0)
        y = jnp.dot(act.astype(BF16), wd_bf_ref[...], preferred_element_type=F32) + bd_ref[0]
        o_ref[...] = y.astype(o_ref.dtype)

    @pl.when(jnp.logical_not(live))
    def _():
        o_ref[...] = jnp.zeros(o_ref.shape, o_ref.dtype)


def _moe_ffn(h, layer, w_router, b_router, w_up_all, b_up, w_down_all, b_down, ln_g, ln_b, tm, bm):
    tp, d = h.shape
    a = tp * TOP_K
    wr = jnp.zeros((d, LANES), F32).at[:, :N_EXPERTS].set(w_router.astype(F32))
    wr_hi = wr.astype(BF16)
    wr = jnp.concatenate([wr_hi, (wr - wr_hi.astype(F32)).astype(BF16)], axis=1)
    br = jnp.zeros((1, LANES), F32).at[0, :N_EXPERTS].set(b_router.astype(F32))
    row = lambda i: (i, 0)
    fix = lambda i: (0, 0)
    nt = tp // tm
    idx, gates, rank, cnt = pl.pallas_call(
        functools.partial(_router_kernel, tm=tm),
        grid=(nt,),
        in_specs=[pl.BlockSpec((tm, d), row), pl.BlockSpec((d, 2 * LANES), fix), pl.BlockSpec((1, LANES), fix)],
        out_specs=[pl.BlockSpec((tm, TOP_K), row), pl.BlockSpec((tm, TOP_K), row),
                   pl.BlockSpec((tm, TOP_K), row), pl.BlockSpec((1, 1, LANES), lambda i: (i, 0, 0))],
        out_shape=[jax.ShapeDtypeStruct((tp, TOP_K), jnp.int32), jax.ShapeDtypeStruct((tp, TOP_K), F32),
                   jax.ShapeDtypeStruct((tp, TOP_K), jnp.int32), jax.ShapeDtypeStruct((nt, 1, LANES), F32)],
        compiler_params=_cparams("parallel"),
        name="router",
    )(h, wr, br)
    counts = cnt[:, 0, :N_EXPERTS].astype(jnp.int32)
    seg = ((counts + SORT_ALIGN - 1) // SORT_ALIGN) * SORT_ALIGN
    tile_off = jnp.cumsum(seg, axis=0) - seg
    region = ((jnp.sum(seg, axis=0) + bm - 1) // bm) * bm
    region_end = jnp.cumsum(region)
    slot_off = (region_end - region)[None, :] + tile_off
    local_off = jnp.cumsum(seg, axis=1) - seg
    used = jnp.sum(seg, axis=0)
    n_used = (region_end[-1:] // bm).astype(jnp.int32)
    table = jnp.concatenate([local_off.reshape(-1), slot_off.reshape(-1), (seg // SORT_ALIGN).reshape(-1),
                             region_end - region + used, (region - used) // SORT_ALIGN,
                             n_used]).astype(jnp.int32)
    onehot = idx.reshape(nt, tm, TOP_K, 1) == jnp.arange(N_EXPERTS, dtype=jnp.int32)
    lrow = jnp.sum(jnp.where(onehot, local_off[:, None, None, :], 0), axis=-1).reshape(tp, TOP_K) + rank
    lrow_t = jnp.concatenate([lrow.T, jnp.full((8 - TOP_K, tp), -1, jnp.int32)], axis=0)
    rs = -(-(TOP_K * tm + N_EXPERTS * (SORT_ALIGN - 1)) // SORT_CHUNK) * SORT_CHUNK
    n_blocks = -(-(a + nt * N_EXPERTS * (SORT_ALIGN - 1)) // bm) + N_EXPERTS
    n_slots = n_blocks * bm
    block_start = jnp.arange(n_blocks, dtype=jnp.int32) * bm
    block_e = jnp.minimum(jnp.sum((region_end[None, :] <= block_start[:, None]).astype(jnp.int32), axis=1),
                          N_EXPERTS - 1)
    x_slots = pl.pallas_call(
        functools.partial(_dispatch_kernel, tm=tm, rs=rs, n_blocks=n_blocks),
        grid_spec=pltpu.PrefetchScalarGridSpec(
            num_scalar_prefetch=1,
            grid=(nt,),
            in_specs=[pl.BlockSpec((tm, d), lambda i, tab: (i, 0)),
                      pl.BlockSpec((8, tm), lambda i, tab: (0, i))],
            out_specs=pl.BlockSpec(memory_space=pl.ANY),
            scratch_shapes=[pltpu.VMEM((2, rs, d), BF16), pltpu.VMEM((bm, d), BF16),
                            pltpu.SemaphoreType.DMA((2,)), pltpu.SemaphoreType.DMA(())],
        ),
        out_shape=jax.ShapeDtypeStruct((n_slots, d), BF16),
        compiler_params=_cparams("arbitrary"),
        name="moe_dispatch",
    )(table, h, lrow_t)
    y_slots = pl.pallas_call(
        _expert_kernel,
        grid_spec=pltpu.PrefetchScalarGridSpec(
            num_scalar_prefetch=2,
            grid=(n_blocks,),
            in_specs=[pl.BlockSpec((bm, d), lambda i, be, nu: (i, 0)),
                      pl.BlockSpec((1, 1, d, 2 * D_EXPERT), lambda i, be, nu: (layer, be[i], 0, 0)),
                      pl.BlockSpec((1, 1, 2 * D_EXPERT), lambda i, be, nu: (be[i], 0, 0)),
                      pl.BlockSpec((1, 1, D_EXPERT, d), lambda i, be, nu: (layer, be[i], 0, 0)),
                      pl.BlockSpec((1, 1, d), lambda i, be, nu: (be[i], 0, 0))],
            out_specs=pl.BlockSpec((bm, d), lambda i, be, nu: (i, 0)),
            scratch_shapes=[pltpu.VMEM((d, 2 * D_EXPERT), BF16), pltpu.VMEM((D_EXPERT, d), BF16)],
        ),
        out_shape=jax.ShapeDtypeStruct((n_slots, d), BF16),
        compiler_params=_cparams("arbitrary"),
        name="experts",
    )(block_e, n_used, x_slots, w_up_all, b_up.reshape(N_EXPERTS, 1, 2 * D_EXPERT).astype(F32),
      w_down_all, b_down.reshape(N_EXPERTS, 1, d).astype(F32))
    return pl.pallas_call(
        functools.partial(_combine_kernel, tm=tm, rs=rs),
        grid_spec=pltpu.PrefetchScalarGridSpec(
            num_scalar_prefetch=1,
            grid=(nt,),
            in_specs=[pl.BlockSpec((tm, d), lambda i, tab: (i, 0)),
                      pl.BlockSpec((tm, TOP_K), lambda i, tab: (i, 0)),
                      pl.BlockSpec((tm, TOP_K), lambda i, tab: (i, 0)),
                      pl.BlockSpec((1, d), lambda i, tab: (0, 0)),
                      pl.BlockSpec((1, d), lambda i, tab: (0, 0)),
                      pl.BlockSpec(memory_space=pl.ANY)],
            out_specs=pl.BlockSpec((tm, d), lambda i, tab: (i, 0)),
            scratch_shapes=[pltpu.VMEM((2, rs, d), BF16), pltpu.SemaphoreType.DMA((2,))],
        ),
        out_shape=jax.ShapeDtypeStruct((tp, d), F32),
        compiler_params=_cparams("arbitrary"),
        name="moe_combine_ln",
    )(table, h, lrow, gates, ln_g.reshape(1, d), ln_b.reshape(1, d), y_slots)


def kernel(x, meta_tokens, s5_w_in, s5_lambda_re, s5_lambda_im, s5_log_step, s5_b_re, s5_b_im, s5_c_re, s5_c_im, s5_d, s5_w_glu, s5_b_glu, s5_w_out, ret_w_in, ret_gn_g, ret_w_out, diff_w_in, diff_lambda_q1, diff_lambda_k1, diff_lambda_q2, diff_lambda_k2, diff_subln_g, diff_w_out, ln_mix_g, ln_mix_b, moe_w_router, moe_b_router, moe_w_up, moe_b_up, moe_w_down, moe_b_down, ln_ffn_g, ln_ffn_b):
    n_batch, seq, d = x.shape
    length = seq + N_META
    pad = (-length) % SEQ_ALIGN
    lp = length + pad
    assert pad % S5_CHUNK == 0 and d == D_MODEL
    tp = n_batch * lp
    tm = _tile(tp, 768)
    meta = jnp.broadcast_to(meta_tokens[None].astype(x.dtype), (n_batch, N_META, d))
    h = jnp.concatenate([jnp.zeros((n_batch, pad, d), x.dtype), meta, x], axis=1).reshape(tp, d)
    for i in range(DEPTH):
        kind = i % N_MIXERS
        j = i // N_MIXERS
        if kind == 0:
            h = _s5_mixer(h, lp, pad, s5_w_in[j], s5_lambda_re[j], s5_lambda_im[j], s5_log_step[j],
                          s5_b_re[j], s5_b_im[j], s5_c_re[j], s5_c_im[j], s5_d[j],
                          s5_w_glu[j], s5_b_glu[j], s5_w_out[j], ln_mix_g[i], ln_mix_b[i], tm)
        elif kind == 1:
            h = _retention_mixer(h, lp, pad, ret_w_in[j], ret_gn_g[j], ret_w_out[j],
                                 ln_mix_g[i], ln_mix_b[i], tm)
        else:
            lambda_init = 0.8 - 0.6 * math.exp(-0.3 * i)
            h = _diff_attn_mixer(h, lp, pad, diff_w_in[j], diff_lambda_q1[j], diff_lambda_k1[j],
                                 diff_lambda_q2[j], diff_lambda_k2[j], diff_subln_g[j], diff_w_out[j],
                                 lambda_init, ln_mix_g[i], ln_mix_b[i], tm)
        h = _moe_ffn(h, i, moe_w_router[i], moe_b_router[i], moe_w_up, moe_b_up[i],
                     moe_w_down, moe_b_down[i], ln_ffn_g[i], ln_ffn_b[i], tm, MOE_BLOCK)
    return h.reshape(n_batch, lp, d)[:, pad + N_META:, :]
```

```python
import functools
import math

import jax
import jax.numpy as jnp
from jax import lax
from jax.experimental import pallas as pl
from jax.experimental.pallas import tpu as pltpu

F32 = jnp.float32
BF16 = jnp.bfloat16

D_MODEL = 1024
DEPTH = 4
N_META = 16
N_MIXERS = 3
S5_GROUP_CH = 16
S5_GROUPS = D_MODEL // S5_GROUP_CH
S5_STATE = 64
S5_CHUNK = 16
S5_OCT = 8
N_OCT = S5_GROUPS // S5_OCT
RET_HEADS = 4
RET_DK = D_MODEL // RET_HEADS
RET_DV = 2 * RET_DK
RET_THETA = 10000.0
DIFF_HEADS = D_MODEL // 128
DIFF_DQK = 64
DIFF_DV = 128
N_EXPERTS = 32
TOP_K = 4
D_EXPERT = D_MODEL
SWIGLU_LIMIT = 7.0
SWIGLU_ALPHA = 1.702
MOE_BLOCK = 512
LN_EPS = 1e-5
MASK_VALUE = -(2.0 ** 100)
DEEPNORM_ALPHA = (2 * DEPTH) ** 0.25
SEQ_ALIGN = 128
LANES = 128
VMEM_LIMIT = 56 * 1024 * 1024


def _cparams(*sem):
    return pltpu.CompilerParams(dimension_semantics=sem, vmem_limit_bytes=VMEM_LIMIT)


def _tile(n, target, mult=8):
    best = None
    for t in range(mult, min(n, target) + 1, mult):
        if n % t == 0:
            best = t
    assert best is not None, (n, target, mult)
    return best


def _layer_norm_rows(v, g, b):
    mu = jnp.mean(v, axis=-1, keepdims=True)
    c = v - mu
    var = jnp.mean(c * c, axis=-1, keepdims=True)
    return c * lax.rsqrt(var + LN_EPS) * g + b


def _mm_kernel(x_ref, w_ref, o_ref):
    o_ref[...] = jnp.dot(x_ref[...].astype(BF16), w_ref[...],
                         preferred_element_type=F32).astype(o_ref.dtype)


def _matmul(x, w, out_dtype, tm, tn):
    m, k = x.shape
    n = w.shape[1]
    return pl.pallas_call(
        _mm_kernel,
        grid=(n // tn, m // tm),
        in_specs=[pl.BlockSpec((tm, k), lambda j, i: (i, 0)),
                  pl.BlockSpec((k, tn), lambda j, i: (0, j))],
        out_specs=pl.BlockSpec((tm, tn), lambda j, i: (i, j)),
        out_shape=jax.ShapeDtypeStruct((m, n), out_dtype),
        compiler_params=_cparams("parallel", "parallel"),
        name="matmul",
    )(x, w)


def _mm_res_ln_kernel(x_ref, w_ref, res_ref, g_ref, b_ref, o_ref):
    y = jnp.dot(x_ref[...].astype(BF16), w_ref[...], preferred_element_type=F32)
    o_ref[...] = _layer_norm_rows(DEEPNORM_ALPHA * res_ref[...] + y, g_ref[...], b_ref[...])


def _matmul_res_ln(x, w, res, g, b, tm):
    m, k = x.shape
    n = w.shape[1]
    row = lambda i: (i, 0)
    fix = lambda i: (0, 0)
    return pl.pallas_call(
        _mm_res_ln_kernel,
        grid=(m // tm,),
        in_specs=[pl.BlockSpec((tm, k), row), pl.BlockSpec((k, n), fix), pl.BlockSpec((tm, n), row),
                  pl.BlockSpec((1, n), fix), pl.BlockSpec((1, n), fix)],
        out_specs=pl.BlockSpec((tm, n), row),
        out_shape=jax.ShapeDtypeStruct((m, n), F32),
        compiler_params=_cparams("parallel"),
        name="matmul_res_ln",
    )(x, w, res, g.reshape(1, n), b.reshape(1, n))


def _expand_block_diag(a, row_inner, outer, inner):
    rows, k = a.shape
    cols = outer * S5_OCT * inner
    r = jnp.arange(k)[:, None]
    c = jnp.arange(cols)[None, :]
    rep = ((r // inner == c // (S5_OCT * inner)) & (r % inner == c % inner)).astype(BF16)
    out = jnp.dot(a.astype(BF16), rep, preferred_element_type=BF16)
    g_row = (jnp.arange(rows)[:, None] // row_inner) % S5_OCT
    h_col = (c // inner) % S5_OCT
    return jnp.where(g_row == h_col, out, jnp.zeros_like(out))


def _s5_tables(lam_re, lam_im, log_step, b_re, b_im, c_re, c_im, d_skip):
    hp = lax.Precision.HIGHEST
    g_, n_, c_, q_, o_ = S5_GROUPS, S5_STATE, S5_GROUP_CH, S5_CHUNK, S5_OCT
    dt = jnp.exp(log_step.astype(F32))[:, None]
    lr = jnp.minimum(lam_re.astype(F32), -1e-4)
    li = lam_im.astype(F32)
    mag = jnp.exp(lr * dt)
    ab_re = mag * jnp.cos(li * dt)
    ab_im = mag * jnp.sin(li * dt)
    den = lr * lr + li * li
    f_re = ((ab_re - 1.0) * lr + ab_im * li) / den
    f_im = (ab_im * lr - (ab_re - 1.0) * li) / den
    br, bi = b_re.astype(F32), b_im.astype(F32)
    bbar_re = f_re[..., None] * br - f_im[..., None] * bi
    bbar_im = f_re[..., None] * bi + f_im[..., None] * br
    j = jnp.arange(q_ + 1, dtype=F32)[:, None, None]
    pmag = jnp.exp(j * (lr * dt)[None])
    pw_re = pmag * jnp.cos(j * (li * dt)[None])
    pw_im = pmag * jnp.sin(j * (li * dt)[None])
    cr, ci = c_re.astype(F32), c_im.astype(F32)
    w_re = pw_re[..., None] * bbar_re[None] - pw_im[..., None] * bbar_im[None]
    w_im = pw_re[..., None] * bbar_im[None] + pw_im[..., None] * bbar_re[None]
    kj = (jnp.einsum('gon,jgni->jgoi', cr, w_re[:q_], precision=hp)
          - jnp.einsum('gon,jgni->jgoi', ci, w_im[:q_], precision=hp))
    kj = kj.at[0].add(d_skip.astype(F32).reshape(g_, c_)[:, :, None] * jnp.eye(c_, dtype=F32)[None])
    s_idx = jnp.arange(q_)[:, None]
    t_idx = jnp.arange(q_)[None, :]
    lag = t_idx - s_idx
    kts = jnp.where((lag >= 0)[:, :, None, None, None], kj[jnp.clip(lag, 0, q_ - 1)], 0.0)
    kts = kts.reshape(q_, q_, N_OCT, o_, c_, c_).transpose(2, 0, 3, 5, 1, 4)
    m_tab = _expand_block_diag(kts.reshape(N_OCT * q_ * o_ * c_, q_ * c_), c_, q_, c_)
    m_tab = m_tab.reshape(N_OCT, q_, o_ * c_, q_ * o_ * c_)
    pst = jnp.stack([w_re[:q_][::-1], w_im[:q_][::-1]], axis=0)
    pst = pst.reshape(2, q_, N_OCT, o_, n_, c_).transpose(2, 1, 3, 5, 0, 4)
    p_tab = _expand_block_diag(pst.reshape(N_OCT * q_ * o_ * c_, 2 * n_), c_, 2, n_)
    p_tab = p_tab.reshape(N_OCT, q_, o_ * c_, 2 * o_ * n_)
    ar = pw_re[1:].reshape(q_, N_OCT, o_, n_)
    ai = pw_im[1:].reshape(q_, N_OCT, o_, n_)
    cr_o = cr.reshape(N_OCT, o_, c_, n_)
    ci_o = ci.reshape(N_OCT, o_, c_, n_)
    r_re = cr_o[None] * ar[:, :, :, None, :] - ci_o[None] * ai[:, :, :, None, :]
    r_im = -(cr_o[None] * ai[:, :, :, None, :] + ci_o[None] * ar[:, :, :, None, :])
    rst = jnp.stack([r_re, r_im], axis=0).transpose(2, 0, 3, 5, 1, 4)
    r_tab = _expand_block_diag(rst.reshape(N_OCT * 2 * o_ * n_, q_ * c_), n_, q_, c_)
    r_tab = r_tab.reshape(N_OCT, 2 * o_ * n_, q_ * o_ * c_)
    a16_re = pw_re[q_].reshape(N_OCT, o_ * n_)
    a16_im = pw_im[q_].reshape(N_OCT, o_ * n_)
    return m_tab, p_tab, r_tab, a16_re, a16_im


def _s5_chunk_rows(u_ref, s, tr, valid):
    us = u_ref[pl.ds(s, tr, stride=S5_CHUNK), :]
    return jnp.where(valid, us, 0.0).astype(BF16)


def _s5_valid(tr, chunks_per_batch, pad_chunks):
    chunk = pl.program_id(1) * tr + lax.broadcasted_iota(jnp.int32, (tr, 1), 0)
    return (chunk % chunks_per_batch) >= pad_chunks


def _s5_state_kernel(u_ref, p_ref, s_ref, *, tr, chunks_per_batch, pad_chunks):
    valid = _s5_valid(tr, chunks_per_batch, pad_chunks)
    acc = jnp.zeros(s_ref.shape, F32)
    for sp in range(S5_CHUNK // 2):
        us = jnp.concatenate([_s5_chunk_rows(u_ref, 2 * sp, tr, valid),
                              _s5_chunk_rows(u_ref, 2 * sp + 1, tr, valid)], axis=1)
        acc += jnp.dot(us, p_ref[0, sp], preferred_element_type=F32)
    s_ref[...] = acc


def _s5_scan_kernel(s_ref, ar_ref, ai_ref, x_ref, st_ref, *, tc):
    half = st_ref.shape[1] // 2

    @pl.when(pl.program_id(1) == 0)
    def _():
        st_ref[...] = jnp.zeros(st_ref.shape, F32)

    ar = ar_ref[...]
    ai = ai_ref[...]

    def body(c, carry):
        xr, xi = carry
        x_ref[c, :, :half] = xr
        x_ref[c, :, half:] = xi
        s = s_ref[c]
        return ar * xr - ai * xi + s[:, :half], ar * xi + ai * xr + s[:, half:]

    xr, xi = lax.fori_loop(0, tc, body, (st_ref[:, :half], st_ref[:, half:]))
    st_ref[:, :half] = xr
    st_ref[:, half:] = xi


def _gelu_tanh(y):
    return 0.5 * y * (1.0 + jnp.tanh(math.sqrt(2.0 / math.pi) * (y + 0.044715 * (y * y * y))))


def _s5_out_kernel(u_ref, xp_ref, m_ref, r_ref, z_ref, acc_ref, *, tr, chunks_per_batch, pad_chunks):
    valid = _s5_valid(tr, chunks_per_batch, pad_chunks)
    acc_ref[...] = jnp.dot(xp_ref[...].astype(BF16), r_ref[0], preferred_element_type=F32)
    for sp in range(S5_CHUNK // 2):
        us = jnp.concatenate([_s5_chunk_rows(u_ref, 2 * sp, tr, valid),
                              _s5_chunk_rows(u_ref, 2 * sp + 1, tr, valid)], axis=1)
        acc_ref[:, 2 * sp * LANES:] += jnp.dot(us, m_ref[0, sp, :, 2 * sp * LANES:],
                                               preferred_element_type=F32)
    for t in range(S5_CHUNK):
        z_ref[pl.ds(t, tr, stride=S5_CHUNK), :] = _gelu_tanh(acc_ref[:, t * LANES:(t + 1) * LANES])


def _s5_tail_kernel(z_ref, wg_ref, bg_ref, wo_ref, res_ref, g_ref, b_ref, o_ref):
    z = z_ref[...]
    t = jnp.dot(z.astype(BF16), wg_ref[...], preferred_element_type=F32) + bg_ref[...]
    glu = z * jax.nn.sigmoid(t)
    y = jnp.dot(glu.astype(BF16), wo_ref[...], preferred_element_type=F32)
    o_ref[...] = _layer_norm_rows(DEEPNORM_ALPHA * res_ref[...] + y, g_ref[...], b_ref[...])


def _s5_mixer(h, lp, pad, w_in, lam_re, lam_im, log_step, b_re, b_im, c_re, c_im, d_skip,
              w_glu, b_glu, w_out, ln_g, ln_b, tm):
    tp, d = h.shape
    n_batch = tp // lp
    m_tab, p_tab, r_tab, a16_re, a16_im = _s5_tables(lam_re, lam_im, log_step, b_re, b_im,
                                                     c_re, c_im, d_skip)
    u = _matmul(h, w_in.astype(BF16), F32, tm, d)
    n_chunks = tp // S5_CHUNK
    chunks_per_batch = lp // S5_CHUNK
    pad_chunks = pad // S5_CHUNK
    tr = _tile(n_chunks, 344)
    st_w = 2 * S5_OCT * S5_STATE
    kw = dict(tr=tr, chunks_per_batch=chunks_per_batch, pad_chunks=pad_chunks)
    s_all = pl.pallas_call(
        functools.partial(_s5_state_kernel, **kw),
        grid=(N_OCT, n_chunks // tr),
        in_specs=[pl.BlockSpec((tr * S5_CHUNK, LANES), lambda o, i: (i, o)),
                  pl.BlockSpec((1, S5_CHUNK // 2, 2 * LANES, st_w), lambda o, i: (o, 0, 0, 0))],
        out_specs=pl.BlockSpec((tr, st_w), lambda o, i: (i, o)),
        out_shape=jax.ShapeDtypeStruct((n_chunks, N_OCT * st_w), F32),
        compiler_params=_cparams("parallel", "parallel"),
        name="s5_state",
    )(u, p_tab.reshape(N_OCT, S5_CHUNK // 2, 2 * LANES, st_w))
    tc = _tile(chunks_per_batch, 129, 1)
    nt = chunks_per_batch // tc
    x_prev = pl.pallas_call(
        functools.partial(_s5_scan_kernel, tc=tc),
        grid=(n_batch, nt),
        in_specs=[pl.BlockSpec((tc, N_OCT, st_w), lambda b, i: (b * nt + i, 0, 0)),
                  pl.BlockSpec((N_OCT, st_w // 2), lambda b, i: (0, 0)),
                  pl.BlockSpec((N_OCT, st_w // 2), lambda b, i: (0, 0))],
        out_specs=pl.BlockSpec((tc, N_OCT, st_w), lambda b, i: (b * nt + i, 0, 0)),
        out_shape=jax.ShapeDtypeStruct((n_chunks, N_OCT, st_w), F32),
        scratch_shapes=[pltpu.VMEM((N_OCT, st_w), F32)],
        compiler_params=_cparams("arbitrary", "arbitrary"),
        name="s5_scan",
    )(s_all.reshape(n_chunks, N_OCT, st_w), a16_re, a16_im)
    z = pl.pallas_call(
        functools.partial(_s5_out_kernel, **kw),
        grid=(N_OCT, n_chunks // tr),
        in_specs=[pl.BlockSpec((tr * S5_CHUNK, LANES), lambda o, i: (i, o)),
                  pl.BlockSpec((tr, st_w), lambda o, i: (i, o)),
                  pl.BlockSpec((1, S5_CHUNK // 2, 2 * LANES, S5_CHUNK * LANES), lambda o, i: (o, 0, 0, 0)),
                  pl.BlockSpec((1, st_w, S5_CHUNK * LANES), lambda o, i: (o, 0, 0))],
        out_specs=pl.BlockSpec((tr * S5_CHUNK, LANES), lambda o, i: (i, o)),
        out_shape=jax.ShapeDtypeStruct((tp, d), F32),
        scratch_shapes=[pltpu.VMEM((tr, S5_CHUNK * LANES), F32)],
        compiler_params=_cparams("parallel", "parallel"),
        name="s5_out",
    )(u, x_prev.reshape(n_chunks, N_OCT * st_w),
      m_tab.reshape(N_OCT, S5_CHUNK // 2, 2 * LANES, S5_CHUNK * LANES), r_tab)
    row = lambda i: (i, 0)
    fix = lambda i: (0, 0)
    return pl.pallas_call(
        _s5_tail_kernel,
        grid=(tp // tm,),
        in_specs=[pl.BlockSpec((tm, d), row), pl.BlockSpec((d, d), fix), pl.BlockSpec((1, d), fix),
                  pl.BlockSpec((d, d), fix), pl.BlockSpec((tm, d), row),
                  pl.BlockSpec((1, d), fix), pl.BlockSpec((1, d), fix)],
        out_specs=pl.BlockSpec((tm, d), row),
        out_shape=jax.ShapeDtypeStruct((tp, d), F32),
        compiler_params=_cparams("parallel"),
        name="s5_tail",
    )(z, w_glu.astype(BF16), b_glu.reshape(1, d).astype(F32), w_out.astype(BF16), h,
      ln_g.reshape(1, d), ln_b.reshape(1, d))


def _ret_kernel(q_ref, k_ref, v_ref, gate_ref, cos_ref, sin_ref, dm_ref, qd_ref, kd_ref, cd_ref, g_ref,
                o_ref, st_ref, *, chunk, pad):
    c = pl.program_id(2)

    @pl.when(c == 0)
    def _():
        st_ref[...] = jnp.zeros(st_ref.shape, F32)

    cos = cos_ref[...]
    sin = sin_ref[...]
    half = RET_DK // 2

    def rot(t):
        t1 = t[:, :half]
        t2 = t[:, half:]
        return jnp.concatenate([t1 * cos - t2 * sin, t1 * sin + t2 * cos], axis=-1)

    q = rot(q_ref[...])
    k = rot(k_ref[...]) * (RET_DK ** -0.5)
    pos = c * chunk + lax.broadcasted_iota(jnp.int32, (chunk, 1), 0)
    k = jnp.where(pos >= pad, k, 0.0)
    qb = q.astype(BF16)
    vb = v_ref[...].astype(BF16)
    scores = lax.dot_general(qb, k.astype(BF16), (((1,), (1,)), ((), ())),
                             preferred_element_type=F32) * dm_ref[0]
    intra = jnp.dot(scores.astype(BF16), vb, preferred_element_type=F32)
    state = st_ref[...]
    inter = jnp.dot(qb, state.astype(BF16), preferred_element_type=F32) * qd_ref[0]
    st_ref[...] = state * cd_ref[0] + lax.dot_general((k * kd_ref[0]).astype(BF16), vb,
                                                      (((0,), (0,)), ((), ())),
                                                      preferred_element_type=F32)
    o = intra + inter
    mu = jnp.mean(o, axis=-1, keepdims=True)
    oc = o - mu
    var = jnp.mean(oc * oc, axis=-1, keepdims=True)
    o = oc * lax.rsqrt(var + LN_EPS) * g_ref[...]
    gate = gate_ref[...]
    o_ref[...] = (gate * jax.nn.sigmoid(gate) * o).astype(o_ref.dtype)


def _retention_mixer(h, lp, pad, w_in, gn_g, w_out, ln_g, ln_b, tm):
    tp, d = h.shape
    n_batch = tp // lp
    chunk = _tile(lp, 384, SEQ_ALIGN)
    ncb = lp // chunk
    qd = RET_HEADS * RET_DK
    vd = RET_HEADS * RET_DV
    proj = _matmul(h, w_in.astype(BF16), F32, tm, _tile(w_in.shape[1], 1536, LANES))
    half = RET_DK // 2
    pos = jnp.arange(lp, dtype=F32) - pad
    inv_freq = jnp.power(RET_THETA, -jnp.arange(half, dtype=F32) / half)
    ang = pos[:, None] * inv_freq[None, :]
    cos, sin = jnp.cos(ang), jnp.sin(ang)
    log_gamma = jnp.log1p(-jnp.power(2.0, -5.0 - jnp.arange(RET_HEADS, dtype=F32)))
    idx = jnp.arange(chunk, dtype=F32)
    rel = idx[:, None] - idx[None, :]
    dmask = jnp.where(rel[None] >= 0, jnp.exp(log_gamma[:, None, None] * jnp.maximum(rel, 0.0)[None]), 0.0)
    q_decay = jnp.exp(log_gamma[:, None] * (idx[None, :] + 1.0))[:, :, None]
    k_decay = jnp.exp(log_gamma[:, None] * (chunk - 1.0 - idx[None, :]))[:, :, None]
    chunk_decay = jnp.exp(log_gamma * chunk).reshape(RET_HEADS, 1, 1)
    kb = qd // RET_DK
    vb = 2 * qd // RET_DV
    gb = (2 * qd + vd) // RET_DV
    o = pl.pallas_call(
        functools.partial(_ret_kernel, chunk=chunk, pad=pad),
        grid=(n_batch, RET_HEADS, ncb),
        in_specs=[pl.BlockSpec((chunk, RET_DK), lambda b, hh, c: (b * ncb + c, hh)),
                  pl.BlockSpec((chunk, RET_DK), lambda b, hh, c: (b * ncb + c, kb + hh)),
                  pl.BlockSpec((chunk, RET_DV), lambda b, hh, c: (b * ncb + c, vb + hh)),
                  pl.BlockSpec((chunk, RET_DV), lambda b, hh, c: (b * ncb + c, gb + hh)),
                  pl.BlockSpec((chunk, half), lambda b, hh, c: (c, 0)),
                  pl.BlockSpec((chunk, half), lambda b, hh, c: (c, 0)),
                  pl.BlockSpec((1, chunk, chunk), lambda b, hh, c: (hh, 0, 0)),
                  pl.BlockSpec((1, chunk, 1), lambda b, hh, c: (hh, 0, 0)),
                  pl.BlockSpec((1, chunk, 1), lambda b, hh, c: (hh, 0, 0)),
                  pl.BlockSpec((1, 1, 1), lambda b, hh, c: (hh, 0, 0)),
                  pl.BlockSpec((1, RET_DV), lambda b, hh, c: (0, hh))],
        out_specs=pl.BlockSpec((chunk, RET_DV), lambda b, hh, c: (b * ncb + c, hh)),
        out_shape=jax.ShapeDtypeStruct((tp, vd), BF16),
        scratch_shapes=[pltpu.VMEM((RET_DK, RET_DV), F32)],
        compiler_params=_cparams("parallel", "parallel", "arbitrary"),
        name="retention",
    )(proj, proj, proj, proj, cos, sin, dmask, q_decay, k_decay, chunk_decay,
      gn_g.reshape(1, vd).astype(F32))
    return _matmul_res_ln(o, w_out.astype(BF16), h, ln_g, ln_b, tm)


ONES_ROWS = 16


def _diff_kernel(q_ref, k_ref, vt_ref, lam_ref, g_ref, o_ref, vext_ref, m_ref, acc_ref, s0_ref, s1_ref,
                 s2_ref, s3_ref, mx_ref,
                 *, tq, pad, lambda_init):
    i = pl.program_id(2)
    nkb = vext_ref.shape[0]

    @pl.when(i == 0)
    def _():
        for jb in range(nkb):
            vext_ref[jb, 0:DIFF_DV, :] = vt_ref[:, jb * tq:(jb + 1) * tq]
            vext_ref[jb, DIFF_DV:, :] = jnp.ones((ONES_ROWS, tq), BF16)

    q = q_ref[...]
    lane = lax.broadcasted_iota(jnp.int32, q.shape, 1)
    zero = jnp.zeros_like(q)
    q_both = jnp.concatenate([jnp.where(lane < DIFF_DQK, q, zero), jnp.where(lane >= DIFF_DQK, q, zero)],
                             axis=0)
    m_ref[...] = jnp.full(m_ref.shape, MASK_VALUE, F32)
    acc_ref[...] = jnp.zeros(acc_ref.shape, F32)

    def scores(j):
        start = j * tq if isinstance(j, int) else pl.multiple_of(j * tq, tq)
        k = k_ref[pl.ds(start, tq), :]
        return lax.dot_general(k, q_both, (((1,), (1,)), ((), ())), preferred_element_type=F32)

    s_refs = (s0_ref, s1_ref, s2_ref, s3_ref)

    def produce(slot, j):
        s = scores(j)
        s_refs[slot][...] = s.astype(BF16)
        mx_ref[slot] = jnp.max(s, axis=0, keepdims=True)

    def running_max(block_max):
        return jnp.maximum(m_ref[...], block_max).astype(BF16).astype(F32)

    def accumulate(m_new, pe, v):
        m_prev = m_ref[...]
        acc_ref[...] = jnp.exp2(m_prev - m_new) * acc_ref[...] + jnp.dot(v, pe, preferred_element_type=F32)
        m_ref[...] = m_new

    def consume_masked(s, j):
        kpos = j * tq + lax.broadcasted_iota(jnp.int32, (tq, tq), 0)
        qpos = i * tq + lax.broadcasted_iota(jnp.int32, (tq, tq), 1)
        allowed = (kpos <= qpos) & (kpos >= pad)
        s = jnp.where(jnp.concatenate([allowed, allowed], axis=1), s, MASK_VALUE)
        m_new = running_max(jnp.max(s, axis=0, keepdims=True))
        accumulate(m_new, jnp.exp2((s - m_new).astype(BF16)), vext_ref[j])

    def consume(slot, j):
        m_new = running_max(mx_ref[slot])
        accumulate(m_new, jnp.exp2(s_refs[slot][...] - m_new.astype(BF16)), vext_ref[j])

    def consume2(slot_a, slot_b, j):
        m_new = running_max(jnp.maximum(mx_ref[slot_a], mx_ref[slot_b]))
        m_bf = m_new.astype(BF16)
        pe = jnp.concatenate([jnp.exp2(s_refs[slot_a][...] - m_bf), jnp.exp2(s_refs[slot_b][...] - m_bf)],
                             axis=0)
        accumulate(m_new, pe, jnp.concatenate([vext_ref[j], vext_ref[j + 1]], axis=1))

    @pl.when(i == 0)
    def _():
        consume_masked(scores(0), 0)

    @pl.when(i > 0)
    def _():
        n = i - 1
        n_quads = n // 4
        s3_ref[...] = scores(0).astype(BF16)
        produce(0, 1)
        produce(1, jnp.minimum(2, i))
        consume_masked(s3_ref[...].astype(F32), 0)

        def quad(u, carry):
            a = 1 + 4 * u
            produce(2, a + 2)
            produce(3, a + 3)
            consume2(0, 1, a)
            produce(0, jnp.minimum(a + 4, i))
            produce(1, jnp.minimum(a + 5, i))
            consume2(2, 3, a + 2)
            return carry

        lax.fori_loop(0, n_quads, quad, 0)
        first = 1 + 4 * n_quads

        def pair(jj, carry):
            a = first + 2 * jj
            produce(1, a + 1)
            consume(0, a)
            produce(0, a + 2)
            consume(1, a + 1)
            return carry

        lax.fori_loop(0, (n - 4 * n_quads) // 2, pair, 0)

        @pl.when(n % 2 == 1)
        def _():
            produce(1, i)
            consume(0, i - 1)
            consume_masked(s1_ref[...].astype(F32), i)

        @pl.when(n % 2 == 0)
        def _():
            consume_masked(s0_ref[...].astype(F32), i)

    lam_p = lam_ref[...]
    lam = (jnp.exp(jnp.sum(lam_p[0:1] * lam_p[1:2], axis=-1, keepdims=True))
           - jnp.exp(jnp.sum(lam_p[2:3] * lam_p[3:4], axis=-1, keepdims=True)) + lambda_init)
    acc = acc_ref[...]
    o_both = acc[0:DIFF_DV] / acc[DIFF_DV:DIFF_DV + 1]
    o = o_both[:, :tq] - lam * o_both[:, tq:]
    o = o * lax.rsqrt(jnp.mean(o * o, axis=0, keepdims=True) + LN_EPS)
    o = o * (g_ref[...] * (1.0 - lambda_init))
    o_ref[...] = o.T.astype(o_ref.dtype)


def _mm_nt_kernel(wt_ref, x_ref, o_ref):
    o_ref[...] = lax.dot_general(wt_ref[...], x_ref[...].astype(BF16), (((1,), (1,)), ((), ())),
                                 preferred_element_type=F32).astype(o_ref.dtype)


def _matmul_nt(wt, x, out_dtype, tm):
    n, k = wt.shape
    m = x.shape[0]
    return pl.pallas_call(
        _mm_nt_kernel,
        grid=(m // tm,),
        in_specs=[pl.BlockSpec((n, k), lambda i: (0, 0)), pl.BlockSpec((tm, k), lambda i: (i, 0))],
        out_specs=pl.BlockSpec((n, tm), lambda i: (0, i)),
        out_shape=jax.ShapeDtypeStruct((n, m), out_dtype),
        compiler_params=_cparams("parallel"),
        name="matmul_nt",
    )(wt, x)


def _diff_attn_mixer(h, lp, pad, w_in, lq1, lk1, lq2, lk2, subln_g, w_out, lambda_init, ln_g, ln_b, tm):
    tp, d = h.shape
    n_batch = tp // lp
    qk = DIFF_HEADS * 2 * DIFF_DQK
    col_scale = jnp.concatenate([jnp.full((qk,), DIFF_DQK ** -0.5 * math.log2(math.e), F32),
                                 jnp.ones((qk,), F32)])
    proj = _matmul(h, (w_in[:, :2 * qk] * col_scale[None, :]).astype(BF16), BF16, tm, _tile(2 * qk, 2048, LANES))
    vt = _matmul_nt(w_in[:, 2 * qk:].T.astype(BF16), h, BF16, _tile(tp, 768, LANES))
    tq = _tile(lp, 384, SEQ_ALIGN)
    nq = lp // tq
    lam_p = jnp.stack([lq1, lk1, lq2, lk2]).astype(F32)
    o = pl.pallas_call(
        functools.partial(_diff_kernel, tq=tq, pad=pad, lambda_init=lambda_init),
        grid=(n_batch, DIFF_HEADS, nq),
        in_specs=[pl.BlockSpec((tq, LANES), lambda b, hh, i: (b * nq + i, hh)),
                  pl.BlockSpec((lp, LANES), lambda b, hh, i: (b, DIFF_HEADS + hh)),
                  pl.BlockSpec((DIFF_DV, lp), lambda b, hh, i: (hh, b)),
                  pl.BlockSpec((4, DIFF_DQK), lambda b, hh, i: (0, 0)),
                  pl.BlockSpec((DIFF_DV, 1), lambda b, hh, i: (0, 0))],
        out_specs=pl.BlockSpec((tq, DIFF_DV), lambda b, hh, i: (b * nq + i, hh)),
        out_shape=jax.ShapeDtypeStruct((tp, DIFF_HEADS * DIFF_DV), BF16),
        scratch_shapes=[pltpu.VMEM((nq, DIFF_DV + ONES_ROWS, tq), BF16),
                        pltpu.VMEM((1, 2 * tq), F32),
                        pltpu.VMEM((DIFF_DV + ONES_ROWS, 2 * tq), F32),
                        ] + [pltpu.VMEM((tq, 2 * tq), BF16)] * 4 + [pltpu.VMEM((4, 1, 2 * tq), F32)],
        compiler_params=_cparams("parallel", "parallel", "arbitrary"),
        name="diff_attn",
    )(proj, proj, vt, lam_p, subln_g.reshape(DIFF_DV, 1).astype(F32))
    return _matmul_res_ln(o, w_out.astype(BF16), h, ln_g, ln_b, tm)


SORT_ALIGN = 16
SORT_CHUNK = 512


def _router_kernel(x_ref, w_ref, b_ref, idx_ref, gate_ref, rank_ref, cnt_ref, *, tm):
    x = x_ref[...]
    x_hi = x.astype(BF16)
    x_lo = (x - x_hi.astype(F32)).astype(BF16)
    hi = jnp.dot(x_hi, w_ref[...], preferred_element_type=F32)
    lo = jnp.dot(x_lo, w_ref[:, :LANES], preferred_element_type=F32)
    logits = hi[:, :LANES] + hi[:, LANES:] + lo + b_ref[...]
    lane = lax.broadcasted_iota(jnp.int32, logits.shape, 1).astype(F32)
    work = jnp.where(lane < N_EXPERTS, logits, -jnp.inf)
    vals, idxs = [], []
    picked = jnp.zeros(logits.shape, F32)
    for _ in range(TOP_K):
        m = jnp.max(work, axis=-1, keepdims=True)
        sel = jnp.min(jnp.where(work == m, lane, float(LANES)), axis=-1, keepdims=True)
        hit = lane == sel
        work = jnp.where(hit, -jnp.inf, work)
        picked = jnp.where(hit, 1.0, picked)
        vals.append(m)
        idxs.append(sel)
    exps = [jnp.exp(v - vals[0]) for v in vals]
    tot = exps[0] + exps[1] + exps[2] + exps[3]
    r = lax.broadcasted_iota(jnp.int32, (tm, tm), 0)
    c = lax.broadcasted_iota(jnp.int32, (tm, tm), 1)
    lower = jnp.where(c < r, 1.0, 0.0).astype(BF16)
    before = jnp.dot(lower, picked.astype(BF16), preferred_element_type=F32)
    for kk in range(TOP_K):
        idx_ref[:, kk:kk + 1] = idxs[kk].astype(jnp.int32)
        gate_ref[:, kk:kk + 1] = exps[kk] / tot
        rank_ref[:, kk:kk + 1] = jnp.sum(jnp.where(lane == idxs[kk], before, 0.0), axis=-1,
                                         keepdims=True).astype(jnp.int32)
    cnt_ref[0] = jnp.sum(picked, axis=0, keepdims=True)


BIG_CHUNKS = 4


def _row_dmas(n_chunks, make_copy, wait):
    n_big = n_chunks // BIG_CHUNKS
    big = BIG_CHUNKS * SORT_ALIGN

    def run(cp):
        if wait:
            cp.wait()
        else:
            cp.start()

    def big_body(c, carry):
        run(make_copy(c * big, big))
        return carry

    def small_body(c, carry):
        run(make_copy(n_big * big + c * SORT_ALIGN, SORT_ALIGN))
        return carry

    lax.fori_loop(0, n_big, big_body, 0)
    lax.fori_loop(0, n_chunks - n_big * BIG_CHUNKS, small_body, 0)


def _segment_dma(tab_ref, n_seg, tile, local_ref, slots_ref, sem, to_slots, wait):
    def per_expert(e, carry):
        lo0 = tab_ref[tile * N_EXPERTS + e]
        go0 = tab_ref[n_seg + tile * N_EXPERTS + e]

        def make_copy(row, rows):
            local = local_ref.at[pl.ds(pl.multiple_of(lo0 + row, SORT_ALIGN), rows), :]
            slots = slots_ref.at[pl.ds(pl.multiple_of(go0 + row, SORT_ALIGN), rows), :]
            return (pltpu.make_async_copy(local, slots, sem) if to_slots
                    else pltpu.make_async_copy(slots, local, sem))

        _row_dmas(tab_ref[2 * n_seg + tile * N_EXPERTS + e], make_copy, wait)
        return carry
    lax.fori_loop(0, N_EXPERTS, per_expert, 0)


def _zero_fill_dma(tab_ref, n_seg, n_blocks, zero_ref, slots_ref, sem, wait):
    bm = zero_ref.shape[0]

    def per_expert(e, carry):
        off = tab_ref[3 * n_seg + e]

        def make_copy(row, rows):
            dst = slots_ref.at[pl.ds(pl.multiple_of(off + row, SORT_ALIGN), rows), :]
            return pltpu.make_async_copy(zero_ref.at[pl.ds(0, rows), :], dst, sem)

        _row_dmas(tab_ref[3 * n_seg + N_EXPERTS + e], make_copy, wait)
        return carry
    lax.fori_loop(0, N_EXPERTS, per_expert, 0)

    def per_block(blk, carry):
        cp = pltpu.make_async_copy(zero_ref, slots_ref.at[pl.ds(pl.multiple_of(blk * bm, bm), bm), :], sem)
        if wait:
            cp.wait()
        else:
            cp.start()
        return carry
    lax.fori_loop(tab_ref[3 * n_seg + 2 * N_EXPERTS], n_blocks, per_block, 0)


def _dispatch_kernel(tab_ref, x_ref, lrow_ref, slots_ref, sorted_ref, zero_ref, sem, zero_sem,
                     *, tm, rs, n_blocks):
    i = pl.program_id(0)
    n_seg = pl.num_programs(0) * N_EXPERTS
    slot = i % 2
    buf = sorted_ref.at[slot]

    @pl.when(i == 0)
    def _():
        zero_ref[...] = jnp.zeros(zero_ref.shape, BF16)
        _zero_fill_dma(tab_ref, n_seg, n_blocks, zero_ref, slots_ref, zero_sem, wait=False)
        _zero_fill_dma(tab_ref, n_seg, n_blocks, zero_ref, slots_ref, zero_sem, wait=True)

    xb = x_ref[...].astype(BF16)
    lr = lrow_ref[...]
    for rc in range(rs // SORT_CHUNK):
        rows = rc * SORT_CHUNK + lax.broadcasted_iota(jnp.int32, (SORT_CHUNK, tm), 0)
        hit = (rows == lr[0:1]) | (rows == lr[1:2]) | (rows == lr[2:3]) | (rows == lr[3:4])
        perm = jnp.where(hit, 1.0, 0.0).astype(BF16)
        buf[rc * SORT_CHUNK:(rc + 1) * SORT_CHUNK, :] = jnp.dot(
            perm, xb, preferred_element_type=F32).astype(BF16)
    _segment_dma(tab_ref, n_seg, i, buf, slots_ref, sem.at[slot], True, wait=False)

    @pl.when(i > 0)
    def _():
        _segment_dma(tab_ref, n_seg, i - 1, sorted_ref.at[1 - slot], slots_ref, sem.at[1 - slot], True, wait=True)

    @pl.when(i == pl.num_programs(0) - 1)
    def _():
        _segment_dma(tab_ref, n_seg, i, buf, slots_ref, sem.at[slot], True, wait=True)


def _combine_kernel(tab_ref, res_ref, lrow_ref, gate_ref, g_ref, b_ref, y_ref, o_ref, ysort_ref, sem,
                    *, tm, rs):
    i = pl.program_id(0)
    n_seg = pl.num_programs(0) * N_EXPERTS
    slot = i % 2

    @pl.when(i == 0)
    def _():
        ysort_ref[...] = jnp.zeros(ysort_ref.shape, BF16)
        _segment_dma(tab_ref, n_seg, 0, ysort_ref.at[0], y_ref, sem.at[0], False, wait=False)

    @pl.when(i + 1 < pl.num_programs(0))
    def _():
        _segment_dma(tab_ref, n_seg, i + 1, ysort_ref.at[1 - slot], y_ref, sem.at[1 - slot], False, wait=False)

    _segment_dma(tab_ref, n_seg, i, ysort_ref.at[slot], y_ref, sem.at[slot], False, wait=True)
    ysort = ysort_ref.at[slot]
    lr = lrow_ref[...]
    gt = gate_ref[...]
    y = jnp.zeros(res_ref.shape, F32)
    for rc in range(rs // SORT_CHUNK):
        cols = rc * SORT_CHUNK + lax.broadcasted_iota(jnp.int32, (tm, SORT_CHUNK), 1)
        w = jnp.zeros((tm, SORT_CHUNK), F32)
        for kk in range(TOP_K):
            w = jnp.where(cols == lr[:, kk:kk + 1], gt[:, kk:kk + 1], w)
        y += jnp.dot(w.astype(BF16), ysort[rc * SORT_CHUNK:(rc + 1) * SORT_CHUNK, :],
                     preferred_element_type=F32)
    o_ref[...] = _layer_norm_rows(DEEPNORM_ALPHA * res_ref[...] + y, g_ref[...], b_ref[...])


def _expert_kernel(be_ref, nu_ref, x_ref, wu_ref, bu_ref, wd_ref, bd_ref, o_ref, wu_bf_ref, wd_bf_ref):
    i = pl.program_id(0)
    live = i < nu_ref[0]

    @pl.when(live & ((i == 0) | (be_ref[i] != be_ref[jnp.maximum(i - 1, 0)])))
    def _():
        wu_bf_ref[...] = wu_ref[0, 0].astype(BF16)
        wd_bf_ref[...] = wd_ref[0, 0].astype(BF16)

    @pl.when(live)
    def _():
        hb = jnp.dot(x_ref[...], wu_bf_ref[...], preferred_element_type=F32) + bu_ref[0]
        x_glu = jnp.minimum(hb[:, :D_EXPERT], SWIGLU_LIMIT)
        x_lin = jnp.clip(hb[:, D_EXPERT:], -SWIGLU_LIMIT, SWIGLU_LIMIT)
        act = x_glu * jax.nn.sigmoid(SWIGLU_ALPHA * x_glu) * (x_lin + 1.0)
        y = jnp.dot(act.astype(BF16), wd_bf_ref[...], preferred_element_type=F32) + bd_ref[0]
        o_ref[...] = y.astype(o_ref.dtype)

    @pl.when(jnp.logical_not(live))
    def _():
        o_ref[...] = jnp.zeros(o_ref.shape, o_ref.dtype)


def _moe_ffn(h, layer, w_router, b_router, w_up_all, b_up, w_down_all, b_down, ln_g, ln_b, tm, bm):
    tp, d = h.shape
    a = tp * TOP_K
    wr = jnp.zeros((d, LANES), F32).at[:, :N_EXPERTS].set(w_router.astype(F32))
    wr_hi = wr.astype(BF16)
    wr = jnp.concatenate([wr_hi, (wr - wr_hi.astype(F32)).astype(BF16)], axis=1)
    br = jnp.zeros((1, LANES), F32).at[0, :N_EXPERTS].set(b_router.astype(F32))
    row = lambda i: (i, 0)
    fix = lambda i: (0, 0)
    nt = tp // tm
    idx, gates, rank, cnt = pl.pallas_call(
        functools.partial(_router_kernel, tm=tm),
        grid=(nt,),
        in_specs=[pl.BlockSpec((tm, d), row), pl.BlockSpec((d, 2 * LANES), fix), pl.BlockSpec((1, LANES), fix)],
        out_specs=[pl.BlockSpec((tm, TOP_K), row), pl.BlockSpec((tm, TOP_K), row),
                   pl.BlockSpec((tm, TOP_K), row), pl.BlockSpec((1, 1, LANES), lambda i: (i, 0, 0))],
        out_shape=[jax.ShapeDtypeStruct((tp, TOP_K), jnp.int32), jax.ShapeDtypeStruct((tp, TOP_K), F32),
                   jax.ShapeDtypeStruct((tp, TOP_K), jnp.int32), jax.ShapeDtypeStruct((nt, 1, LANES), F32)],
        compiler_params=_cparams("parallel"),
        name="router",
    )(h, wr, br)
    counts = cnt[:, 0, :N_EXPERTS].astype(jnp.int32)
    seg = ((counts + SORT_ALIGN - 1) // SORT_ALIGN) * SORT_ALIGN
    tile_off = jnp.cumsum(seg, axis=0) - seg
    region = ((jnp.sum(seg, axis=0) + bm - 1) // bm) * bm
    region_end = jnp.cumsum(region)
    slot_off = (region_end - region)[None, :] + tile_off
    local_off = jnp.cumsum(seg, axis=1) - seg
    used = jnp.sum(seg, axis=0)
    n_used = (region_end[-1:] // bm).astype(jnp.int32)
    table = jnp.concatenate([local_off.reshape(-1), slot_off.reshape(-1), (seg // SORT_ALIGN).reshape(-1),
                             region_end - region + used, (region - used) // SORT_ALIGN,
                             n_used]).astype(jnp.int32)
    onehot = idx.reshape(nt, tm, TOP_K, 1) == jnp.arange(N_EXPERTS, dtype=jnp.int32)
    lrow = jnp.sum(jnp.where(onehot, local_off[:, None, None, :], 0), axis=-1).reshape(tp, TOP_K) + rank
    lrow_t = jnp.concatenate([lrow.T, jnp.full((8 - TOP_K, tp), -1, jnp.int32)], axis=0)
    rs = -(-(TOP_K * tm + N_EXPERTS * (SORT_ALIGN - 1)) // SORT_CHUNK) * SORT_CHUNK
    n_blocks = -(-(a + nt * N_EXPERTS * (SORT_ALIGN - 1)) // bm) + N_EXPERTS
    n_slots = n_blocks * bm
    block_start = jnp.arange(n_blocks, dtype=jnp.int32) * bm
    block_e = jnp.minimum(jnp.sum((region_end[None, :] <= block_start[:, None]).astype(jnp.int32), axis=1),
                          N_EXPERTS - 1)
    x_slots = pl.pallas_call(
        functools.partial(_dispatch_kernel, tm=tm, rs=rs, n_blocks=n_blocks),
        grid_spec=pltpu.PrefetchScalarGridSpec(
            num_scalar_prefetch=1,
            grid=(nt,),
            in_specs=[pl.BlockSpec((tm, d), lambda i, tab: (i, 0)),
                      pl.BlockSpec((8, tm), lambda i, tab: (0, i))],
            out_specs=pl.BlockSpec(memory_space=pl.ANY),
            scratch_shapes=[pltpu.VMEM((2, rs, d), BF16), pltpu.VMEM((bm, d), BF16),
                            pltpu.SemaphoreType.DMA((2,)), pltpu.SemaphoreType.DMA(())],
        ),
        out_shape=jax.ShapeDtypeStruct((n_slots, d), BF16),
        compiler_params=_cparams("arbitrary"),
        name="moe_dispatch",
    )(table, h, lrow_t)
    y_slots = pl.pallas_call(
        _expert_kernel,
        grid_spec=pltpu.PrefetchScalarGridSpec(
            num_scalar_prefetch=2,
            grid=(n_blocks,),
            in_specs=[pl.BlockSpec((bm, d), lambda i, be, nu: (i, 0)),
                      pl.BlockSpec((1, 1, d, 2 * D_EXPERT), lambda i, be, nu: (layer, be[i], 0, 0)),
                      pl.BlockSpec((1, 1, 2 * D_EXPERT), lambda i, be, nu: (be[i], 0, 0)),
                      pl.BlockSpec((1, 1, D_EXPERT, d), lambda i, be, nu: (layer, be[i], 0, 0)),
                      pl.BlockSpec((1, 1, d), lambda i, be, nu: (be[i], 0, 0))],
            out_specs=pl.BlockSpec((bm, d), lambda i, be, nu: (i, 0)),
            scratch_shapes=[pltpu.VMEM((d, 2 * D_EXPERT), BF16), pltpu.VMEM((D_EXPERT, d), BF16)],
        ),
        out_shape=jax.ShapeDtypeStruct((n_slots, d), BF16),
        compiler_params=_cparams("arbitrary"),
        name="experts",
    )(block_e, n_used, x_slots, w_up_all, b_up.reshape(N_EXPERTS, 1, 2 * D_EXPERT).astype(F32),
      w_down_all, b_down.reshape(N_EXPERTS, 1, d).astype(F32))
    return pl.pallas_call(
        functools.partial(_combine_kernel, tm=tm, rs=rs),
        grid_spec=pltpu.PrefetchScalarGridSpec(
            num_scalar_prefetch=1,
            grid=(nt,),
            in_specs=[pl.BlockSpec((tm, d), lambda i, tab: (i, 0)),
                      pl.BlockSpec((tm, TOP_K), lambda i, tab: (i, 0)),
                      pl.BlockSpec((tm, TOP_K), lambda i, tab: (i, 0)),
                      pl.BlockSpec((1, d), lambda i, tab: (0, 0)),
                      pl.BlockSpec((1, d), lambda i, tab: (0, 0)),
                      pl.BlockSpec(memory_space=pl.ANY)],
            out_specs=pl.BlockSpec((tm, d), lambda i, tab: (i, 0)),
            scratch_shapes=[pltpu.VMEM((2, rs, d), BF16), pltpu.SemaphoreType.DMA((2,))],
        ),
        out_shape=jax.ShapeDtypeStruct((tp, d), F32),
        compiler_params=_cparams("arbitrary"),
        name="moe_combine_ln",
    )(table, h, lrow, gates, ln_g.reshape(1, d), ln_b.reshape(1, d), y_slots)


def kernel(x, meta_tokens, s5_w_in, s5_lambda_re, s5_lambda_im, s5_log_step, s5_b_re, s5_b_im, s5_c_re, s5_c_im, s5_d, s5_w_glu, s5_b_glu, s5_w_out, ret_w_in, ret_gn_g, ret_w_out, diff_w_in, diff_lambda_q1, diff_lambda_k1, diff_lambda_q2, diff_lambda_k2, diff_subln_g, diff_w_out, ln_mix_g, ln_mix_b, moe_w_router, moe_b_router, moe_w_up, moe_b_up, moe_w_down, moe_b_down, ln_ffn_g, ln_ffn_b):
    n_batch, seq, d = x.shape
    length = seq + N_META
    pad = (-length) % SEQ_ALIGN
    lp = length + pad
    assert pad % S5_CHUNK == 0 and d == D_MODEL
    tp = n_batch * lp
    tm = _tile(tp, 768)
    meta = jnp.broadcast_to(meta_tokens[None].astype(x.dtype), (n_batch, N_META, d))
    h = jnp.concatenate([jnp.zeros((n_batch, pad, d), x.dtype), meta, x], axis=1).reshape(tp, d)
    for i in range(DEPTH):
        kind = i % N_MIXERS
        j = i // N_MIXERS
        if kind == 0:
            h = _s5_mixer(h, lp, pad, s5_w_in[j], s5_lambda_re[j], s5_lambda_im[j], s5_log_step[j],
                          s5_b_re[j], s5_b_im[j], s5_c_re[j], s5_c_im[j], s5_d[j],
                          s5_w_glu[j], s5_b_glu[j], s5_w_out[j], ln_mix_g[i], ln_mix_b[i], tm)
        elif kind == 1:
            h = _retention_mixer(h, lp, pad, ret_w_in[j], ret_gn_g[j], ret_w_out[j],
                                 ln_mix_g[i], ln_mix_b[i], tm)
        else:
            lambda_init = 0.8 - 0.6 * math.exp(-0.3 * i)
            h = _diff_attn_mixer(h, lp, pad, diff_w_in[j], diff_lambda_q1[j], diff_lambda_k1[j],
                                 diff_lambda_q2[j], diff_lambda_k2[j], diff_subln_g[j], diff_w_out[j],
                                 lambda_init, ln_mix_g[i], ln_mix_b[i], tm)
        h = _moe_ffn(h, i, moe_w_router[i], moe_b_router[i], moe_w_up, moe_b_up[i],
                     moe_w_down, moe_b_down[i], ln_ffn_g[i], ln_ffn_b[i], tm, MOE_BLOCK)
    return h.reshape(n_batch, lp, d)[:, pad + N_META:, :]
```

```python
import functools
import math

import jax
import jax.numpy as jnp
from jax import lax
from jax.experimental import pallas as pl
from jax.experimental.pallas import tpu as pltpu

F32 = jnp.float32
BF16 = jnp.bfloat16

D_MODEL = 1024
DEPTH = 4
N_META = 16
N_MIXERS = 3
S5_GROUP_CH = 16
S5_GROUPS = D_MODEL // S5_GROUP_CH
S5_STATE = 64
S5_CHUNK = 16
S5_OCT = 8
N_OCT = S5_GROUPS // S5_OCT
RET_HEADS = 4
RET_DK = D_MODEL // RET_HEADS
RET_DV = 2 * RET_DK
RET_THETA = 10000.0
DIFF_HEADS = D_MODEL // 128
DIFF_DQK = 64
DIFF_DV = 128
N_EXPERTS = 32
TOP_K = 4
D_EXPERT = D_MODEL
SWIGLU_LIMIT = 7.0
SWIGLU_ALPHA = 1.702
MOE_BLOCK = 512
LN_EPS = 1e-5
MASK_VALUE = -(2.0 ** 100)
DEEPNORM_ALPHA = (2 * DEPTH) ** 0.25
SEQ_ALIGN = 128
LANES = 128
VMEM_LIMIT = 56 * 1024 * 1024


def _cparams(*sem):
    return pltpu.CompilerParams(dimension_semantics=sem, vmem_limit_bytes=VMEM_LIMIT)


def _tile(n, target, mult=8):
    best = None
    for t in range(mult, min(n, target) + 1, mult):
        if n % t == 0:
            best = t
    assert best is not None, (n, target, mult)
    return best


def _layer_norm_rows(v, g, b):
    mu = jnp.mean(v, axis=-1, keepdims=True)
    c = v - mu
    var = jnp.mean(c * c, axis=-1, keepdims=True)
    return c * lax.rsqrt(var + LN_EPS) * g + b


def _mm_kernel(x_ref, w_ref, o_ref):
    o_ref[...] = jnp.dot(x_ref[...].astype(BF16), w_ref[...],
                         preferred_element_type=F32).astype(o_ref.dtype)


def _matmul(x, w, out_dtype, tm, tn):
    m, k = x.shape
    n = w.shape[1]
    return pl.pallas_call(
        _mm_kernel,
        grid=(n // tn, m // tm),
        in_specs=[pl.BlockSpec((tm, k), lambda j, i: (i, 0)),
                  pl.BlockSpec((k, tn), lambda j, i: (0, j))],
        out_specs=pl.BlockSpec((tm, tn), lambda j, i: (i, j)),
        out_shape=jax.ShapeDtypeStruct((m, n), out_dtype),
        compiler_params=_cparams("parallel", "parallel"),
        name="matmul",
    )(x, w)


def _mm_res_ln_kernel(x_ref, w_ref, res_ref, g_ref, b_ref, o_ref):
    y = jnp.dot(x_ref[...].astype(BF16), w_ref[...], preferred_element_type=F32)
    o_ref[...] = _layer_norm_rows(DEEPNORM_ALPHA * res_ref[...] + y, g_ref[...], b_ref[...])


def _matmul_res_ln(x, w, res, g, b, tm):
    m, k = x.shape
    n = w.shape[1]
    row = lambda i: (i, 0)
    fix = lambda i: (0, 0)
    return pl.pallas_call(
        _mm_res_ln_kernel,
        grid=(m // tm,),
        in_specs=[pl.BlockSpec((tm, k), row), pl.BlockSpec((k, n), fix), pl.BlockSpec((tm, n), row),
                  pl.BlockSpec((1, n), fix), pl.BlockSpec((1, n), fix)],
        out_specs=pl.BlockSpec((tm, n), row),
        out_shape=jax.ShapeDtypeStruct((m, n), F32),
        compiler_params=_cparams("parallel"),
        name="matmul_res_ln",
    )(x, w, res, g.reshape(1, n), b.reshape(1, n))


def _expand_block_diag(a, row_inner, outer, inner):
    rows, k = a.shape
    cols = outer * S5_OCT * inner
    r = jnp.arange(k)[:, None]
    c = jnp.arange(cols)[None, :]
    rep = ((r // inner == c // (S5_OCT * inner)) & (r % inner == c % inner)).astype(BF16)
    out = jnp.dot(a.astype(BF16), rep, preferred_element_type=BF16)
    g_row = (jnp.arange(rows)[:, None] // row_inner) % S5_OCT
    h_col = (c // inner) % S5_OCT
    return jnp.where(g_row == h_col, out, jnp.zeros_like(out))


def _s5_tables(lam_re, lam_im, log_step, b_re, b_im, c_re, c_im, d_skip):
    hp = lax.Precision.HIGHEST
    g_, n_, c_, q_, o_ = S5_GROUPS, S5_STATE, S5_GROUP_CH, S5_CHUNK, S5_OCT
    dt = jnp.exp(log_step.astype(F32))[:, None]
    lr = jnp.minimum(lam_re.astype(F32), -1e-4)
    li = lam_im.astype(F32)
    mag = jnp.exp(lr * dt)
    ab_re = mag * jnp.cos(li * dt)
    ab_im = mag * jnp.sin(li * dt)
    den = lr * lr + li * li
    f_re = ((ab_re - 1.0) * lr + ab_im * li) / den
    f_im = (ab_im * lr - (ab_re - 1.0) * li) / den
    br, bi = b_re.astype(F32), b_im.astype(F32)
    bbar_re = f_re[..., None] * br - f_im[..., None] * bi
    bbar_im = f_re[..., None] * bi + f_im[..., None] * br
    j = jnp.arange(q_ + 1, dtype=F32)[:, None, None]
    pmag = jnp.exp(j * (lr * dt)[None])
    pw_re = pmag * jnp.cos(j * (li * dt)[None])
    pw_im = pmag * jnp.sin(j * (li * dt)[None])
    cr_t = c_re.astype(F32).transpose(0, 2, 1)
    ci_t = c_im.astype(F32).transpose(0, 2, 1)
    bre_t = bbar_re.transpose(0, 2, 1)
    bim_t = bbar_im.transpose(0, 2, 1)
    w_re = pw_re[:, :, None, :] * bre_t[None] - pw_im[:, :, None, :] * bim_t[None]
    w_im = pw_re[:, :, None, :] * bim_t[None] + pw_im[:, :, None, :] * bre_t[None]
    kj = (jnp.einsum('gno,jgin->jgio', cr_t, w_re[:q_], precision=hp)
          - jnp.einsum('gno,jgin->jgio', ci_t, w_im[:q_], precision=hp))
    kj = kj.at[0].add(d_skip.astype(F32).reshape(g_, c_)[:, :, None] * jnp.eye(c_, dtype=F32)[None])
    kj = kj.transpose(1, 2, 0, 3).reshape(g_ * c_, q_ * c_)
    kts = jnp.stack([jnp.pad(kj[:, :(q_ - s) * c_], ((0, 0), (s * c_, 0))) for s in range(q_)], axis=1)
    kts = kts.reshape(N_OCT, o_ * c_, q_, q_ * c_).transpose(0, 2, 1, 3)
    m_tab = _expand_block_diag(kts.reshape(N_OCT * q_ * o_ * c_, q_ * c_), c_, q_, c_)
    m_tab = m_tab.reshape(N_OCT, q_, o_ * c_, q_ * o_ * c_)
    pst = jnp.concatenate([w_re[:q_][::-1], w_im[:q_][::-1]], axis=-1)
    pst = pst.reshape(q_, N_OCT, o_ * c_, 2 * n_).transpose(1, 0, 2, 3)
    p_tab = _expand_block_diag(pst.reshape(N_OCT * q_ * o_ * c_, 2 * n_), c_, 2, n_)
    p_tab = p_tab.reshape(N_OCT, q_, o_ * c_, 2 * o_ * n_)
    ar_t = pw_re[1:].transpose(1, 2, 0)[..., None]
    ai_t = pw_im[1:].transpose(1, 2, 0)[..., None]
    r_re = cr_t[:, :, None, :] * ar_t - ci_t[:, :, None, :] * ai_t
    r_im = -(cr_t[:, :, None, :] * ai_t + ci_t[:, :, None, :] * ar_t)
    rst = jnp.stack([r_re.reshape(N_OCT, o_ * n_, q_ * c_), r_im.reshape(N_OCT, o_ * n_, q_ * c_)], axis=1)
    r_tab = _expand_block_diag(rst.reshape(N_OCT * 2 * o_ * n_, q_ * c_), n_, q_, c_)
    r_tab = r_tab.reshape(N_OCT, 2 * o_ * n_, q_ * o_ * c_)
    a16_re = pw_re[q_].reshape(N_OCT, o_ * n_)
    a16_im = pw_im[q_].reshape(N_OCT, o_ * n_)
    return m_tab, p_tab, r_tab, a16_re, a16_im


def _s5_chunk_rows(u_ref, s, tr, valid):
    us = u_ref[pl.ds(s, tr, stride=S5_CHUNK), :]
    return jnp.where(valid, us, 0.0).astype(BF16)


def _s5_valid(tr, chunks_per_batch, pad_chunks):
    chunk = pl.program_id(1) * tr + lax.broadcasted_iota(jnp.int32, (tr, 1), 0)
    return (chunk % chunks_per_batch) >= pad_chunks


def _s5_state_kernel(u_ref, p_ref, s_ref, *, tr, chunks_per_batch, pad_chunks):
    valid = _s5_valid(tr, chunks_per_batch, pad_chunks)
    acc = jnp.zeros(s_ref.shape, F32)
    for sp in range(S5_CHUNK // 2):
        us = jnp.concatenate([_s5_chunk_rows(u_ref, 2 * sp, tr, valid),
                              _s5_chunk_rows(u_ref, 2 * sp + 1, tr, valid)], axis=1)
        acc += jnp.dot(us, p_ref[0, sp], preferred_element_type=F32)
    s_ref[...] = acc


def _s5_scan_kernel(s_ref, ar_ref, ai_ref, x_ref, st_ref, *, tc):
    half = st_ref.shape[1] // 2

    @pl.when(pl.program_id(1) == 0)
    def _():
        st_ref[...] = jnp.zeros(st_ref.shape, F32)

    ar = ar_ref[...]
    ai = ai_ref[...]

    def body(c, carry):
        xr, xi = carry
        x_ref[c, :, :half] = xr
        x_ref[c, :, half:] = xi
        s = s_ref[c]
        return ar * xr - ai * xi + s[:, :half], ar * xi + ai * xr + s[:, half:]

    xr, xi = lax.fori_loop(0, tc, body, (st_ref[:, :half], st_ref[:, half:]))
    st_ref[:, :half] = xr
    st_ref[:, half:] = xi


def _gelu_tanh(y):
    return 0.5 * y * (1.0 + jnp.tanh(math.sqrt(2.0 / math.pi) * (y + 0.044715 * (y * y * y))))


def _s5_out_kernel(u_ref, xp_ref, m_ref, r_ref, z_ref, acc_ref, *, tr, chunks_per_batch, pad_chunks):
    valid = _s5_valid(tr, chunks_per_batch, pad_chunks)
    acc_ref[...] = jnp.dot(xp_ref[...].astype(BF16), r_ref[0], preferred_element_type=F32)
    for sp in range(S5_CHUNK // 2):
        us = jnp.concatenate([_s5_chunk_rows(u_ref, 2 * sp, tr, valid),
                              _s5_chunk_rows(u_ref, 2 * sp + 1, tr, valid)], axis=1)
        acc_ref[:, 2 * sp * LANES:] += jnp.dot(us, m_ref[0, sp, :, 2 * sp * LANES:],
                                               preferred_element_type=F32)
    for t in range(S5_CHUNK):
        z_ref[pl.ds(t, tr, stride=S5_CHUNK), :] = _gelu_tanh(acc_ref[:, t * LANES:(t + 1) * LANES])


def _s5_tail_kernel(z_ref, wg_ref, bg_ref, wo_ref, res_ref, g_ref, b_ref, o_ref):
    z = z_ref[...]
    t = jnp.dot(z.astype(BF16), wg_ref[...], preferred_element_type=F32) + bg_ref[...]
    glu = z * jax.nn.sigmoid(t)
    y = jnp.dot(glu.astype(BF16), wo_ref[...], preferred_element_type=F32)
    o_ref[...] = _layer_norm_rows(DEEPNORM_ALPHA * res_ref[...] + y, g_ref[...], b_ref[...])


def _s5_mixer(h, lp, pad, w_in, lam_re, lam_im, log_step, b_re, b_im, c_re, c_im, d_skip,
              w_glu, b_glu, w_out, ln_g, ln_b, tm):
    tp, d = h.shape
    n_batch = tp // lp
    m_tab, p_tab, r_tab, a16_re, a16_im = _s5_tables(lam_re, lam_im, log_step, b_re, b_im,
                                                     c_re, c_im, d_skip)
    u = _matmul(h, w_in.astype(BF16), F32, tm, d)
    n_chunks = tp // S5_CHUNK
    chunks_per_batch = lp // S5_CHUNK
    pad_chunks = pad // S5_CHUNK
    tr = _tile(n_chunks, 344)
    st_w = 2 * S5_OCT * S5_STATE
    kw = dict(tr=tr, chunks_per_batch=chunks_per_batch, pad_chunks=pad_chunks)
    s_all = pl.pallas_call(
        functools.partial(_s5_state_kernel, **kw),
        grid=(N_OCT, n_chunks // tr),
        in_specs=[pl.BlockSpec((tr * S5_CHUNK, LANES), lambda o, i: (i, o)),
                  pl.BlockSpec((1, S5_CHUNK // 2, 2 * LANES, st_w), lambda o, i: (o, 0, 0, 0))],
        out_specs=pl.BlockSpec((tr, st_w), lambda o, i: (i, o)),
        out_shape=jax.ShapeDtypeStruct((n_chunks, N_OCT * st_w), F32),
        compiler_params=_cparams("parallel", "parallel"),
        name="s5_state",
    )(u, p_tab.reshape(N_OCT, S5_CHUNK // 2, 2 * LANES, st_w))
    tc = _tile(chunks_per_batch, 129, 1)
    nt = chunks_per_batch // tc
    x_prev = pl.pallas_call(
        functools.partial(_s5_scan_kernel, tc=tc),
        grid=(n_batch, nt),
        in_specs=[pl.BlockSpec((tc, N_OCT, st_w), lambda b, i: (b * nt + i, 0, 0)),
                  pl.BlockSpec((N_OCT, st_w // 2), lambda b, i: (0, 0)),
                  pl.BlockSpec((N_OCT, st_w // 2), lambda b, i: (0, 0))],
        out_specs=pl.BlockSpec((tc, N_OCT, st_w), lambda b, i: (b * nt + i, 0, 0)),
        out_shape=jax.ShapeDtypeStruct((n_chunks, N_OCT, st_w), F32),
        scratch_shapes=[pltpu.VMEM((N_OCT, st_w), F32)],
        compiler_params=_cparams("arbitrary", "arbitrary"),
        name="s5_scan",
    )(s_all.reshape(n_chunks, N_OCT, st_w), a16_re, a16_im)
    z = pl.pallas_call(
        functools.partial(_s5_out_kernel, **kw),
        grid=(N_OCT, n_chunks // tr),
        in_specs=[pl.BlockSpec((tr * S5_CHUNK, LANES), lambda o, i: (i, o)),
                  pl.BlockSpec((tr, st_w), lambda o, i: (i, o)),
                  pl.BlockSpec((1, S5_CHUNK // 2, 2 * LANES, S5_CHUNK * LANES), lambda o, i: (o, 0, 0, 0)),
                  pl.BlockSpec((1, st_w, S5_CHUNK * LANES), lambda o, i: (o, 0, 0))],
        out_specs=pl.BlockSpec((tr * S5_CHUNK, LANES), lambda o, i: (i, o)),
        out_shape=jax.ShapeDtypeStruct((tp, d), F32),
        scratch_shapes=[pltpu.VMEM((tr, S5_CHUNK * LANES), F32)],
        compiler_params=_cparams("parallel", "parallel"),
        name="s5_out",
    )(u, x_prev.reshape(n_chunks, N_OCT * st_w),
      m_tab.reshape(N_OCT, S5_CHUNK // 2, 2 * LANES, S5_CHUNK * LANES), r_tab)
    row = lambda i: (i, 0)
    fix = lambda i: (0, 0)
    return pl.pallas_call(
        _s5_tail_kernel,
        grid=(tp // tm,),
        in_specs=[pl.BlockSpec((tm, d), row), pl.BlockSpec((d, d), fix), pl.BlockSpec((1, d), fix),
                  pl.BlockSpec((d, d), fix), pl.BlockSpec((tm, d), row),
                  pl.BlockSpec((1, d), fix), pl.BlockSpec((1, d), fix)],
        out_specs=pl.BlockSpec((tm, d), row),
        out_shape=jax.ShapeDtypeStruct((tp, d), F32),
        compiler_params=_cparams("parallel"),
        name="s5_tail",
    )(z, w_glu.astype(BF16), b_glu.reshape(1, d).astype(F32), w_out.astype(BF16), h,
      ln_g.reshape(1, d), ln_b.reshape(1, d))


def _ret_kernel(q_ref, k_ref, v_ref, gate_ref, cos_ref, sin_ref, dm_ref, qd_ref, kd_ref, cd_ref, g_ref,
                o_ref, st_ref, *, chunk, pad):
    c = pl.program_id(2)

    @pl.when(c == 0)
    def _():
        st_ref[...] = jnp.zeros(st_ref.shape, F32)

    cos = cos_ref[...]
    sin = sin_ref[...]
    half = RET_DK // 2

    def rot(t):
        t1 = t[:, :half]
        t2 = t[:, half:]
        return jnp.concatenate([t1 * cos - t2 * sin, t1 * sin + t2 * cos], axis=-1)

    q = rot(q_ref[...])
    k = rot(k_ref[...]) * (RET_DK ** -0.5)
    pos = c * chunk + lax.broadcasted_iota(jnp.int32, (chunk, 1), 0)
    k = jnp.where(pos >= pad, k, 0.0)
    qb = q.astype(BF16)
    vb = v_ref[...].astype(BF16)
    scores = lax.dot_general(qb, k.astype(BF16), (((1,), (1,)), ((), ())),
                             preferred_element_type=F32) * dm_ref[0]
    intra = jnp.dot(scores.astype(BF16), vb, preferred_element_type=F32)
    state = st_ref[...]
    inter = jnp.dot(qb, state.astype(BF16), preferred_element_type=F32) * qd_ref[0]
    st_ref[...] = state * cd_ref[0] + lax.dot_general((k * kd_ref[0]).astype(BF16), vb,
                                                      (((0,), (0,)), ((), ())),
                                                      preferred_element_type=F32)
    o = intra + inter
    mu = jnp.mean(o, axis=-1, keepdims=True)
    oc = o - mu
    var = jnp.mean(oc * oc, axis=-1, keepdims=True)
    o = oc * lax.rsqrt(var + LN_EPS) * g_ref[...]
    gate = gate_ref[...]
    o_ref[...] = (gate * jax.nn.sigmoid(gate) * o).astype(o_ref.dtype)


def _retention_mixer(h, lp, pad, w_in, gn_g, w_out, ln_g, ln_b, tm):
    tp, d = h.shape
    n_batch = tp // lp
    chunk = _tile(lp, 384, SEQ_ALIGN)
    ncb = lp // chunk
    qd = RET_HEADS * RET_DK
    vd = RET_HEADS * RET_DV
    proj = _matmul(h, w_in.astype(BF16), F32, tm, _tile(w_in.shape[1], 1536, LANES))
    half = RET_DK // 2
    pos = jnp.arange(lp, dtype=F32) - pad
    inv_freq = jnp.power(RET_THETA, -jnp.arange(half, dtype=F32) / half)
    ang = pos[:, None] * inv_freq[None, :]
    cos, sin = jnp.cos(ang), jnp.sin(ang)
    log_gamma = jnp.log1p(-jnp.power(2.0, -5.0 - jnp.arange(RET_HEADS, dtype=F32)))
    idx = jnp.arange(chunk, dtype=F32)
    rel = idx[:, None] - idx[None, :]
    dmask = jnp.where(rel[None] >= 0, jnp.exp(log_gamma[:, None, None] * jnp.maximum(rel, 0.0)[None]), 0.0)
    q_decay = jnp.exp(log_gamma[:, None] * (idx[None, :] + 1.0))[:, :, None]
    k_decay = jnp.exp(log_gamma[:, None] * (chunk - 1.0 - idx[None, :]))[:, :, None]
    chunk_decay = jnp.exp(log_gamma * chunk).reshape(RET_HEADS, 1, 1)
    kb = qd // RET_DK
    vb = 2 * qd // RET_DV
    gb = (2 * qd + vd) // RET_DV
    o = pl.pallas_call(
        functools.partial(_ret_kernel, chunk=chunk, pad=pad),
        grid=(n_batch, RET_HEADS, ncb),
        in_specs=[pl.BlockSpec((chunk, RET_DK), lambda b, hh, c: (b * ncb + c, hh)),
                  pl.BlockSpec((chunk, RET_DK), lambda b, hh, c: (b * ncb + c, kb + hh)),
                  pl.BlockSpec((chunk, RET_DV), lambda b, hh, c: (b * ncb + c, vb + hh)),
                  pl.BlockSpec((chunk, RET_DV), lambda b, hh, c: (b * ncb + c, gb + hh)),
                  pl.BlockSpec((chunk, half), lambda b, hh, c: (c, 0)),
                  pl.BlockSpec((chunk, half), lambda b, hh, c: (c, 0)),
                  pl.BlockSpec((1, chunk, chunk), lambda b, hh, c: (hh, 0, 0)),
                  pl.BlockSpec((1, chunk, 1), lambda b, hh, c: (hh, 0, 0)),
                  pl.BlockSpec((1, chunk, 1), lambda b, hh, c: (hh, 0, 0)),
                  pl.BlockSpec((1, 1, 1), lambda b, hh, c: (hh, 0, 0)),
                  pl.BlockSpec((1, RET_DV), lambda b, hh, c: (0, hh))],
        out_specs=pl.BlockSpec((chunk, RET_DV), lambda b, hh, c: (b * ncb + c, hh)),
        out_shape=jax.ShapeDtypeStruct((tp, vd), BF16),
        scratch_shapes=[pltpu.VMEM((RET_DK, RET_DV), F32)],
        compiler_params=_cparams("parallel", "parallel", "arbitrary"),
        name="retention",
    )(proj, proj, proj, proj, cos, sin, dmask, q_decay, k_decay, chunk_decay,
      gn_g.reshape(1, vd).astype(F32))
    return _matmul_res_ln(o, w_out.astype(BF16), h, ln_g, ln_b, tm)


ONES_ROWS = 16


def _diff_kernel(q_ref, k_ref, vt_ref, lam_ref, g_ref, o_ref, vext_ref, m_ref, acc_ref, s0_ref, s1_ref,
                 s2_ref, s3_ref, mx_ref,
                 *, tq, pad, lambda_init):
    i = pl.program_id(2)
    nkb = vext_ref.shape[0]

    @pl.when(i == 0)
    def _():
        for jb in range(nkb):
            vext_ref[jb, 0:DIFF_DV, :] = vt_ref[:, jb * tq:(jb + 1) * tq]
            vext_ref[jb, DIFF_DV:, :] = jnp.ones((ONES_ROWS, tq), BF16)

    q = q_ref[...]
    lane = lax.broadcasted_iota(jnp.int32, q.shape, 1)
    zero = jnp.zeros_like(q)
    q_both = jnp.concatenate([jnp.where(lane < DIFF_DQK, q, zero), jnp.where(lane >= DIFF_DQK, q, zero)],
                             axis=0)
    m_ref[...] = jnp.full(m_ref.shape, MASK_VALUE, F32)
    acc_ref[...] = jnp.zeros(acc_ref.shape, F32)

    def scores(j):
        start = j * tq if isinstance(j, int) else pl.multiple_of(j * tq, tq)
        k = k_ref[pl.ds(start, tq), :]
        return lax.dot_general(k, q_both, (((1,), (1,)), ((), ())), preferred_element_type=F32)

    s_refs = (s0_ref, s1_ref, s2_ref, s3_ref)

    def produce(slot, j):
        s = scores(j)
        s_refs[slot][...] = s.astype(BF16)
        mx_ref[slot] = jnp.max(s, axis=0, keepdims=True)

    def running_max(block_max):
        return jnp.maximum(m_ref[...], block_max).astype(BF16).astype(F32)

    def accumulate(m_new, pe, v):
        m_prev = m_ref[...]
        acc_ref[...] = jnp.exp2(m_prev - m_new) * acc_ref[...] + jnp.dot(v, pe, preferred_element_type=F32)
        m_ref[...] = m_new

    def consume_masked(s, j):
        kpos = j * tq + lax.broadcasted_iota(jnp.int32, (tq, tq), 0)
        qpos = i * tq + lax.broadcasted_iota(jnp.int32, (tq, tq), 1)
        allowed = (kpos <= qpos) & (kpos >= pad)
        s = jnp.where(jnp.concatenate([allowed, allowed], axis=1), s, MASK_VALUE)
        m_new = running_max(jnp.max(s, axis=0, keepdims=True))
        accumulate(m_new, jnp.exp2((s - m_new).astype(BF16)), vext_ref[j])

    def consume(slot, j):
        m_new = running_max(mx_ref[slot])
        accumulate(m_new, jnp.exp2(s_refs[slot][...] - m_new.astype(BF16)), vext_ref[j])

    def consume2(slot_a, slot_b, j):
        m_new = running_max(jnp.maximum(mx_ref[slot_a], mx_ref[slot_b]))
        m_bf = m_new.astype(BF16)
        pe = jnp.concatenate([jnp.exp2(s_refs[slot_a][...] - m_bf), jnp.exp2(s_refs[slot_b][...] - m_bf)],
                             axis=0)
        accumulate(m_new, pe, jnp.concatenate([vext_ref[j], vext_ref[j + 1]], axis=1))

    @pl.when(i == 0)
    def _():
        consume_masked(scores(0), 0)

    @pl.when(i > 0)
    def _():
        n = i - 1
        n_quads = n // 4
        s3_ref[...] = scores(0).astype(BF16)
        produce(0, 1)
        produce(1, jnp.minimum(2, i))
        consume_masked(s3_ref[...].astype(F32), 0)

        def quad(u, carry):
            a = 1 + 4 * u
            produce(2, a + 2)
            produce(3, a + 3)
            consume2(0, 1, a)
            produce(0, jnp.minimum(a + 4, i))
            produce(1, jnp.minimum(a + 5, i))
            consume2(2, 3, a + 2)
            return carry

        lax.fori_loop(0, n_quads, quad, 0)
        first = 1 + 4 * n_quads

        def pair(jj, carry):
            a = first + 2 * jj
            produce(1, a + 1)
            consume(0, a)
            produce(0, a + 2)
            consume(1, a + 1)
            return carry

        lax.fori_loop(0, (n - 4 * n_quads) // 2, pair, 0)

        @pl.when(n % 2 == 1)
        def _():
            produce(1, i)
            consume(0, i - 1)
            consume_masked(s1_ref[...].astype(F32), i)

        @pl.when(n % 2 == 0)
        def _():
            consume_masked(s0_ref[...].astype(F32), i)

    lam_p = lam_ref[...]
    lam = (jnp.exp(jnp.sum(lam_p[0:1] * lam_p[1:2], axis=-1, keepdims=True))
           - jnp.exp(jnp.sum(lam_p[2:3] * lam_p[3:4], axis=-1, keepdims=True)) + lambda_init)
    acc = acc_ref[...]
    o_both = acc[0:DIFF_DV] / acc[DIFF_DV:DIFF_DV + 1]
    o = o_both[:, :tq] - lam * o_both[:, tq:]
    o = o * lax.rsqrt(jnp.mean(o * o, axis=0, keepdims=True) + LN_EPS)
    o = o * (g_ref[...] * (1.0 - lambda_init))
    o_ref[...] = o.T.astype(o_ref.dtype)


def _mm_nt_kernel(wt_ref, x_ref, o_ref):
    o_ref[...] = lax.dot_general(wt_ref[...], x_ref[...].astype(BF16), (((1,), (1,)), ((), ())),
                                 preferred_element_type=F32).astype(o_ref.dtype)


def _matmul_nt(wt, x, out_dtype, tm):
    n, k = wt.shape
    m = x.shape[0]
    return pl.pallas_call(
        _mm_nt_kernel,
        grid=(m // tm,),
        in_specs=[pl.BlockSpec((n, k), lambda i: (0, 0)), pl.BlockSpec((tm, k), lambda i: (i, 0))],
        out_specs=pl.BlockSpec((n, tm), lambda i: (0, i)),
        out_shape=jax.ShapeDtypeStruct((n, m), out_dtype),
        compiler_params=_cparams("parallel"),
        name="matmul_nt",
    )(wt, x)


def _diff_attn_mixer(h, lp, pad, w_in, lq1, lk1, lq2, lk2, subln_g, w_out, lambda_init, ln_g, ln_b, tm):
    tp, d = h.shape
    n_batch = tp // lp
    qk = DIFF_HEADS * 2 * DIFF_DQK
    col_scale = jnp.concatenate([jnp.full((qk,), DIFF_DQK ** -0.5 * math.log2(math.e), F32),
                                 jnp.ones((qk,), F32)])
    proj = _matmul(h, (w_in[:, :2 * qk] * col_scale[None, :]).astype(BF16), BF16, tm, _tile(2 * qk, 2048, LANES))
    vt = _matmul_nt(w_in[:, 2 * qk:].T.astype(BF16), h, BF16, _tile(tp, 768, LANES))
    tq = _tile(lp, 384, SEQ_ALIGN)
    nq = lp // tq
    lam_p = jnp.stack([lq1, lk1, lq2, lk2]).astype(F32)
    o = pl.pallas_call(
        functools.partial(_diff_kernel, tq=tq, pad=pad, lambda_init=lambda_init),
        grid=(n_batch, DIFF_HEADS, nq),
        in_specs=[pl.BlockSpec((tq, LANES), lambda b, hh, i: (b * nq + i, hh)),
                  pl.BlockSpec((lp, LANES), lambda b, hh, i: (b, DIFF_HEADS + hh)),
                  pl.BlockSpec((DIFF_DV, lp), lambda b, hh, i: (hh, b)),
                  pl.BlockSpec((4, DIFF_DQK), lambda b, hh, i: (0, 0)),
                  pl.BlockSpec((DIFF_DV, 1), lambda b, hh, i: (0, 0))],
        out_specs=pl.BlockSpec((tq, DIFF_DV), lambda b, hh, i: (b * nq + i, hh)),
        out_shape=jax.ShapeDtypeStruct((tp, DIFF_HEADS * DIFF_DV), BF16),
        scratch_shapes=[pltpu.VMEM((nq, DIFF_DV + ONES_ROWS, tq), BF16),
                        pltpu.VMEM((1, 2 * tq), F32),
                        pltpu.VMEM((DIFF_DV + ONES_ROWS, 2 * tq), F32),
                        ] + [pltpu.VMEM((tq, 2 * tq), BF16)] * 4 + [pltpu.VMEM((4, 1, 2 * tq), F32)],
        compiler_params=_cparams("parallel", "parallel", "arbitrary"),
        name="diff_attn",
    )(proj, proj, vt, lam_p, subln_g.reshape(DIFF_DV, 1).astype(F32))
    return _matmul_res_ln(o, w_out.astype(BF16), h, ln_g, ln_b, tm)


SORT_ALIGN = 16
SORT_CHUNK = 512


def _router_kernel(x_ref, w_ref, b_ref, idx_ref, gate_ref, rank_ref, cnt_ref, *, tm):
    x = x_ref[...]
    x_hi = x.astype(BF16)
    x_lo = (x - x_hi.astype(F32)).astype(BF16)
    hi = jnp.dot(x_hi, w_ref[...], preferred_element_type=F32)
    lo = jnp.dot(x_lo, w_ref[:, :LANES], preferred_element_type=F32)
    logits = hi[:, :LANES] + hi[:, LANES:] + lo + b_ref[...]
    lane = lax.broadcasted_iota(jnp.int32, logits.shape, 1).astype(F32)
    work = jnp.where(lane < N_EXPERTS, logits, -jnp.inf)
    vals, idxs = [], []
    picked = jnp.zeros(logits.shape, F32)
    for _ in range(TOP_K):
        m = jnp.max(work, axis=-1, keepdims=True)
        sel = jnp.min(jnp.where(work == m, lane, float(LANES)), axis=-1, keepdims=True)
        hit = lane == sel
        work = jnp.where(hit, -jnp.inf, work)
        picked = jnp.where(hit, 1.0, picked)
        vals.append(m)
        idxs.append(sel)
    exps = [jnp.exp(v - vals[0]) for v in vals]
    tot = exps[0] + exps[1] + exps[2] + exps[3]
    r = lax.broadcasted_iota(jnp.int32, (tm, tm), 0)
    c = lax.broadcasted_iota(jnp.int32, (tm, tm), 1)
    lower = jnp.where(c < r, 1.0, 0.0).astype(BF16)
    before = jnp.dot(lower, picked.astype(BF16), preferred_element_type=F32)
    for kk in range(TOP_K):
        idx_ref[:, kk:kk + 1] = idxs[kk].astype(jnp.int32)
        gate_ref[:, kk:kk + 1] = exps[kk] / tot
        rank_ref[:, kk:kk + 1] = jnp.sum(jnp.where(lane == idxs[kk], before, 0.0), axis=-1,
                                         keepdims=True).astype(jnp.int32)
    cnt_ref[0] = jnp.sum(picked, axis=0, keepdims=True)


BIG_CHUNKS = 4


def _row_dmas(n_chunks, make_copy, wait):
    n_big = n_chunks // BIG_CHUNKS
    big = BIG_CHUNKS * SORT_ALIGN

    def run(cp):
        if wait:
            cp.wait()
        else:
            cp.start()

    def big_body(c, carry):
        run(make_copy(c * big, big))
        return carry

    def small_body(c, carry):
        run(make_copy(n_big * big + c * SORT_ALIGN, SORT_ALIGN))
        return carry

    lax.fori_loop(0, n_big, big_body, 0)
    lax.fori_loop(0, n_chunks - n_big * BIG_CHUNKS, small_body, 0)


def _segment_dma(tab_ref, n_seg, tile, local_ref, slots_ref, sem, to_slots, wait):
    def per_expert(e, carry):
        lo0 = tab_ref[tile * N_EXPERTS + e]
        go0 = tab_ref[n_seg + tile * N_EXPERTS + e]

        def make_copy(row, rows):
            local = local_ref.at[pl.ds(pl.multiple_of(lo0 + row, SORT_ALIGN), rows), :]
            slots = slots_ref.at[pl.ds(pl.multiple_of(go0 + row, SORT_ALIGN), rows), :]
            return (pltpu.make_async_copy(local, slots, sem) if to_slots
                    else pltpu.make_async_copy(slots, local, sem))

        _row_dmas(tab_ref[2 * n_seg + tile * N_EXPERTS + e], make_copy, wait)
        return carry
    lax.fori_loop(0, N_EXPERTS, per_expert, 0)


def _zero_fill_dma(tab_ref, n_seg, n_blocks, zero_ref, slots_ref, sem, wait):
    bm = zero_ref.shape[0]

    def per_expert(e, carry):
        off = tab_ref[3 * n_seg + e]

        def make_copy(row, rows):
            dst = slots_ref.at[pl.ds(pl.multiple_of(off + row, SORT_ALIGN), rows), :]
            return pltpu.make_async_copy(zero_ref.at[pl.ds(0, rows), :], dst, sem)

        _row_dmas(tab_ref[3 * n_seg + N_EXPERTS + e], make_copy, wait)
        return carry
    lax.fori_loop(0, N_EXPERTS, per_expert, 0)

    def per_block(blk, carry):
        cp = pltpu.make_async_copy(zero_ref, slots_ref.at[pl.ds(pl.multiple_of(blk * bm, bm), bm), :], sem)
        if wait:
            cp.wait()
        else:
            cp.start()
        return carry
    lax.fori_loop(tab_ref[3 * n_seg + 2 * N_EXPERTS], n_blocks, per_block, 0)


def _dispatch_kernel(tab_ref, x_ref, lrow_ref, slots_ref, sorted_ref, zero_ref, sem, zero_sem,
                     *, tm, rs, n_blocks):
    i = pl.program_id(0)
    n_seg = pl.num_programs(0) * N_EXPERTS
    slot = i % 2
    buf = sorted_ref.at[slot]

    @pl.when(i == 0)
    def _():
        zero_ref[...] = jnp.zeros(zero_ref.shape, BF16)
        _zero_fill_dma(tab_ref, n_seg, n_blocks, zero_ref, slots_ref, zero_sem, wait=False)
        _zero_fill_dma(tab_ref, n_seg, n_blocks, zero_ref, slots_ref, zero_sem, wait=True)

    xb = x_ref[...].astype(BF16)
    lr = lrow_ref[...]
    for rc in range(rs // SORT_CHUNK):
        rows = rc * SORT_CHUNK + lax.broadcasted_iota(jnp.int32, (SORT_CHUNK, tm), 0)
        hit = (rows == lr[0:1]) | (rows == lr[1:2]) | (rows == lr[2:3]) | (rows == lr[3:4])
        perm = jnp.where(hit, 1.0, 0.0).astype(BF16)
        buf[rc * SORT_CHUNK:(rc + 1) * SORT_CHUNK, :] = jnp.dot(
            perm, xb, preferred_element_type=F32).astype(BF16)
    _segment_dma(tab_ref, n_seg, i, buf, slots_ref, sem.at[slot], True, wait=False)

    @pl.when(i > 0)
    def _():
        _segment_dma(tab_ref, n_seg, i - 1, sorted_ref.at[1 - slot], slots_ref, sem.at[1 - slot], True, wait=True)

    @pl.when(i == pl.num_programs(0) - 1)
    def _():
        _segment_dma(tab_ref, n_seg, i, buf, slots_ref, sem.at[slot], True, wait=True)


def _combine_kernel(tab_ref, res_ref, lrow_ref, gate_ref, g_ref, b_ref, y_ref, o_ref, ysort_ref, sem,
                    *, tm, rs):
    i = pl.program_id(0)
    n_seg = pl.num_programs(0) * N_EXPERTS
    slot = i % 2

    @pl.when(i == 0)
    def _():
        ysort_ref[...] = jnp.zeros(ysort_ref.shape, BF16)
        _segment_dma(tab_ref, n_seg, 0, ysort_ref.at[0], y_ref, sem.at[0], False, wait=False)

    @pl.when(i + 1 < pl.num_programs(0))
    def _():
        _segment_dma(tab_ref, n_seg, i + 1, ysort_ref.at[1 - slot], y_ref, sem.at[1 - slot], False, wait=False)

    _segment_dma(tab_ref, n_seg, i, ysort_ref.at[slot], y_ref, sem.at[slot], False, wait=True)
    ysort = ysort_ref.at[slot]
    lr = lrow_ref[...]
    gt = gate_ref[...]
    y = jnp.zeros(res_ref.shape, F32)
    for rc in range(rs // SORT_CHUNK):
        cols = rc * SORT_CHUNK + lax.broadcasted_iota(jnp.int32, (tm, SORT_CHUNK), 1)
        w = jnp.zeros((tm, SORT_CHUNK), F32)
        for kk in range(TOP_K):
            w = jnp.where(cols == lr[:, kk:kk + 1], gt[:, kk:kk + 1], w)
        y += jnp.dot(w.astype(BF16), ysort[rc * SORT_CHUNK:(rc + 1) * SORT_CHUNK, :],
                     preferred_element_type=F32)
    o_ref[...] = _layer_norm_rows(DEEPNORM_ALPHA * res_ref[...] + y, g_ref[...], b_ref[...])


def _expert_kernel(be_ref, nu_ref, x_ref, wu_ref, bu_ref, wd_ref, bd_ref, o_ref, wu_bf_ref, wd_bf_ref):
    i = pl.program_id(0)
    live = i < nu_ref[0]

    @pl.when(live & ((i == 0) | (be_ref[i] != be_ref[jnp.maximum(i - 1, 0)])))
    def _():
        wu_bf_ref[...] = wu_ref[0, 0].astype(BF16)
        wd_bf_ref[...] = wd_ref[0, 0].astype(BF16)

    @pl.when(live)
    def _():
        hb = jnp.dot(x_ref[...], wu_bf_ref[...], preferred_element_type=F32) + bu_ref[0]
        x_glu = jnp.minimum(hb[:, :D_EXPERT], SWIGLU_LIMIT)
        x_lin = jnp.clip(hb[:, D_EXPERT:], -SWIGLU_LIMIT, SWIGLU_LIMIT)
        act = x_glu * jax.nn.sigmoid(SWIGLU_ALPHA * x_glu) * (x_lin + 1.0)
        y = jnp.dot(act.astype(BF16), wd_bf_ref[...], preferred_element_type=F32) + bd_ref[0]
        o_ref[...] = y.astype(o_ref.dtype)

    @pl.when(jnp.logical_not(live))
    def _():
        o_ref[...] = jnp.zeros(o_ref.shape, o_ref.dtype)


def _moe_ffn(h, layer, w_router, b_router, w_up_all, b_up, w_down_all, b_down, ln_g, ln_b, tm, bm):
    tp, d = h.shape
    a = tp * TOP_K
    wr = jnp.zeros((d, LANES), F32).at[:, :N_EXPERTS].set(w_router.astype(F32))
    wr_hi = wr.astype(BF16)
    wr = jnp.concatenate([wr_hi, (wr - wr_hi.astype(F32)).astype(BF16)], axis=1)
    br = jnp.zeros((1, LANES), F32).at[0, :N_EXPERTS].set(b_router.astype(F32))
    row = lambda i: (i, 0)
    fix = lambda i: (0, 0)
    nt = tp // tm
    idx, gates, rank, cnt = pl.pallas_call(
        functools.partial(_router_kernel, tm=tm),
        grid=(nt,),
        in_specs=[pl.BlockSpec((tm, d), row), pl.BlockSpec((d, 2 * LANES), fix), pl.BlockSpec((1, LANES), fix)],
        out_specs=[pl.BlockSpec((tm, TOP_K), row), pl.BlockSpec((tm, TOP_K), row),
                   pl.BlockSpec((tm, TOP_K), row), pl.BlockSpec((1, 1, LANES), lambda i: (i, 0, 0))],
        out_shape=[jax.ShapeDtypeStruct((tp, TOP_K), jnp.int32), jax.ShapeDtypeStruct((tp, TOP_K), F32),
                   jax.ShapeDtypeStruct((tp, TOP_K), jnp.int32), jax.ShapeDtypeStruct((nt, 1, LANES), F32)],
        compiler_params=_cparams("parallel"),
        name="router",
    )(h, wr, br)
    counts = cnt[:, 0, :N_EXPERTS].astype(jnp.int32)
    seg = ((counts + SORT_ALIGN - 1) // SORT_ALIGN) * SORT_ALIGN
    tile_off = jnp.cumsum(seg, axis=0) - seg
    region = ((jnp.sum(seg, axis=0) + bm - 1) // bm) * bm
    region_end = jnp.cumsum(region)
    slot_off = (region_end - region)[None, :] + tile_off
    local_off = jnp.cumsum(seg, axis=1) - seg
    used = jnp.sum(seg, axis=0)
    n_used = (region_end[-1:] // bm).astype(jnp.int32)
    table = jnp.concatenate([local_off.reshape(-1), slot_off.reshape(-1), (seg // SORT_ALIGN).reshape(-1),
                             region_end - region + used, (region - used) // SORT_ALIGN,
                             n_used]).astype(jnp.int32)
    onehot = idx.reshape(nt, tm, TOP_K, 1) == jnp.arange(N_EXPERTS, dtype=jnp.int32)
    lrow = jnp.sum(jnp.where(onehot, local_off[:, None, None, :], 0), axis=-1).reshape(tp, TOP_K) + rank
    lrow_t = jnp.concatenate([lrow.T, jnp.full((8 - TOP_K, tp), -1, jnp.int32)], axis=0)
    rs = -(-(TOP_K * tm + N_EXPERTS * (SORT_ALIGN - 1)) // SORT_CHUNK) * SORT_CHUNK
    n_blocks = -(-(a + nt * N_EXPERTS * (SORT_ALIGN - 1)) // bm) + N_EXPERTS
    n_slots = n_blocks * bm
    block_start = jnp.arange(n_blocks, dtype=jnp.int32) * bm
    block_e = jnp.minimum(jnp.sum((region_end[None, :] <= block_start[:, None]).astype(jnp.int32), axis=1),
                          N_EXPERTS - 1)
    x_slots = pl.pallas_call(
        functools.partial(_dispatch_kernel, tm=tm, rs=rs, n_blocks=n_blocks),
        grid_spec=pltpu.PrefetchScalarGridSpec(
            num_scalar_prefetch=1,
            grid=(nt,),
            in_specs=[pl.BlockSpec((tm, d), lambda i, tab: (i, 0)),
                      pl.BlockSpec((8, tm), lambda i, tab: (0, i))],
            out_specs=pl.BlockSpec(memory_space=pl.ANY),
            scratch_shapes=[pltpu.VMEM((2, rs, d), BF16), pltpu.VMEM((bm, d), BF16),
                            pltpu.SemaphoreType.DMA((2,)), pltpu.SemaphoreType.DMA(())],
        ),
        out_shape=jax.ShapeDtypeStruct((n_slots, d), BF16),
        compiler_params=_cparams("arbitrary"),
        name="moe_dispatch",
    )(table, h, lrow_t)
    y_slots = pl.pallas_call(
        _expert_kernel,
        grid_spec=pltpu.PrefetchScalarGridSpec(
            num_scalar_prefetch=2,
            grid=(n_blocks,),
            in_specs=[pl.BlockSpec((bm, d), lambda i, be, nu: (i, 0)),
                      pl.BlockSpec((1, 1, d, 2 * D_EXPERT), lambda i, be, nu: (layer, be[i], 0, 0)),
                      pl.BlockSpec((1, 1, 2 * D_EXPERT), lambda i, be, nu: (be[i], 0, 0)),
                      pl.BlockSpec((1, 1, D_EXPERT, d), lambda i, be, nu: (layer, be[i], 0, 0)),
                      pl.BlockSpec((1, 1, d), lambda i, be, nu: (be[i], 0, 0))],
            out_specs=pl.BlockSpec((bm, d), lambda i, be, nu: (i, 0)),
            scratch_shapes=[pltpu.VMEM((d, 2 * D_EXPERT), BF16), pltpu.VMEM((D_EXPERT, d), BF16)],
        ),
        out_shape=jax.ShapeDtypeStruct((n_slots, d), BF16),
        compiler_params=_cparams("arbitrary"),
        name="experts",
    )(block_e, n_used, x_slots, w_up_all, b_up.reshape(N_EXPERTS, 1, 2 * D_EXPERT).astype(F32),
      w_down_all, b_down.reshape(N_EXPERTS, 1, d).astype(F32))
    return pl.pallas_call(
        functools.partial(_combine_kernel, tm=tm, rs=rs),
        grid_spec=pltpu.PrefetchScalarGridSpec(
            num_scalar_prefetch=1,
            grid=(nt,),
            in_specs=[pl.BlockSpec((tm, d), lambda i, tab: (i, 0)),
                      pl.BlockSpec((tm, TOP_K), lambda i, tab: (i, 0)),
                      pl.BlockSpec((tm, TOP_K), lambda i, tab: (i, 0)),
                      pl.BlockSpec((1, d), lambda i, tab: (0, 0)),
                      pl.BlockSpec((1, d), lambda i, tab: (0, 0)),
                      pl.BlockSpec(memory_space=pl.ANY)],
            out_specs=pl.BlockSpec((tm, d), lambda i, tab: (i, 0)),
            scratch_shapes=[pltpu.VMEM((2, rs, d), BF16), pltpu.SemaphoreType.DMA((2,))],
        ),
        out_shape=jax.ShapeDtypeStruct((tp, d), F32),
        compiler_params=_cparams("arbitrary"),
        name="moe_combine_ln",
    )(table, h, lrow, gates, ln_g.reshape(1, d), ln_b.reshape(1, d), y_slots)


def kernel(x, meta_tokens, s5_w_in, s5_lambda_re, s5_lambda_im, s5_log_step, s5_b_re, s5_b_im, s5_c_re, s5_c_im, s5_d, s5_w_glu, s5_b_glu, s5_w_out, ret_w_in, ret_gn_g, ret_w_out, diff_w_in, diff_lambda_q1, diff_lambda_k1, diff_lambda_q2, diff_lambda_k2, diff_subln_g, diff_w_out, ln_mix_g, ln_mix_b, moe_w_router, moe_b_router, moe_w_up, moe_b_up, moe_w_down, moe_b_down, ln_ffn_g, ln_ffn_b):
    n_batch, seq, d = x.shape
    length = seq + N_META
    pad = (-length) % SEQ_ALIGN
    lp = length + pad
    assert pad % S5_CHUNK == 0 and d == D_MODEL
    tp = n_batch * lp
    tm = _tile(tp, 768)
    meta = jnp.broadcast_to(meta_tokens[None].astype(x.dtype), (n_batch, N_META, d))
    h = jnp.concatenate([jnp.zeros((n_batch, pad, d), x.dtype), meta, x], axis=1).reshape(tp, d)
    for i in range(DEPTH):
        kind = i % N_MIXERS
        j = i // N_MIXERS
        if kind == 0:
            h = _s5_mixer(h, lp, pad, s5_w_in[j], s5_lambda_re[j], s5_lambda_im[j], s5_log_step[j],
                          s5_b_re[j], s5_b_im[j], s5_c_re[j], s5_c_im[j], s5_d[j],
                          s5_w_glu[j], s5_b_glu[j], s5_w_out[j], ln_mix_g[i], ln_mix_b[i], tm)
        elif kind == 1:
            h = _retention_mixer(h, lp, pad, ret_w_in[j], ret_gn_g[j], ret_w_out[j],
                                 ln_mix_g[i], ln_mix_b[i], tm)
        else:
            lambda_init = 0.8 - 0.6 * math.exp(-0.3 * i)
            h = _diff_attn_mixer(h, lp, pad, diff_w_in[j], diff_lambda_q1[j], diff_lambda_k1[j],
                                 diff_lambda_q2[j], diff_lambda_k2[j], diff_subln_g[j], diff_w_out[j],
                                 lambda_init, ln_mix_g[i], ln_mix_b[i], tm)
        h = _moe_ffn(h, i, moe_w_router[i], moe_b_router[i], moe_w_up, moe_b_up[i],
                     moe_w_down, moe_b_down[i], ln_ffn_g[i], ln_ffn_b[i], tm, MOE_BLOCK)
    return h.reshape(n_batch, lp, d)[:, pad + N_META:, :]
```

```python
import functools
import math

import jax
import jax.numpy as jnp
from jax import lax
from jax.experimental import pallas as pl
from jax.experimental.pallas import tpu as pltpu

F32 = jnp.float32
BF16 = jnp.bfloat16

D_MODEL = 1024
DEPTH = 4
N_META = 16
N_MIXERS = 3
S5_GROUP_CH = 16
S5_GROUPS = D_MODEL // S5_GROUP_CH
S5_STATE = 64
S5_CHUNK = 16
S5_OCT = 8
N_OCT = S5_GROUPS // S5_OCT
RET_HEADS = 4
RET_DK = D_MODEL // RET_HEADS
RET_DV = 2 * RET_DK
RET_THETA = 10000.0
DIFF_HEADS = D_MODEL // 128
DIFF_DQK = 64
DIFF_DV = 128
N_EXPERTS = 32
TOP_K = 4
D_EXPERT = D_MODEL
SWIGLU_LIMIT = 7.0
SWIGLU_ALPHA = 1.702
MOE_BLOCK = 512
LN_EPS = 1e-5
MASK_VALUE = -(2.0 ** 100)
DEEPNORM_ALPHA = (2 * DEPTH) ** 0.25
SEQ_ALIGN = 128
LANES = 128
VMEM_LIMIT = 56 * 1024 * 1024


def _cparams(*sem):
    return pltpu.CompilerParams(dimension_semantics=sem, vmem_limit_bytes=VMEM_LIMIT)


def _tile(n, target, mult=8):
    best = None
    for t in range(mult, min(n, target) + 1, mult):
        if n % t == 0:
            best = t
    assert best is not None, (n, target, mult)
    return best


def _layer_norm_rows(v, g, b):
    mu = jnp.mean(v, axis=-1, keepdims=True)
    c = v - mu
    var = jnp.mean(c * c, axis=-1, keepdims=True)
    return c * lax.rsqrt(var + LN_EPS) * g + b


def _mm_kernel(x_ref, w_ref, o_ref):
    o_ref[...] = jnp.dot(x_ref[...].astype(BF16), w_ref[...],
                         preferred_element_type=F32).astype(o_ref.dtype)


def _matmul(x, w, out_dtype, tm, tn):
    m, k = x.shape
    n = w.shape[1]
    return pl.pallas_call(
        _mm_kernel,
        grid=(n // tn, m // tm),
        in_specs=[pl.BlockSpec((tm, k), lambda j, i: (i, 0)),
                  pl.BlockSpec((k, tn), lambda j, i: (0, j))],
        out_specs=pl.BlockSpec((tm, tn), lambda j, i: (i, j)),
        out_shape=jax.ShapeDtypeStruct((m, n), out_dtype),
        compiler_params=_cparams("parallel", "parallel"),
        name="matmul",
    )(x, w)


def _mm_res_ln_kernel(x_ref, w_ref, res_ref, g_ref, b_ref, o_ref):
    y = jnp.dot(x_ref[...].astype(BF16), w_ref[...], preferred_element_type=F32)
    o_ref[...] = _layer_norm_rows(DEEPNORM_ALPHA * res_ref[...] + y, g_ref[...], b_ref[...])


def _matmul_res_ln(x, w, res, g, b, tm):
    m, k = x.shape
    n = w.shape[1]
    row = lambda i: (i, 0)
    fix = lambda i: (0, 0)
    return pl.pallas_call(
        _mm_res_ln_kernel,
        grid=(m // tm,),
        in_specs=[pl.BlockSpec((tm, k), row), pl.BlockSpec((k, n), fix), pl.BlockSpec((tm, n), row),
                  pl.BlockSpec((1, n), fix), pl.BlockSpec((1, n), fix)],
        out_specs=pl.BlockSpec((tm, n), row),
        out_shape=jax.ShapeDtypeStruct((m, n), F32),
        compiler_params=_cparams("parallel"),
        name="matmul_res_ln",
    )(x, w, res, g.reshape(1, n), b.reshape(1, n))


def _expand_block_diag(a, row_inner, outer, inner):
    rows, k = a.shape
    cols = outer * S5_OCT * inner
    r = jnp.arange(k)[:, None]
    c = jnp.arange(cols)[None, :]
    rep = ((r // inner == c // (S5_OCT * inner)) & (r % inner == c % inner)).astype(BF16)
    out = jnp.dot(a.astype(BF16), rep, preferred_element_type=BF16)
    g_row = (jnp.arange(rows)[:, None] // row_inner) % S5_OCT
    h_col = (c // inner) % S5_OCT
    return jnp.where(g_row == h_col, out, jnp.zeros_like(out))


def _s5_tables(lam_re, lam_im, log_step, b_re, b_im, c_re, c_im, d_skip):
    hp = lax.Precision.HIGHEST
    g_, n_, c_, q_, o_ = S5_GROUPS, S5_STATE, S5_GROUP_CH, S5_CHUNK, S5_OCT
    dt = jnp.exp(log_step.astype(F32))[:, None]
    lr = jnp.minimum(lam_re.astype(F32), -1e-4)
    li = lam_im.astype(F32)
    mag = jnp.exp(lr * dt)
    ab_re = mag * jnp.cos(li * dt)
    ab_im = mag * jnp.sin(li * dt)
    den = lr * lr + li * li
    f_re = ((ab_re - 1.0) * lr + ab_im * li) / den
    f_im = (ab_im * lr - (ab_re - 1.0) * li) / den
    br, bi = b_re.astype(F32), b_im.astype(F32)
    bbar_re = f_re[..., None] * br - f_im[..., None] * bi
    bbar_im = f_re[..., None] * bi + f_im[..., None] * br
    j = jnp.arange(q_ + 1, dtype=F32)[:, None, None]
    pmag = jnp.exp(j * (lr * dt)[None])
    pw_re = pmag * jnp.cos(j * (li * dt)[None])
    pw_im = pmag * jnp.sin(j * (li * dt)[None])
    cr_t = c_re.astype(F32).transpose(0, 2, 1)
    ci_t = c_im.astype(F32).transpose(0, 2, 1)
    bre_t = bbar_re.transpose(0, 2, 1)
    bim_t = bbar_im.transpose(0, 2, 1)
    w_re = pw_re[:, :, None, :] * bre_t[None] - pw_im[:, :, None, :] * bim_t[None]
    w_im = pw_re[:, :, None, :] * bim_t[None] + pw_im[:, :, None, :] * bre_t[None]
    kj = (jnp.einsum('gno,jgin->jgio', cr_t, w_re[:q_], precision=hp)
          - jnp.einsum('gno,jgin->jgio', ci_t, w_im[:q_], precision=hp))
    kj = kj.at[0].add(d_skip.astype(F32).reshape(g_, c_)[:, :, None] * jnp.eye(c_, dtype=F32)[None])
    kj = kj.transpose(1, 2, 0, 3).reshape(g_ * c_, q_ * c_)
    kts = jnp.stack([jnp.pad(kj[:, :(q_ - s) * c_], ((0, 0), (s * c_, 0))) for s in range(q_)], axis=1)
    kts = kts.reshape(N_OCT, o_ * c_, q_, q_ * c_).transpose(0, 2, 1, 3)
    m_tab = _expand_block_diag(kts.reshape(N_OCT * q_ * o_ * c_, q_ * c_), c_, q_, c_)
    m_tab = m_tab.reshape(N_OCT, q_, o_ * c_, q_ * o_ * c_)
    pst = jnp.concatenate([w_re[:q_][::-1], w_im[:q_][::-1]], axis=-1)
    pst = pst.reshape(q_, N_OCT, o_ * c_, 2 * n_).transpose(1, 0, 2, 3)
    p_tab = _expand_block_diag(pst.reshape(N_OCT * q_ * o_ * c_, 2 * n_), c_, 2, n_)
    p_tab = p_tab.reshape(N_OCT, q_, o_ * c_, 2 * o_ * n_)
    ar_t = pw_re[1:].transpose(1, 2, 0)[..., None]
    ai_t = pw_im[1:].transpose(1, 2, 0)[..., None]
    r_re = cr_t[:, :, None, :] * ar_t - ci_t[:, :, None, :] * ai_t
    r_im = -(cr_t[:, :, None, :] * ai_t + ci_t[:, :, None, :] * ar_t)
    rst = jnp.stack([r_re.reshape(N_OCT, o_ * n_, q_ * c_), r_im.reshape(N_OCT, o_ * n_, q_ * c_)], axis=1)
    r_tab = _expand_block_diag(rst.reshape(N_OCT * 2 * o_ * n_, q_ * c_), n_, q_, c_)
    r_tab = r_tab.reshape(N_OCT, 2 * o_ * n_, q_ * o_ * c_)
    a16_re = pw_re[q_].reshape(N_OCT, o_ * n_)
    a16_im = pw_im[q_].reshape(N_OCT, o_ * n_)
    return m_tab, p_tab, r_tab, a16_re, a16_im


def _s5_chunk_rows(u_ref, s, tr, valid):
    us = u_ref[pl.ds(s, tr, stride=S5_CHUNK), :]
    return jnp.where(valid, us, 0.0).astype(BF16)


def _s5_valid(tr, chunks_per_batch, pad_chunks):
    chunk = pl.program_id(1) * tr + lax.broadcasted_iota(jnp.int32, (tr, 1), 0)
    return (chunk % chunks_per_batch) >= pad_chunks


def _s5_state_kernel(u_ref, p_ref, s_ref, *, tr, chunks_per_batch, pad_chunks):
    valid = _s5_valid(tr, chunks_per_batch, pad_chunks)
    acc = jnp.zeros(s_ref.shape, F32)
    for sp in range(S5_CHUNK // 2):
        us = jnp.concatenate([_s5_chunk_rows(u_ref, 2 * sp, tr, valid),
                              _s5_chunk_rows(u_ref, 2 * sp + 1, tr, valid)], axis=1)
        acc += jnp.dot(us, p_ref[0, sp], preferred_element_type=F32)
    s_ref[...] = acc


def _s5_scan_kernel(s_ref, ar_ref, ai_ref, x_ref, st_ref, *, tc):
    half = st_ref.shape[1] // 2

    @pl.when(pl.program_id(1) == 0)
    def _():
        st_ref[...] = jnp.zeros(st_ref.shape, F32)

    ar = ar_ref[...]
    ai = ai_ref[...]

    def body(c, carry):
        xr, xi = carry
        x_ref[c, :, :half] = xr
        x_ref[c, :, half:] = xi
        s = s_ref[c]
        return ar * xr - ai * xi + s[:, :half], ar * xi + ai * xr + s[:, half:]

    xr, xi = lax.fori_loop(0, tc, body, (st_ref[:, :half], st_ref[:, half:]))
    st_ref[:, :half] = xr
    st_ref[:, half:] = xi


def _gelu_tanh(y):
    return 0.5 * y * (1.0 + jnp.tanh(math.sqrt(2.0 / math.pi) * (y + 0.044715 * (y * y * y))))


def _s5_out_kernel(u_ref, xp_ref, m_ref, r_ref, z_ref, acc_ref, *, tr, chunks_per_batch, pad_chunks):
    valid = _s5_valid(tr, chunks_per_batch, pad_chunks)
    acc_ref[...] = jnp.dot(xp_ref[...].astype(BF16), r_ref[0], preferred_element_type=F32)
    for sp in range(S5_CHUNK // 2):
        us = jnp.concatenate([_s5_chunk_rows(u_ref, 2 * sp, tr, valid),
                              _s5_chunk_rows(u_ref, 2 * sp + 1, tr, valid)], axis=1)
        acc_ref[:, 2 * sp * LANES:] += jnp.dot(us, m_ref[0, sp, :, 2 * sp * LANES:],
                                               preferred_element_type=F32)
    for t in range(S5_CHUNK):
        z_ref[pl.ds(t, tr, stride=S5_CHUNK), :] = _gelu_tanh(acc_ref[:, t * LANES:(t + 1) * LANES])


def _s5_tail_kernel(z_ref, wg_ref, bg_ref, wo_ref, res_ref, g_ref, b_ref, o_ref):
    z = z_ref[...]
    t = jnp.dot(z.astype(BF16), wg_ref[...], preferred_element_type=F32) + bg_ref[...]
    glu = z * jax.nn.sigmoid(t)
    y = jnp.dot(glu.astype(BF16), wo_ref[...], preferred_element_type=F32)
    o_ref[...] = _layer_norm_rows(DEEPNORM_ALPHA * res_ref[...] + y, g_ref[...], b_ref[...])


def _s5_mixer(h, lp, pad, w_in, lam_re, lam_im, log_step, b_re, b_im, c_re, c_im, d_skip,
              w_glu, b_glu, w_out, ln_g, ln_b, tm):
    tp, d = h.shape
    n_batch = tp // lp
    m_tab, p_tab, r_tab, a16_re, a16_im = _s5_tables(lam_re, lam_im, log_step, b_re, b_im,
                                                     c_re, c_im, d_skip)
    u = _matmul(h, w_in.astype(BF16), F32, tm, d)
    n_chunks = tp // S5_CHUNK
    chunks_per_batch = lp // S5_CHUNK
    pad_chunks = pad // S5_CHUNK
    tr = _tile(n_chunks, 344)
    st_w = 2 * S5_OCT * S5_STATE
    kw = dict(tr=tr, chunks_per_batch=chunks_per_batch, pad_chunks=pad_chunks)
    s_all = pl.pallas_call(
        functools.partial(_s5_state_kernel, **kw),
        grid=(N_OCT, n_chunks // tr),
        in_specs=[pl.BlockSpec((tr * S5_CHUNK, LANES), lambda o, i: (i, o)),
                  pl.BlockSpec((1, S5_CHUNK // 2, 2 * LANES, st_w), lambda o, i: (o, 0, 0, 0))],
        out_specs=pl.BlockSpec((tr, st_w), lambda o, i: (i, o)),
        out_shape=jax.ShapeDtypeStruct((n_chunks, N_OCT * st_w), F32),
        compiler_params=_cparams("parallel", "parallel"),
        name="s5_state",
    )(u, p_tab.reshape(N_OCT, S5_CHUNK // 2, 2 * LANES, st_w))
    tc = _tile(chunks_per_batch, 129, 1)
    nt = chunks_per_batch // tc
    x_prev = pl.pallas_call(
        functools.partial(_s5_scan_kernel, tc=tc),
        grid=(n_batch, nt),
        in_specs=[pl.BlockSpec((tc, N_OCT, st_w), lambda b, i: (b * nt + i, 0, 0)),
                  pl.BlockSpec((N_OCT, st_w // 2), lambda b, i: (0, 0)),
                  pl.BlockSpec((N_OCT, st_w // 2), lambda b, i: (0, 0))],
        out_specs=pl.BlockSpec((tc, N_OCT, st_w), lambda b, i: (b * nt + i, 0, 0)),
        out_shape=jax.ShapeDtypeStruct((n_chunks, N_OCT, st_w), F32),
        scratch_shapes=[pltpu.VMEM((N_OCT, st_w), F32)],
        compiler_params=_cparams("arbitrary", "arbitrary"),
        name="s5_scan",
    )(s_all.reshape(n_chunks, N_OCT, st_w), a16_re, a16_im)
    z = pl.pallas_call(
        functools.partial(_s5_out_kernel, **kw),
        grid=(N_OCT, n_chunks // tr),
        in_specs=[pl.BlockSpec((tr * S5_CHUNK, LANES), lambda o, i: (i, o)),
                  pl.BlockSpec((tr, st_w), lambda o, i: (i, o)),
                  pl.BlockSpec((1, S5_CHUNK // 2, 2 * LANES, S5_CHUNK * LANES), lambda o, i: (o, 0, 0, 0)),
                  pl.BlockSpec((1, st_w, S5_CHUNK * LANES), lambda o, i: (o, 0, 0))],
        out_specs=pl.BlockSpec((tr * S5_CHUNK, LANES), lambda o, i: (i, o)),
        out_shape=jax.ShapeDtypeStruct((tp, d), F32),
        scratch_shapes=[pltpu.VMEM((tr, S5_CHUNK * LANES), F32)],
        compiler_params=_cparams("parallel", "parallel"),
        name="s5_out",
    )(u, x_prev.reshape(n_chunks, N_OCT * st_w),
      m_tab.reshape(N_OCT, S5_CHUNK // 2, 2 * LANES, S5_CHUNK * LANES), r_tab)
    row = lambda i: (i, 0)
    fix = lambda i: (0, 0)
    return pl.pallas_call(
        _s5_tail_kernel,
        grid=(tp // tm,),
        in_specs=[pl.BlockSpec((tm, d), row), pl.BlockSpec((d, d), fix), pl.BlockSpec((1, d), fix),
                  pl.BlockSpec((d, d), fix), pl.BlockSpec((tm, d), row),
                  pl.BlockSpec((1, d), fix), pl.BlockSpec((1, d), fix)],
        out_specs=pl.BlockSpec((tm, d), row),
        out_shape=jax.ShapeDtypeStruct((tp, d), F32),
        compiler_params=_cparams("parallel"),
        name="s5_tail",
    )(z, w_glu.astype(BF16), b_glu.reshape(1, d).astype(F32), w_out.astype(BF16), h,
      ln_g.reshape(1, d), ln_b.reshape(1, d))


def _ret_kernel(q_ref, k_ref, v_ref, gate_ref, cos_ref, sin_ref, dm_ref, qd_ref, kd_ref, cd_ref, g_ref,
                o_ref, st_ref, *, chunk, pad):
    c = pl.program_id(2)

    @pl.when(c == 0)
    def _():
        st_ref[...] = jnp.zeros(st_ref.shape, F32)

    cos = cos_ref[...]
    sin = sin_ref[...]
    half = RET_DK // 2

    def rot(t):
        t1 = t[:, :half]
        t2 = t[:, half:]
        return jnp.concatenate([t1 * cos - t2 * sin, t1 * sin + t2 * cos], axis=-1)

    q = rot(q_ref[...].astype(F32))
    k = rot(k_ref[...].astype(F32)) * (RET_DK ** -0.5)
    pos = c * chunk + lax.broadcasted_iota(jnp.int32, (chunk, 1), 0)
    k = jnp.where(pos >= pad, k, 0.0)
    qb = q.astype(BF16)
    vb = v_ref[...]
    scores = lax.dot_general(qb, k.astype(BF16), (((1,), (1,)), ((), ())),
                             preferred_element_type=F32) * dm_ref[0]
    intra = jnp.dot(scores.astype(BF16), vb, preferred_element_type=F32)
    state = st_ref[...]
    inter = jnp.dot(qb, state.astype(BF16), preferred_element_type=F32) * qd_ref[0]
    st_ref[...] = state * cd_ref[0] + lax.dot_general((k * kd_ref[0]).astype(BF16), vb,
                                                      (((0,), (0,)), ((), ())),
                                                      preferred_element_type=F32)
    o = intra + inter
    mu = jnp.mean(o, axis=-1, keepdims=True)
    oc = o - mu
    var = jnp.mean(oc * oc, axis=-1, keepdims=True)
    o = oc * lax.rsqrt(var + LN_EPS) * g_ref[...]
    gate = gate_ref[...].astype(F32)
    o_ref[...] = (gate * jax.nn.sigmoid(gate) * o).astype(o_ref.dtype)


def _retention_mixer(h, lp, pad, w_in, gn_g, w_out, ln_g, ln_b, tm):
    tp, d = h.shape
    n_batch = tp // lp
    chunk = _tile(lp, 384, SEQ_ALIGN)
    ncb = lp // chunk
    qd = RET_HEADS * RET_DK
    vd = RET_HEADS * RET_DV
    proj = _matmul(h, w_in.astype(BF16), BF16, tm, _tile(w_in.shape[1], 1536, LANES))
    half = RET_DK // 2
    pos = jnp.arange(lp, dtype=F32) - pad
    inv_freq = jnp.power(RET_THETA, -jnp.arange(half, dtype=F32) / half)
    ang = pos[:, None] * inv_freq[None, :]
    cos, sin = jnp.cos(ang), jnp.sin(ang)
    log_gamma = jnp.log1p(-jnp.power(2.0, -5.0 - jnp.arange(RET_HEADS, dtype=F32)))
    idx = jnp.arange(chunk, dtype=F32)
    rel = idx[:, None] - idx[None, :]
    dmask = jnp.where(rel[None] >= 0, jnp.exp(log_gamma[:, None, None] * jnp.maximum(rel, 0.0)[None]), 0.0)
    q_decay = jnp.exp(log_gamma[:, None] * (idx[None, :] + 1.0))[:, :, None]
    k_decay = jnp.exp(log_gamma[:, None] * (chunk - 1.0 - idx[None, :]))[:, :, None]
    chunk_decay = jnp.exp(log_gamma * chunk).reshape(RET_HEADS, 1, 1)
    kb = qd // RET_DK
    vb = 2 * qd // RET_DV
    gb = (2 * qd + vd) // RET_DV
    o = pl.pallas_call(
        functools.partial(_ret_kernel, chunk=chunk, pad=pad),
        grid=(n_batch, RET_HEADS, ncb),
        in_specs=[pl.BlockSpec((chunk, RET_DK), lambda b, hh, c: (b * ncb + c, hh)),
                  pl.BlockSpec((chunk, RET_DK), lambda b, hh, c: (b * ncb + c, kb + hh)),
                  pl.BlockSpec((chunk, RET_DV), lambda b, hh, c: (b * ncb + c, vb + hh)),
                  pl.BlockSpec((chunk, RET_DV), lambda b, hh, c: (b * ncb + c, gb + hh)),
                  pl.BlockSpec((chunk, half), lambda b, hh, c: (c, 0)),
                  pl.BlockSpec((chunk, half), lambda b, hh, c: (c, 0)),
                  pl.BlockSpec((1, chunk, chunk), lambda b, hh, c: (hh, 0, 0)),
                  pl.BlockSpec((1, chunk, 1), lambda b, hh, c: (hh, 0, 0)),
                  pl.BlockSpec((1, chunk, 1), lambda b, hh, c: (hh, 0, 0)),
                  pl.BlockSpec((1, 1, 1), lambda b, hh, c: (hh, 0, 0)),
                  pl.BlockSpec((1, RET_DV), lambda b, hh, c: (0, hh))],
        out_specs=pl.BlockSpec((chunk, RET_DV), lambda b, hh, c: (b * ncb + c, hh)),
        out_shape=jax.ShapeDtypeStruct((tp, vd), BF16),
        scratch_shapes=[pltpu.VMEM((RET_DK, RET_DV), F32)],
        compiler_params=_cparams("parallel", "parallel", "arbitrary"),
        name="retention",
    )(proj, proj, proj, proj, cos, sin, dmask, q_decay, k_decay, chunk_decay,
      gn_g.reshape(1, vd).astype(F32))
    return _matmul_res_ln(o, w_out.astype(BF16), h, ln_g, ln_b, tm)


ONES_ROWS = 16


def _diff_kernel(q_ref, k_ref, vt_ref, lam_ref, g_ref, o_ref, vext_ref, m_ref, acc_ref, s0_ref, s1_ref,
                 s2_ref, s3_ref, mx_ref,
                 *, tq, pad, lambda_init):
    i = pl.program_id(2)
    nkb = vext_ref.shape[0]

    @pl.when(i == 0)
    def _():
        for jb in range(nkb):
            vext_ref[jb, 0:DIFF_DV, :] = vt_ref[:, jb * tq:(jb + 1) * tq]
            vext_ref[jb, DIFF_DV:, :] = jnp.ones((ONES_ROWS, tq), BF16)

    q = q_ref[...]
    lane = lax.broadcasted_iota(jnp.int32, q.shape, 1)
    zero = jnp.zeros_like(q)
    q_both = jnp.concatenate([jnp.where(lane < DIFF_DQK, q, zero), jnp.where(lane >= DIFF_DQK, q, zero)],
                             axis=0)
    m_ref[...] = jnp.full(m_ref.shape, MASK_VALUE, F32)
    acc_ref[...] = jnp.zeros(acc_ref.shape, F32)

    def scores(j):
        start = j * tq if isinstance(j, int) else pl.multiple_of(j * tq, tq)
        k = k_ref[pl.ds(start, tq), :]
        return lax.dot_general(k, q_both, (((1,), (1,)), ((), ())), preferred_element_type=F32)

    s_refs = (s0_ref, s1_ref, s2_ref, s3_ref)

    def produce(slot, j):
        s = scores(j)
        s_refs[slot][...] = s.astype(BF16)
        mx_ref[slot] = jnp.max(s, axis=0, keepdims=True)

    def running_max(block_max):
        return jnp.maximum(m_ref[...], block_max).astype(BF16).astype(F32)

    def accumulate(m_new, pe, v):
        m_prev = m_ref[...]
        acc_ref[...] = jnp.exp2(m_prev - m_new) * acc_ref[...] + jnp.dot(v, pe, preferred_element_type=F32)
        m_ref[...] = m_new

    def consume_masked(s, j):
        kpos = j * tq + lax.broadcasted_iota(jnp.int32, (tq, tq), 0)
        qpos = i * tq + lax.broadcasted_iota(jnp.int32, (tq, tq), 1)
        allowed = (kpos <= qpos) & (kpos >= pad)
        s = jnp.where(jnp.concatenate([allowed, allowed], axis=1), s, MASK_VALUE)
        m_new = running_max(jnp.max(s, axis=0, keepdims=True))
        accumulate(m_new, jnp.exp2((s - m_new).astype(BF16)), vext_ref[j])

    def consume(slot, j):
        m_new = running_max(mx_ref[slot])
        accumulate(m_new, jnp.exp2(s_refs[slot][...] - m_new.astype(BF16)), vext_ref[j])

    def consume2(slot_a, slot_b, j):
        m_new = running_max(jnp.maximum(mx_ref[slot_a], mx_ref[slot_b]))
        m_bf = m_new.astype(BF16)
        pe = jnp.concatenate([jnp.exp2(s_refs[slot_a][...] - m_bf), jnp.exp2(s_refs[slot_b][...] - m_bf)],
                             axis=0)
        accumulate(m_new, pe, jnp.concatenate([vext_ref[j], vext_ref[j + 1]], axis=1))

    @pl.when(i == 0)
    def _():
        consume_masked(scores(0), 0)

    @pl.when(i > 0)
    def _():
        n = i - 1
        n_quads = n // 4
        s3_ref[...] = scores(0).astype(BF16)
        produce(0, 1)
        produce(1, jnp.minimum(2, i))
        consume_masked(s3_ref[...].astype(F32), 0)

        def quad(u, carry):
            a = 1 + 4 * u
            produce(2, a + 2)
            produce(3, a + 3)
            consume2(0, 1, a)
            produce(0, jnp.minimum(a + 4, i))
            produce(1, jnp.minimum(a + 5, i))
            consume2(2, 3, a + 2)
            return carry

        lax.fori_loop(0, n_quads, quad, 0)
        first = 1 + 4 * n_quads

        def pair(jj, carry):
            a = first + 2 * jj
            produce(1, a + 1)
            consume(0, a)
            produce(0, a + 2)
            consume(1, a + 1)
            return carry

        lax.fori_loop(0, (n - 4 * n_quads) // 2, pair, 0)

        @pl.when(n % 2 == 1)
        def _():
            produce(1, i)
            consume(0, i - 1)
            consume_masked(s1_ref[...].astype(F32), i)

        @pl.when(n % 2 == 0)
        def _():
            consume_masked(s0_ref[...].astype(F32), i)

    lam_p = lam_ref[...]
    lam = (jnp.exp(jnp.sum(lam_p[0:1] * lam_p[1:2], axis=-1, keepdims=True))
           - jnp.exp(jnp.sum(lam_p[2:3] * lam_p[3:4], axis=-1, keepdims=True)) + lambda_init)
    acc = acc_ref[...]
    o_both = acc[0:DIFF_DV] / acc[DIFF_DV:DIFF_DV + 1]
    o = o_both[:, :tq] - lam * o_both[:, tq:]
    o = o * lax.rsqrt(jnp.mean(o * o, axis=0, keepdims=True) + LN_EPS)
    o = o * (g_ref[...] * (1.0 - lambda_init))
    o_ref[...] = o.T.astype(o_ref.dtype)


def _mm_nt_kernel(wt_ref, x_ref, o_ref):
    o_ref[...] = lax.dot_general(wt_ref[...], x_ref[...].astype(BF16), (((1,), (1,)), ((), ())),
                                 preferred_element_type=F32).astype(o_ref.dtype)


def _matmul_nt(wt, x, out_dtype, tm):
    n, k = wt.shape
    m = x.shape[0]
    return pl.pallas_call(
        _mm_nt_kernel,
        grid=(m // tm,),
        in_specs=[pl.BlockSpec((n, k), lambda i: (0, 0)), pl.BlockSpec((tm, k), lambda i: (i, 0))],
        out_specs=pl.BlockSpec((n, tm), lambda i: (0, i)),
        out_shape=jax.ShapeDtypeStruct((n, m), out_dtype),
        compiler_params=_cparams("parallel"),
        name="matmul_nt",
    )(wt, x)


def _diff_attn_mixer(h, lp, pad, w_in, lq1, lk1, lq2, lk2, subln_g, w_out, lambda_init, ln_g, ln_b, tm):
    tp, d = h.shape
    n_batch = tp // lp
    qk = DIFF_HEADS * 2 * DIFF_DQK
    col_scale = jnp.concatenate([jnp.full((qk,), DIFF_DQK ** -0.5 * math.log2(math.e), F32),
                                 jnp.ones((qk,), F32)])
    proj = _matmul(h, (w_in[:, :2 * qk] * col_scale[None, :]).astype(BF16), BF16, tm, _tile(2 * qk, 2048, LANES))
    vt = _matmul_nt(w_in[:, 2 * qk:].T.astype(BF16), h, BF16, _tile(tp, 768, LANES))
    tq = _tile(lp, 384, SEQ_ALIGN)
    nq = lp // tq
    lam_p = jnp.stack([lq1, lk1, lq2, lk2]).astype(F32)
    o = pl.pallas_call(
        functools.partial(_diff_kernel, tq=tq, pad=pad, lambda_init=lambda_init),
        grid=(n_batch, DIFF_HEADS, nq),
        in_specs=[pl.BlockSpec((tq, LANES), lambda b, hh, i: (b * nq + i, hh)),
                  pl.BlockSpec((lp, LANES), lambda b, hh, i: (b, DIFF_HEADS + hh)),
                  pl.BlockSpec((DIFF_DV, lp), lambda b, hh, i: (hh, b)),
                  pl.BlockSpec((4, DIFF_DQK), lambda b, hh, i: (0, 0)),
                  pl.BlockSpec((DIFF_DV, 1), lambda b, hh, i: (0, 0))],
        out_specs=pl.BlockSpec((tq, DIFF_DV), lambda b, hh, i: (b * nq + i, hh)),
        out_shape=jax.ShapeDtypeStruct((tp, DIFF_HEADS * DIFF_DV), BF16),
        scratch_shapes=[pltpu.VMEM((nq, DIFF_DV + ONES_ROWS, tq), BF16),
                        pltpu.VMEM((1, 2 * tq), F32),
                        pltpu.VMEM((DIFF_DV + ONES_ROWS, 2 * tq), F32),
                        ] + [pltpu.VMEM((tq, 2 * tq), BF16)] * 4 + [pltpu.VMEM((4, 1, 2 * tq), F32)],
        compiler_params=_cparams("parallel", "parallel", "arbitrary"),
        name="diff_attn",
    )(proj, proj, vt, lam_p, subln_g.reshape(DIFF_DV, 1).astype(F32))
    return _matmul_res_ln(o, w_out.astype(BF16), h, ln_g, ln_b, tm)


SORT_ALIGN = 16
SORT_CHUNK = 512


def _router_kernel(x_ref, w_ref, b_ref, idx_ref, gate_ref, rank_ref, cnt_ref, *, tm):
    x = x_ref[...]
    x_hi = x.astype(BF16)
    x_lo = (x - x_hi.astype(F32)).astype(BF16)
    hi = jnp.dot(x_hi, w_ref[...], preferred_element_type=F32)
    lo = jnp.dot(x_lo, w_ref[:, :LANES], preferred_element_type=F32)
    logits = hi[:, :LANES] + hi[:, LANES:] + lo + b_ref[...]
    lane = lax.broadcasted_iota(jnp.int32, logits.shape, 1).astype(F32)
    work = jnp.where(lane < N_EXPERTS, logits, -jnp.inf)
    vals, idxs = [], []
    picked = jnp.zeros(logits.shape, F32)
    for _ in range(TOP_K):
        m = jnp.max(work, axis=-1, keepdims=True)
        sel = jnp.min(jnp.where(work == m, lane, float(LANES)), axis=-1, keepdims=True)
        hit = lane == sel
        work = jnp.where(hit, -jnp.inf, work)
        picked = jnp.where(hit, 1.0, picked)
        vals.append(m)
        idxs.append(sel)
    exps = [jnp.exp(v - vals[0]) for v in vals]
    tot = exps[0] + exps[1] + exps[2] + exps[3]
    r = lax.broadcasted_iota(jnp.int32, (tm, tm), 0)
    c = lax.broadcasted_iota(jnp.int32, (tm, tm), 1)
    lower = jnp.where(c < r, 1.0, 0.0).astype(BF16)
    before = jnp.dot(lower, picked.astype(BF16), preferred_element_type=F32)
    for kk in range(TOP_K):
        idx_ref[:, kk:kk + 1] = idxs[kk].astype(jnp.int32)
        gate_ref[:, kk:kk + 1] = exps[kk] / tot
        rank_ref[:, kk:kk + 1] = jnp.sum(jnp.where(lane == idxs[kk], before, 0.0), axis=-1,
                                         keepdims=True).astype(jnp.int32)
    cnt_ref[0] = jnp.sum(picked, axis=0, keepdims=True)


BIG_CHUNKS = 4


def _row_dmas(n_chunks, make_copy, wait):
    n_big = n_chunks // BIG_CHUNKS
    big = BIG_CHUNKS * SORT_ALIGN

    def run(cp):
        if wait:
            cp.wait()
        else:
            cp.start()

    def big_body(c, carry):
        run(make_copy(c * big, big))
        return carry

    def small_body(c, carry):
        run(make_copy(n_big * big + c * SORT_ALIGN, SORT_ALIGN))
        return carry

    lax.fori_loop(0, n_big, big_body, 0)
    lax.fori_loop(0, n_chunks - n_big * BIG_CHUNKS, small_body, 0)


def _segment_dma(tab_ref, n_seg, tile, local_ref, slots_ref, sem, to_slots, wait):
    def per_expert(e, carry):
        lo0 = tab_ref[tile * N_EXPERTS + e]
        go0 = tab_ref[n_seg + tile * N_EXPERTS + e]

        def make_copy(row, rows):
            local = local_ref.at[pl.ds(pl.multiple_of(lo0 + row, SORT_ALIGN), rows), :]
            slots = slots_ref.at[pl.ds(pl.multiple_of(go0 + row, SORT_ALIGN), rows), :]
            return (pltpu.make_async_copy(local, slots, sem) if to_slots
                    else pltpu.make_async_copy(slots, local, sem))

        _row_dmas(tab_ref[2 * n_seg + tile * N_EXPERTS + e], make_copy, wait)
        return carry
    lax.fori_loop(0, N_EXPERTS, per_expert, 0)


def _zero_fill_dma(tab_ref, n_seg, n_blocks, zero_ref, slots_ref, sem, wait):
    bm = zero_ref.shape[0]

    def per_expert(e, carry):
        off = tab_ref[3 * n_seg + e]

        def make_copy(row, rows):
            dst = slots_ref.at[pl.ds(pl.multiple_of(off + row, SORT_ALIGN), rows), :]
            return pltpu.make_async_copy(zero_ref.at[pl.ds(0, rows), :], dst, sem)

        _row_dmas(tab_ref[3 * n_seg + N_EXPERTS + e], make_copy, wait)
        return carry
    lax.fori_loop(0, N_EXPERTS, per_expert, 0)

    def per_block(blk, carry):
        cp = pltpu.make_async_copy(zero_ref, slots_ref.at[pl.ds(pl.multiple_of(blk * bm, bm), bm), :], sem)
        if wait:
            cp.wait()
        else:
            cp.start()
        return carry
    lax.fori_loop(tab_ref[3 * n_seg + 2 * N_EXPERTS], n_blocks, per_block, 0)


def _dispatch_kernel(tab_ref, x_ref, lrow_ref, slots_ref, sorted_ref, zero_ref, sem, zero_sem,
                     *, tm, rs, n_blocks):
    i = pl.program_id(0)
    n_seg = pl.num_programs(0) * N_EXPERTS
    slot = i % 2
    buf = sorted_ref.at[slot]

    @pl.when(i == 0)
    def _():
        zero_ref[...] = jnp.zeros(zero_ref.shape, BF16)
        _zero_fill_dma(tab_ref, n_seg, n_blocks, zero_ref, slots_ref, zero_sem, wait=False)
        _zero_fill_dma(tab_ref, n_seg, n_blocks, zero_ref, slots_ref, zero_sem, wait=True)

    xb = x_ref[...].astype(BF16)
    lr = lrow_ref[...]
    for rc in range(rs // SORT_CHUNK):
        rows = rc * SORT_CHUNK + lax.broadcasted_iota(jnp.int32, (SORT_CHUNK, tm), 0)
        hit = (rows == lr[0:1]) | (rows == lr[1:2]) | (rows == lr[2:3]) | (rows == lr[3:4])
        perm = jnp.where(hit, 1.0, 0.0).astype(BF16)
        buf[rc * SORT_CHUNK:(rc + 1) * SORT_CHUNK, :] = jnp.dot(
            perm, xb, preferred_element_type=F32).astype(BF16)
    _segment_dma(tab_ref, n_seg, i, buf, slots_ref, sem.at[slot], True, wait=False)

    @pl.when(i > 0)
    def _():
        _segment_dma(tab_ref, n_seg, i - 1, sorted_ref.at[1 - slot], slots_ref, sem.at[1 - slot], True, wait=True)

    @pl.when(i == pl.num_programs(0) - 1)
    def _():
        _segment_dma(tab_ref, n_seg, i, buf, slots_ref, sem.at[slot], True, wait=True)


def _combine_kernel(tab_ref, res_ref, lrow_ref, gate_ref, g_ref, b_ref, y_ref, o_ref, ysort_ref, sem,
                    *, tm, rs):
    i = pl.program_id(0)
    n_seg = pl.num_programs(0) * N_EXPERTS
    slot = i % 2

    @pl.when(i == 0)
    def _():
        ysort_ref[...] = jnp.zeros(ysort_ref.shape, BF16)
        _segment_dma(tab_ref, n_seg, 0, ysort_ref.at[0], y_ref, sem.at[0], False, wait=False)

    @pl.when(i + 1 < pl.num_programs(0))
    def _():
        _segment_dma(tab_ref, n_seg, i + 1, ysort_ref.at[1 - slot], y_ref, sem.at[1 - slot], False, wait=False)

    _segment_dma(tab_ref, n_seg, i, ysort_ref.at[slot], y_ref, sem.at[slot], False, wait=True)
    ysort = ysort_ref.at[slot]
    lr = lrow_ref[...]
    gt = gate_ref[...]
    y = jnp.zeros(res_ref.shape, F32)
    for rc in range(rs // SORT_CHUNK):
        cols = rc * SORT_CHUNK + lax.broadcasted_iota(jnp.int32, (tm, SORT_CHUNK), 1)
        w = jnp.zeros((tm, SORT_CHUNK), F32)
        for kk in range(TOP_K):
            w = jnp.where(cols == lr[:, kk:kk + 1], gt[:, kk:kk + 1], w)
        y += jnp.dot(w.astype(BF16), ysort[rc * SORT_CHUNK:(rc + 1) * SORT_CHUNK, :],
                     preferred_element_type=F32)
    o_ref[...] = _layer_norm_rows(DEEPNORM_ALPHA * res_ref[...] + y, g_ref[...], b_ref[...])


def _expert_kernel(be_ref, nu_ref, x_ref, wu_ref, bu_ref, wd_ref, bd_ref, o_ref, wu_bf_ref, wd_bf_ref):
    i = pl.program_id(0)
    live = i < nu_ref[0]

    @pl.when(live & ((i == 0) | (be_ref[i] != be_ref[jnp.maximum(i - 1, 0)])))
    def _():
        wu_bf_ref[...] = wu_ref[0, 0].astype(BF16)
        wd_bf_ref[...] = wd_ref[0, 0].astype(BF16)

    @pl.when(live)
    def _():
        hb = jnp.dot(x_ref[...], wu_bf_ref[...], preferred_element_type=F32) + bu_ref[0]
        x_glu = jnp.minimum(hb[:, :D_EXPERT], SWIGLU_LIMIT)
        x_lin = jnp.clip(hb[:, D_EXPERT:], -SWIGLU_LIMIT, SWIGLU_LIMIT)
        act = x_glu * jax.nn.sigmoid(SWIGLU_ALPHA * x_glu) * (x_lin + 1.0)
        y = jnp.dot(act.astype(BF16), wd_bf_ref[...], preferred_element_type=F32) + bd_ref[0]
        o_ref[...] = y.astype(o_ref.dtype)

    @pl.when(jnp.logical_not(live))
    def _():
        o_ref[...] = jnp.zeros(o_ref.shape, o_ref.dtype)


def _moe_ffn(h, layer, w_router, b_router, w_up_all, b_up, w_down_all, b_down, ln_g, ln_b, tm, bm):
    tp, d = h.shape
    a = tp * TOP_K
    wr = jnp.zeros((d, LANES), F32).at[:, :N_EXPERTS].set(w_router.astype(F32))
    wr_hi = wr.astype(BF16)
    wr = jnp.concatenate([wr_hi, (wr - wr_hi.astype(F32)).astype(BF16)], axis=1)
    br = jnp.zeros((1, LANES), F32).at[0, :N_EXPERTS].set(b_router.astype(F32))
    row = lambda i: (i, 0)
    fix = lambda i: (0, 0)
    nt = tp // tm
    idx, gates, rank, cnt = pl.pallas_call(
        functools.partial(_router_kernel, tm=tm),
        grid=(nt,),
        in_specs=[pl.BlockSpec((tm, d), row), pl.BlockSpec((d, 2 * LANES), fix), pl.BlockSpec((1, LANES), fix)],
        out_specs=[pl.BlockSpec((tm, TOP_K), row), pl.BlockSpec((tm, TOP_K), row),
                   pl.BlockSpec((tm, TOP_K), row), pl.BlockSpec((1, 1, LANES), lambda i: (i, 0, 0))],
        out_shape=[jax.ShapeDtypeStruct((tp, TOP_K), jnp.int32), jax.ShapeDtypeStruct((tp, TOP_K), F32),
                   jax.ShapeDtypeStruct((tp, TOP_K), jnp.int32), jax.ShapeDtypeStruct((nt, 1, LANES), F32)],
        compiler_params=_cparams("parallel"),
        name="router",
    )(h, wr, br)
    counts = cnt[:, 0, :N_EXPERTS].astype(jnp.int32)
    seg = ((counts + SORT_ALIGN - 1) // SORT_ALIGN) * SORT_ALIGN
    tile_off = jnp.cumsum(seg, axis=0) - seg
    region = ((jnp.sum(seg, axis=0) + bm - 1) // bm) * bm
    region_end = jnp.cumsum(region)
    slot_off = (region_end - region)[None, :] + tile_off
    local_off = jnp.cumsum(seg, axis=1) - seg
    used = jnp.sum(seg, axis=0)
    n_used = (region_end[-1:] // bm).astype(jnp.int32)
    table = jnp.concatenate([local_off.reshape(-1), slot_off.reshape(-1), (seg // SORT_ALIGN).reshape(-1),
                             region_end - region + used, (region - used) // SORT_ALIGN,
                             n_used]).astype(jnp.int32)
    onehot = idx.reshape(nt, tm, TOP_K, 1) == jnp.arange(N_EXPERTS, dtype=jnp.int32)
    lrow = jnp.sum(jnp.where(onehot, local_off[:, None, None, :], 0), axis=-1).reshape(tp, TOP_K) + rank
    lrow_t = jnp.concatenate([lrow.T, jnp.full((8 - TOP_K, tp), -1, jnp.int32)], axis=0)
    rs = -(-(TOP_K * tm + N_EXPERTS * (SORT_ALIGN - 1)) // SORT_CHUNK) * SORT_CHUNK
    n_blocks = -(-(a + nt * N_EXPERTS * (SORT_ALIGN - 1)) // bm) + N_EXPERTS
    n_slots = n_blocks * bm
    block_start = jnp.arange(n_blocks, dtype=jnp.int32) * bm
    block_e = jnp.minimum(jnp.sum((region_end[None, :] <= block_start[:, None]).astype(jnp.int32), axis=1),
                          N_EXPERTS - 1)
    x_slots = pl.pallas_call(
        functools.partial(_dispatch_kernel, tm=tm, rs=rs, n_blocks=n_blocks),
        grid_spec=pltpu.PrefetchScalarGridSpec(
            num_scalar_prefetch=1,
            grid=(nt,),
            in_specs=[pl.BlockSpec((tm, d), lambda i, tab: (i, 0)),
                      pl.BlockSpec((8, tm), lambda i, tab: (0, i))],
            out_specs=pl.BlockSpec(memory_space=pl.ANY),
            scratch_shapes=[pltpu.VMEM((2, rs, d), BF16), pltpu.VMEM((bm, d), BF16),
                            pltpu.SemaphoreType.DMA((2,)), pltpu.SemaphoreType.DMA(())],
        ),
        out_shape=jax.ShapeDtypeStruct((n_slots, d), BF16),
        compiler_params=_cparams("arbitrary"),
        name="moe_dispatch",
    )(table, h, lrow_t)
    y_slots = pl.pallas_call(
        _expert_kernel,
        grid_spec=pltpu.PrefetchScalarGridSpec(
            num_scalar_prefetch=2,
            grid=(n_blocks,),
            in_specs=[pl.BlockSpec((bm, d), lambda i, be, nu: (i, 0)),
                      pl.BlockSpec((1, 1, d, 2 * D_EXPERT), lambda i, be, nu: (layer, be[i], 0, 0)),
                      pl.BlockSpec((1, 1, 2 * D_EXPERT), lambda i, be, nu: (be[i], 0, 0)),
                      pl.BlockSpec((1, 1, D_EXPERT, d), lambda i, be, nu: (layer, be[i], 0, 0)),
                      pl.BlockSpec((1, 1, d), lambda i, be, nu: (be[i], 0, 0))],
            out_specs=pl.BlockSpec((bm, d), lambda i, be, nu: (i, 0)),
            scratch_shapes=[pltpu.VMEM((d, 2 * D_EXPERT), BF16), pltpu.VMEM((D_EXPERT, d), BF16)],
        ),
        out_shape=jax.ShapeDtypeStruct((n_slots, d), BF16),
        compiler_params=_cparams("arbitrary"),
        name="experts",
    )(block_e, n_used, x_slots, w_up_all, b_up.reshape(N_EXPERTS, 1, 2 * D_EXPERT).astype(F32),
      w_down_all, b_down.reshape(N_EXPERTS, 1, d).astype(F32))
    return pl.pallas_call(
        functools.partial(_combine_kernel, tm=tm, rs=rs),
        grid_spec=pltpu.PrefetchScalarGridSpec(
            num_scalar_prefetch=1,
            grid=(nt,),
            in_specs=[pl.BlockSpec((tm, d), lambda i, tab: (i, 0)),
                      pl.BlockSpec((tm, TOP_K), lambda i, tab: (i, 0)),
                      pl.BlockSpec((tm, TOP_K), lambda i, tab: (i, 0)),
                      pl.BlockSpec((1, d), lambda i, tab: (0, 0)),
                      pl.BlockSpec((1, d), lambda i, tab: (0, 0)),
                      pl.BlockSpec(memory_space=pl.ANY)],
            out_specs=pl.BlockSpec((tm, d), lambda i, tab: (i, 0)),
            scratch_shapes=[pltpu.VMEM((2, rs, d), BF16), pltpu.SemaphoreType.DMA((2,))],
        ),
        out_shape=jax.ShapeDtypeStruct((tp, d), F32),
        compiler_params=_cparams("arbitrary"),
        name="moe_combine_ln",
    )(table, h, lrow, gates, ln_g.reshape(1, d), ln_b.reshape(1, d), y_slots)


def kernel(x, meta_tokens, s5_w_in, s5_lambda_re, s5_lambda_im, s5_log_step, s5_b_re, s5_b_im, s5_c_re, s5_c_im, s5_d, s5_w_glu, s5_b_glu, s5_w_out, ret_w_in, ret_gn_g, ret_w_out, diff_w_in, diff_lambda_q1, diff_lambda_k1, diff_lambda_q2, diff_lambda_k2, diff_subln_g, diff_w_out, ln_mix_g, ln_mix_b, moe_w_router, moe_b_router, moe_w_up, moe_b_up, moe_w_down, moe_b_down, ln_ffn_g, ln_ffn_b):
    n_batch, seq, d = x.shape
    length = seq + N_META
    pad = (-length) % SEQ_ALIGN
    lp = length + pad
    assert pad % S5_CHUNK == 0 and d == D_MODEL
    tp = n_batch * lp
    tm = _tile(tp, 768)
    meta = jnp.broadcast_to(meta_tokens[None].astype(x.dtype), (n_batch, N_META, d))
    h = jnp.concatenate([jnp.zeros((n_batch, pad, d), x.dtype), meta, x], axis=1).reshape(tp, d)
    for i in range(DEPTH):
        kind = i % N_MIXERS
        j = i // N_MIXERS
        if kind == 0:
            h = _s5_mixer(h, lp, pad, s5_w_in[j], s5_lambda_re[j], s5_lambda_im[j], s5_log_step[j],
                          s5_b_re[j], s5_b_im[j], s5_c_re[j], s5_c_im[j], s5_d[j],
                          s5_w_glu[j], s5_b_glu[j], s5_w_out[j], ln_mix_g[i], ln_mix_b[i], tm)
        elif kind == 1:
            h = _retention_mixer(h, lp, pad, ret_w_in[j], ret_gn_g[j], ret_w_out[j],
                                 ln_mix_g[i], ln_mix_b[i], tm)
        else:
            lambda_init = 0.8 - 0.6 * math.exp(-0.3 * i)
            h = _diff_attn_mixer(h, lp, pad, diff_w_in[j], diff_lambda_q1[j], diff_lambda_k1[j],
                                 diff_lambda_q2[j], diff_lambda_k2[j], diff_subln_g[j], diff_w_out[j],
                                 lambda_init, ln_mix_g[i], ln_mix_b[i], tm)
        h = _moe_ffn(h, i, moe_w_router[i], moe_b_router[i], moe_w_up, moe_b_up[i],
                     moe_w_down, moe_b_down[i], ln_ffn_g[i], ln_ffn_b[i], tm, MOE_BLOCK)
    return h.reshape(n_batch, lp, d)[:, pad + N_META:, :]
```
